```python
import math
import jax
import jax.numpy as jnp
from jax import lax
import numpy as np

D_MODEL = 2048
BATCH = 4
SEQ = 2048
DEPTH = 2

F32 = jnp.float32
EPS = 1e-6
MIX_WIDTH = D_MODEL
HALF = MIX_WIDTH // 2
N_EVEN = (DEPTH + 1) // 2
N_ODD = DEPTH // 2

GDN_HEADS = 8
GDN_DK = HALF // GDN_HEADS
GDN_DV = HALF // GDN_HEADS
GDN_CONV = 4
GDN_CHUNK = 64
S5_GROUP = 16
S5_GROUPS = HALF // S5_GROUP
S5_STATE = 64
RWKV_HEAD = 64
RWKV_HEADS = HALF // RWKV_HEAD
RWKV_DECAY_LORA = 64
RWKV_ICLR_LORA = 64
RWKV_GATE_LORA = 160
RWKV_GN_EPS = 64e-5
RET_HEADS = 4
RET_DK = HALF // 2 // RET_HEADS
RET_DV = HALF // RET_HEADS
RET_CHUNK = 128
ROPE_BASE = 10000.0
D_FF = 256 * ((8 * D_MODEL // 3 + 255) // 256)
FFN_CONV = 3

EVEN_COLS = (HALF, HALF, HALF, HALF, GDN_HEADS, GDN_HEADS, HALF)
RWKV_COLS = (HALF, HALF, HALF, RWKV_DECAY_LORA, RWKV_ICLR_LORA, RWKV_GATE_LORA)
RWKV_WIDTH = sum(RWKV_COLS)
RET_COLS = (RET_HEADS * RET_DK, RET_HEADS * RET_DK, RET_HEADS * RET_DV, HALF)
EVEN_PROJ = sum(EVEN_COLS)
ODD_PROJ = RWKV_WIDTH + sum(RET_COLS)

kernel_name = 'hybrid_gdn_s5_rwkv7_retention_trunk'


def rms_norm(x, w, eps=EPS):
    xf = x.astype(F32)
    xf = xf * lax.rsqrt(jnp.mean(xf * xf, axis=-1, keepdims=True) + eps)
    return xf * w.astype(F32)


def l2norm(t, eps=1e-6):
    return t * lax.rsqrt(jnp.sum(t * t, axis=-1, keepdims=True) + eps)


def split_cols(z, sizes):
    idx = [int(i) for i in np.cumsum(sizes)[:-1]]
    return jnp.split(z, idx, axis=-1)


def causal_dwconv(x, w):
    k = w.shape[0]
    return lax.conv_general_dilated(
        x, w.astype(x.dtype)[:, None, :], window_strides=(1,), padding=[(k - 1, 0)],
        dimension_numbers=('NWC', 'WIO', 'NWC'), feature_group_count=x.shape[-1])


def token_shift(z):
    return jnp.pad(z, ((0, 0), (1, 0), (0, 0)))[:, :-1]


def to_chunks(t, c):
    b, T, h = t.shape[:3]
    return jnp.moveaxis(t.reshape(b, T // c, c, h, *t.shape[3:]), 2, 3)


def from_chunks(t):
    b, n, h, c = t.shape[:4]
    return jnp.moveaxis(t, 3, 2).reshape(b, n * c, h, *t.shape[4:])


def gated_deltanet(q, k, v, gate, beta_raw, alpha_raw, conv_w, a_log, dt_bias, norm_w):
    b, T, _ = q.shape
    qkv = jax.nn.silu(causal_dwconv(jnp.concatenate([q, k, v], axis=-1).astype(F32), conv_w))
    q, k, v = jnp.split(qkv, 3, axis=-1)
    q = l2norm(q.reshape(b, T, GDN_HEADS, GDN_DK)) * (GDN_DK ** -0.5)
    k = l2norm(k.reshape(b, T, GDN_HEADS, GDN_DK))
    v = v.reshape(b, T, GDN_HEADS, GDN_DV)
    beta = jax.nn.sigmoid(beta_raw.astype(F32))
    g = -jnp.exp(a_log.astype(F32)) * jax.nn.softplus(alpha_raw.astype(F32) + dt_bias.astype(F32))
    c = GDN_CHUNK
    qc, kc, vc = to_chunks(q, c), to_chunks(k, c), to_chunks(v, c)
    gc = jnp.cumsum(to_chunks(g, c), axis=-1)
    bc = to_chunks(beta, c)
    causal = jnp.tril(jnp.ones((c, c), dtype=bool))
    strict = jnp.tril(jnp.ones((c, c), dtype=bool), -1)
    decay = jnp.exp(jnp.where(causal, gc[..., :, None] - gc[..., None, :], -jnp.inf))
    kb = kc * bc[..., None]
    lmat = jnp.where(strict, jnp.einsum('bnhid,bnhjd->bnhij', kb, kc) * decay, 0.0) + jnp.eye(c, dtype=F32)
    rhs = jnp.concatenate([vc * bc[..., None], kb * jnp.exp(gc)[..., None]], axis=-1)
    sol = lax.linalg.triangular_solve(lmat, rhs, left_side=True, lower=True, unit_diagonal=True)
    u, w = sol[..., :GDN_DV], sol[..., GDN_DV:]
    attn = jnp.einsum('bnhid,bnhjd->bnhij', qc, kc) * decay
    qg = qc * jnp.exp(gc)[..., None]
    g_last = gc[..., -1]
    kdec = kc * jnp.exp(g_last[..., None] - gc)[..., None]

    def step(S, xs):
        qg_n, kdec_n, u_n, w_n, attn_n, gl_n = xs
        v_new = u_n - jnp.einsum('bhcd,bhde->bhce', w_n, S)
        o = jnp.einsum('bhcd,bhde->bhce', qg_n, S) + jnp.einsum('bhij,bhje->bhie', attn_n, v_new)
        S = S * jnp.exp(gl_n)[..., None, None] + jnp.einsum('bhcd,bhce->bhde', kdec_n, v_new)
        return S, o

    xs = tuple(jnp.moveaxis(t, 1, 0) for t in (qg, kdec, u, w, attn, g_last))
    S0 = jnp.zeros((b, GDN_HEADS, GDN_DK, GDN_DV), F32)
    _, o = lax.scan(step, S0, xs)
    o = from_chunks(jnp.moveaxis(o, 0, 1))
    o = rms_norm(o, norm_w) * jax.nn.silu(gate.astype(F32).reshape(b, T, GDN_HEADS, GDN_DV))
    return o.reshape(b, T, GDN_HEADS * GDN_DV)


def s5(u, lam_re, lam_im, b_re, b_im, c_re, c_im, d_skip, log_step, w_glu):
    bsz, T, _ = u.shape
    u = u.astype(F32).reshape(bsz, T, S5_GROUPS, S5_GROUP)
    dt = jnp.exp(log_step.astype(F32))[:, None]
    lr, li = lam_re.astype(F32), lam_im.astype(F32)
    mag = jnp.exp(lr * dt)
    ang = li * dt
    ab_re, ab_im = mag * jnp.cos(ang), mag * jnp.sin(ang)
    den = lr * lr + li * li
    nr = ab_re - 1.0
    f_re = (nr * lr + ab_im * li) / den
    f_im = (ab_im * lr - nr * li) / den
    br, bi = b_re.astype(F32), b_im.astype(F32)
    bb_re = f_re[..., None] * br - f_im[..., None] * bi
    bb_im = f_re[..., None] * bi + f_im[..., None] * br
    bu_re = jnp.einsum('btgc,gpc->btgp', u, bb_re)
    bu_im = jnp.einsum('btgc,gpc->btgp', u, bb_im)
    a_re = jnp.broadcast_to(ab_re, bu_re.shape)
    a_im = jnp.broadcast_to(ab_im, bu_im.shape)

    def combine(e1, e2):
        a1r, a1i, b1r, b1i = e1
        a2r, a2i, b2r, b2i = e2
        return (a2r * a1r - a2i * a1i, a2r * a1i + a2i * a1r,
                a2r * b1r - a2i * b1i + b2r, a2r * b1i + a2i * b1r + b2i)

    _, _, xr, xi = lax.associative_scan(combine, (a_re, a_im, bu_re, bu_im), axis=1)
    y = (jnp.einsum('btgp,gcp->btgc', xr, c_re.astype(F32))
         - jnp.einsum('btgp,gcp->btgc', xi, c_im.astype(F32))
         + u * d_skip.astype(F32).reshape(S5_GROUPS, S5_GROUP))
    y = jax.nn.gelu(y.reshape(bsz, T, HALF))
    return y * jax.nn.sigmoid(y @ w_glu.astype(F32))


def rwkv7(zc, shift_mu, w0, w2, a0, a2, g2, k_k, k_a, r_k, ln_w, ln_b):
    bsz, T, _ = zc.shape
    zc = zc.astype(F32)
    zc = zc + (token_shift(zc) - zc) * shift_mu.astype(F32)
    r, k, v, w_lr, a_lr, g_lr = split_cols(zc, RWKV_COLS)
    w = -jax.nn.softplus(-(w0.astype(F32) + jnp.tanh(w_lr) @ w2.astype(F32))) - 0.5
    decay = jnp.exp(-jnp.exp(w))
    a = jax.nn.sigmoid(a0.astype(F32) + a_lr @ a2.astype(F32))
    g = jax.nn.sigmoid(g_lr) @ g2.astype(F32)
    hs = lambda t: t.reshape(bsz, T, RWKV_HEADS, RWKV_HEAD)
    kk = l2norm(hs(k * k_k.astype(F32)))
    k = k * (1.0 + (a - 1.0) * k_a.astype(F32))
    r_h, k_h, v_h, w_h, a_h = hs(r), hs(k), hs(v), hs(decay), hs(a)
    b_h = kk * a_h

    def step(S, xs):
        r_t, w_t, k_t, v_t, kk_t, b_t = xs
        sa = jnp.einsum('bhij,bhj->bhi', S, -kk_t)
        S = S * w_t[:, :, None, :] + sa[..., None] * b_t[:, :, None, :] + v_t[..., None] * k_t[:, :, None, :]
        return S, jnp.einsum('bhij,bhj->bhi', S, r_t)

    xs = tuple(jnp.moveaxis(t, 1, 0) for t in (r_h, w_h, k_h, v_h, kk, b_h))
    S0 = jnp.zeros((bsz, RWKV_HEADS, RWKV_HEAD, RWKV_HEAD), F32)
    _, y = lax.scan(step, S0, xs)
    y = jnp.moveaxis(y, 0, 1)
    mu = jnp.mean(y, axis=-1, keepdims=True)
    var = jnp.mean(jnp.square(y - mu), axis=-1, keepdims=True)
    y = ((y - mu) * lax.rsqrt(var + RWKV_GN_EPS)).reshape(bsz, T, HALF) * ln_w.astype(F32) + ln_b.astype(F32)
    bonus = jnp.sum(r_h * k_h * r_k.astype(F32), axis=-1, keepdims=True) * v_h
    y = y + bonus.reshape(bsz, T, HALF)
    return y * g


def retention(q, k, v, gate):
    bsz, T, _ = q.shape
    q = q.astype(F32).reshape(bsz, T, RET_HEADS, RET_DK)
    k = k.astype(F32).reshape(bsz, T, RET_HEADS, RET_DK)
    v = v.astype(F32).reshape(bsz, T, RET_HEADS, RET_DV)
    pos = jnp.arange(T, dtype=F32)
    inv_freq = ROPE_BASE ** (-jnp.linspace(0.0, 1.0, RET_DK // 2, dtype=F32))
    ang = pos[:, None] * inv_freq[None, :]
    cos, sin = jnp.cos(ang)[None, :, None, :], jnp.sin(ang)[None, :, None, :]

    def rot(t):
        t1, t2 = jnp.split(t, 2, axis=-1)
        return jnp.concatenate([t1 * cos - t2 * sin, t1 * sin + t2 * cos], axis=-1)

    q, k = rot(q), rot(k) * (RET_DK ** -0.5)
    log_g = jnp.log(1.0 - 2.0 ** (-5.0 - jnp.arange(RET_HEADS, dtype=F32)))
    c = RET_CHUNK
    qc, kc, vc = to_chunks(q, c), to_chunks(k, c), to_chunks(v, c)
    idx = jnp.arange(c, dtype=F32)
    causal = jnp.tril(jnp.ones((c, c), dtype=bool))
    dmask = jnp.exp(jnp.where(causal, log_g[:, None, None] * (idx[:, None] - idx[None, :]), -jnp.inf))
    o_inner = jnp.einsum('bnhij,bnhje->bnhie', jnp.einsum('bnhid,bnhjd->bnhij', qc, kc) * dmask, vc)
    q_dec = qc * jnp.exp(log_g[:, None] * (idx + 1.0))[..., None]
    k_dec = kc * jnp.exp(log_g[:, None] * (c - 1.0 - idx))[..., None]
    chunk_decay = jnp.exp(log_g * c)[None, :, None, None]

    def step(R, xs):
        qd, kd, vv = xs
        o = jnp.einsum('bhcd,bhde->bhce', qd, R)
        R = R * chunk_decay + jnp.einsum('bhcd,bhce->bhde', kd, vv)
        return R, o

    xs = tuple(jnp.moveaxis(t, 1, 0) for t in (q_dec, k_dec, vc))
    R0 = jnp.zeros((bsz, RET_HEADS, RET_DK, RET_DV), F32)
    _, o_cross = lax.scan(step, R0, xs)
    o = from_chunks(o_inner + jnp.moveaxis(o_cross, 0, 1))
    o = o * lax.rsqrt(jnp.mean(o * o, axis=-1, keepdims=True) + EPS)
    return o.reshape(bsz, T, HALF) * jax.nn.silu(gate.astype(F32))


def conv_ffn(h, w_up, conv_w, w_down):
    z = causal_dwconv(h @ w_up, conv_w)
    gate, val = jnp.split(z, 2, axis=-1)
    return (jax.nn.silu(gate) * val) @ w_down


def setup_inputs(seed: int = 0) -> dict:
    key = jax.random.key(seed)
    ks = iter(jax.random.split(key, 48))

    def nrm(shape, scale):
        return jax.random.normal(next(ks), shape, F32) * scale

    def unif(shape, lo, hi):
        return jax.random.uniform(next(ks), shape, F32, lo, hi)

    ne, no = N_EVEN, N_ODD
    G, P = S5_GROUPS, S5_STATE
    x = nrm((BATCH, SEQ, D_MODEL), 1.0)
    norm_mix = 1.0 + nrm((DEPTH, D_MODEL), 0.02)
    norm_ffn = 1.0 + nrm((DEPTH, D_MODEL), 0.02)
    norm_final = 1.0 + nrm((D_MODEL,), 0.02)
    ev_w_in = nrm((ne, D_MODEL, EVEN_PROJ), D_MODEL ** -0.5)
    ev_w_out = nrm((ne, MIX_WIDTH, D_MODEL), MIX_WIDTH ** -0.5)
    gdn_conv_w = nrm((ne, GDN_CONV, 3 * HALF), GDN_CONV ** -0.5)
    gdn_a_log = jnp.log(unif((ne, GDN_HEADS), 1.0, 16.0))
    gdn_dt = jnp.exp(unif((ne, GDN_HEADS), math.log(1e-3), math.log(1e-1)))
    gdn_dt_bias = gdn_dt + jnp.log(-jnp.expm1(-gdn_dt))
    gdn_norm_w = 1.0 + nrm((ne, GDN_DV), 0.02)
    n = jnp.arange(P, dtype=F32)
    s5_lam_re = -0.5 * jnp.exp(nrm((ne, G, P), 0.05))
    s5_lam_im = math.pi * n + nrm((ne, G, P), 0.01)
    s5_b_re = nrm((ne, G, P, S5_GROUP), (2 * S5_GROUP) ** -0.5)
    s5_b_im = nrm((ne, G, P, S5_GROUP), (2 * S5_GROUP) ** -0.5)
    s5_c_re = nrm((ne, G, S5_GROUP, P), P ** -0.5)
    s5_c_im = nrm((ne, G, S5_GROUP, P), P ** -0.5)
    s5_d = nrm((ne, HALF), 1.0)
    s5_log_step = unif((ne, G), math.log(1e-3), math.log(1e-1))
    s5_w_glu = nrm((ne, HALF, HALF), HALF ** -0.5)
    od_w_in = nrm((no, D_MODEL, ODD_PROJ), D_MODEL ** -0.5)
    od_w_out = nrm((no, MIX_WIDTH, D_MODEL), MIX_WIDTH ** -0.5)
    rwkv_shift_mu = unif((no, RWKV_WIDTH), 0.0, 1.0)
    rwkv_w0 = jnp.linspace(-6.0, -1.0, HALF, dtype=F32)[None, :] + nrm((no, HALF), 0.1)
    rwkv_w2 = nrm((no, RWKV_DECAY_LORA, HALF), 0.1 * RWKV_DECAY_LORA ** -0.5)
    rwkv_a0 = nrm((no, HALF), 0.1)
    rwkv_a2 = nrm((no, RWKV_ICLR_LORA, HALF), 0.5 * RWKV_ICLR_LORA ** -0.5)
    rwkv_g2 = nrm((no, RWKV_GATE_LORA, HALF), RWKV_GATE_LORA ** -0.5)
    rwkv_k_k = 0.85 + nrm((no, HALF), 0.02)
    rwkv_k_a = 1.0 + nrm((no, HALF), 0.02)
    rwkv_r_k = nrm((no, RWKV_HEADS, RWKV_HEAD), 0.1)
    rwkv_ln_w = 1.0 + nrm((no, HALF), 0.02)
    rwkv_ln_b = nrm((no, HALF), 0.01)
    ffn_w_up = nrm((DEPTH, D_MODEL, 2 * D_FF), D_MODEL ** -0.5)
    ffn_conv_w = nrm((DEPTH, FFN_CONV, 2 * D_FF), FFN_CONV ** -0.5)
    ffn_w_down = nrm((DEPTH, D_FF, D_MODEL), D_FF ** -0.5)
    return {'x': x, 'norm_mix': norm_mix, 'norm_ffn': norm_ffn, 'norm_final': norm_final,
            'ev_w_in': ev_w_in, 'ev_w_out': ev_w_out, 'gdn_conv_w': gdn_conv_w, 'gdn_a_log': gdn_a_log,
            'gdn_dt_bias': gdn_dt_bias, 'gdn_norm_w': gdn_norm_w, 's5_lam_re': s5_lam_re,
            's5_lam_im': s5_lam_im, 's5_b_re': s5_b_re, 's5_b_im': s5_b_im, 's5_c_re': s5_c_re,
            's5_c_im': s5_c_im, 's5_d': s5_d, 's5_log_step': s5_log_step, 's5_w_glu': s5_w_glu,
            'od_w_in': od_w_in, 'od_w_out': od_w_out, 'rwkv_shift_mu': rwkv_shift_mu, 'rwkv_w0': rwkv_w0,
            'rwkv_w2': rwkv_w2, 'rwkv_a0': rwkv_a0, 'rwkv_a2': rwkv_a2, 'rwkv_g2': rwkv_g2,
            'rwkv_k_k': rwkv_k_k, 'rwkv_k_a': rwkv_k_a, 'rwkv_r_k': rwkv_r_k, 'rwkv_ln_w': rwkv_ln_w,
            'rwkv_ln_b': rwkv_ln_b, 'ffn_w_up': ffn_w_up, 'ffn_conv_w': ffn_conv_w, 'ffn_w_down': ffn_w_down}


def reference(x, norm_mix, norm_ffn, norm_final, ev_w_in, ev_w_out, gdn_conv_w, gdn_a_log, gdn_dt_bias,
              gdn_norm_w, s5_lam_re, s5_lam_im, s5_b_re, s5_b_im, s5_c_re, s5_c_im, s5_d, s5_log_step,
              s5_w_glu, od_w_in, od_w_out, rwkv_shift_mu, rwkv_w0, rwkv_w2, rwkv_a0, rwkv_a2, rwkv_g2,
              rwkv_k_k, rwkv_k_a, rwkv_r_k, rwkv_ln_w, rwkv_ln_b, ffn_w_up, ffn_conv_w, ffn_w_down):
    dt = x.dtype
    h = x
    for layer in range(DEPTH):
        hn = rms_norm(h, norm_mix[layer]).astype(dt)
        i = layer // 2
        if layer % 2 == 0:
            z = hn @ ev_w_in[i]
            aq, ak, av, ag, ab, aa, bu = split_cols(z, EVEN_COLS)
            ya = gated_deltanet(aq, ak, av, ag, ab, aa, gdn_conv_w[i], gdn_a_log[i], gdn_dt_bias[i], gdn_norm_w[i])
            yb = s5(bu, s5_lam_re[i], s5_lam_im[i], s5_b_re[i], s5_b_im[i], s5_c_re[i], s5_c_im[i],
                    s5_d[i], s5_log_step[i], s5_w_glu[i])
            mix = jnp.concatenate([ya, yb], axis=-1).astype(dt) @ ev_w_out[i]
        else:
            z = hn @ od_w_in[i]
            yc = rwkv7(z[..., :RWKV_WIDTH], rwkv_shift_mu[i], rwkv_w0[i], rwkv_w2[i], rwkv_a0[i], rwkv_a2[i],
                       rwkv_g2[i], rwkv_k_k[i], rwkv_k_a[i], rwkv_r_k[i], rwkv_ln_w[i], rwkv_ln_b[i])
            dq, dk, dv, dg = split_cols(z[..., RWKV_WIDTH:], RET_COLS)
            yd = retention(dq, dk, dv, dg)
            mix = jnp.concatenate([yc, yd], axis=-1).astype(dt) @ od_w_out[i]
        h = h + mix
        h = h + conv_ffn(rms_norm(h, norm_ffn[layer]).astype(dt), ffn_w_up[layer], ffn_conv_w[layer], ffn_w_down[layer])
    return rms_norm(h, norm_final).astype(dt)
```

```python
import functools
import math

import jax
import jax.numpy as jnp
from jax import lax
from jax.experimental import pallas as pl
from jax.experimental.pallas import tpu as pltpu

F32 = jnp.float32
BF16 = jnp.bfloat16
HI = lax.Precision.HIGHEST

V7X_VMEM_LIMIT_BYTES = 56 * 1024 * 1024
LANES = 128
SUBLANES = 8

EPS = 1e-6
HALF = 1024
GDN_HEADS = 8
GDN_D = 128
GDN_CONV = 4
GDN_CHUNK = 64
S5_GROUP = 16
S5_GROUPS = 64
S5_STATE = 64
S5_TILE_GROUPS = LANES // S5_GROUP
S5_TILE_STATES = S5_TILE_GROUPS * S5_STATE
RWKV_HEAD = 64
RWKV_PAIRS = HALF // LANES
RWKV_CHUNK = 64
RWKV_LORA_PAD = 384
RWKV_GN_EPS = 64e-5
RET_HEADS = 4
RET_DK = 128
RET_DV = 256
RET_CHUNK = 128
ROPE_BASE = 10000.0
INV_BLOCK = 16


def _cparams(*sem):
    return pltpu.CompilerParams(dimension_semantics=sem, vmem_limit_bytes=V7X_VMEM_LIMIT_BYTES)


def _dot(a, b, precision=None):
    return jnp.dot(a, b, preferred_element_type=F32, precision=precision)


def _dot_nt(a, b, precision=None):
    return lax.dot_general(a, b, (((1,), (1,)), ((), ())), preferred_element_type=F32, precision=precision)


def _dot_tn(a, b, precision=None):
    return lax.dot_general(a, b, (((0,), (0,)), ((), ())), preferred_element_type=F32, precision=precision)


def _iota2(shape):
    return lax.broadcasted_iota(jnp.int32, shape, 0), lax.broadcasted_iota(jnp.int32, shape, 1)


def _unit_lower_inverse(a, row, col, sub, precision):
    assert sub == 4 * INV_BLOCK
    shift = INV_BLOCK.bit_length() - 1
    eye = jnp.where(row == col, 1.0, 0.0).astype(F32)
    ad = jnp.where((row >> shift) == (col >> shift), a, 0.0)
    ao = a - ad
    a2 = _dot(ad, ad, precision)
    a4 = _dot(a2, a2, precision)
    a8 = _dot(a4, a4, precision)
    td = _dot(_dot(eye - ad, eye + a2, precision), _dot(eye + a4, eye + a8, precision), precision)
    n = _dot(td, ao, precision)
    n2 = _dot(n, n, precision)
    return _dot(_dot(eye - n, eye + n2, precision), td, precision)


def _rmsnorm_kernel(x_ref, w_ref, o_ref):
    x = x_ref[...]
    ms = jnp.mean(x * x, axis=-1, keepdims=True)
    o_ref[...] = (x * lax.rsqrt(ms + EPS) * w_ref[...]).astype(o_ref.dtype)


def _rmsnorm(x, w, out_dtype):
    m, d = x.shape
    tm = min(512, m)
    return pl.pallas_call(
        _rmsnorm_kernel,
        grid=(m // tm,),
        in_specs=[pl.BlockSpec((tm, d), lambda i: (i, 0)), pl.BlockSpec((1, d), lambda i: (0, 0))],
        out_specs=pl.BlockSpec((tm, d), lambda i: (i, 0)),
        out_shape=jax.ShapeDtypeStruct((m, d), out_dtype),
        compiler_params=_cparams("parallel"),
        name="rmsnorm",
    )(x, w.reshape(1, d).astype(F32))


def _mm_kernel(*refs, n_pairs, has_res):
    o_ref = refs[-1]
    acc = None
    for p in range(n_pairs):
        d = _dot(refs[2 * p][...].astype(BF16), refs[2 * p + 1][...])
        acc = d if acc is None else acc + d
    if has_res:
        acc = acc + refs[2 * n_pairs][...]
    o_ref[...] = acc.astype(o_ref.dtype)


def _pick_tile(n, prefs):
    for t in prefs:
        if n % t == 0:
            return t
    return n


def _matmul(pairs, res=None, out_dtype=F32, tm=1024, tn=None):
    m = pairs[0][0].shape[0]
    n = pairs[0][1].shape[1]
    tm = min(tm, m)
    tn = tn or _pick_tile(n, (1024, 512, 384, 256, 128))
    in_specs, args = [], []
    for a, w in pairs:
        k = a.shape[1]
        in_specs += [pl.BlockSpec((tm, k), lambda i, j: (i, 0)), pl.BlockSpec((k, tn), lambda i, j: (0, j))]
        args += [a, w]
    if res is not None:
        in_specs.append(pl.BlockSpec((tm, tn), lambda i, j: (i, j)))
        args.append(res)
    return pl.pallas_call(
        functools.partial(_mm_kernel, n_pairs=len(pairs), has_res=res is not None),
        grid=(m // tm, n // tn),
        in_specs=in_specs,
        out_specs=pl.BlockSpec((tm, tn), lambda i, j: (i, j)),
        out_shape=jax.ShapeDtypeStruct((m, n), out_dtype),
        compiler_params=_cparams("parallel", "parallel"),
        name="matmul",
    )(*args)


def _ffn_up_kernel(h_ref, wg_ref, wv_ref, cg_ref, cv_ref, o_ref, carry_ref, *, blocks_per_seq):
    i = pl.program_id(1)

    @pl.when(i % blocks_per_seq == 0)
    def _():
        carry_ref[...] = jnp.zeros_like(carry_ref)

    h = h_ref[...]
    zg = _dot(h, wg_ref[...])
    zv = _dot(h, wv_ref[...])
    tm = zg.shape[0]
    row = lax.broadcasted_iota(jnp.int32, zg.shape, 0)

    def conv(z, c_ref, prev):
        m1 = jnp.where(row == 0, prev[7:8, :], pltpu.roll(z, 1, 0))
        m2 = jnp.where(row == 0, prev[6:7, :], jnp.where(row == 1, prev[7:8, :], pltpu.roll(z, 2, 0)))
        c = c_ref[...]
        return c[0:1, :] * m2 + c[1:2, :] * m1 + c[2:3, :] * z

    g = conv(zg, cg_ref, carry_ref[0])
    v = conv(zv, cv_ref, carry_ref[1])
    carry_ref[0] = zg[tm - SUBLANES:, :]
    carry_ref[1] = zv[tm - SUBLANES:, :]
    o_ref[...] = (g * jax.nn.sigmoid(g) * v).astype(o_ref.dtype)


def _ffn_up(hn, w_up, conv_w, seq):
    m, d = hn.shape
    f = w_up.shape[1] // 2
    tm = min(1024, seq)
    tn = _pick_tile(f, (512, 256, 128))
    nj = f // tn
    return pl.pallas_call(
        functools.partial(_ffn_up_kernel, blocks_per_seq=seq // tm),
        grid=(nj, m // tm),
        in_specs=[
            pl.BlockSpec((tm, d), lambda j, i: (i, 0)),
            pl.BlockSpec((d, tn), lambda j, i: (0, j)),
            pl.BlockSpec((d, tn), lambda j, i: (0, j + nj)),
            pl.BlockSpec((3, tn), lambda j, i: (0, j)),
            pl.BlockSpec((3, tn), lambda j, i: (0, j + nj)),
        ],
        out_specs=pl.BlockSpec((tm, tn), lambda j, i: (i, j)),
        out_shape=jax.ShapeDtypeStruct((m, f), BF16),
        scratch_shapes=[pltpu.VMEM((2, SUBLANES, tn), F32)],
        compiler_params=_cparams("parallel", "arbitrary"),
        name="ffn_up_conv",
    )(hn, w_up, w_up, conv_w, conv_w)


def _gdn_prep_kernel(zq_ref, zk_ref, zv_ref, cq_ref, ck_ref, cv_ref, oq_ref, ok_ref, ov_ref):
    row = lax.broadcasted_iota(jnp.int32, zq_ref.shape[1:], 0)

    def conv_silu(z_ref, c_ref):
        z = z_ref[0]
        c = c_ref[...]
        acc = c[GDN_CONV - 1:GDN_CONV, :] * z
        for s in range(1, GDN_CONV):
            acc = acc + c[GDN_CONV - 1 - s:GDN_CONV - s, :] * jnp.where(row >= s, pltpu.roll(z, s, 0), 0.0)
        return acc * jax.nn.sigmoid(acc)

    def l2n(t):
        return t * lax.rsqrt(jnp.sum(t * t, axis=-1, keepdims=True) + 1e-6)

    oq_ref[0] = l2n(conv_silu(zq_ref, cq_ref)) * (GDN_D ** -0.5)
    ok_ref[0] = l2n(conv_silu(zk_ref, ck_ref))
    ov_ref[0] = conv_silu(zv_ref, cv_ref)


def _gdn_prep(z3, conv_w):
    b, t, _ = z3.shape
    h = GDN_HEADS
    zspec = lambda off: pl.BlockSpec((1, t, GDN_D), lambda bi, hi: (bi, 0, hi + off))
    cspec = lambda off: pl.BlockSpec((GDN_CONV, GDN_D), lambda bi, hi: (0, hi + off))
    ospec = pl.BlockSpec((1, t, GDN_D), lambda bi, hi: (bi, 0, hi))
    oshape = jax.ShapeDtypeStruct((b, t, HALF), F32)
    return pl.pallas_call(
        _gdn_prep_kernel,
        grid=(b, h),
        in_specs=[zspec(0), zspec(h), zspec(2 * h), cspec(0), cspec(h), cspec(2 * h)],
        out_specs=[ospec, ospec, ospec],
        out_shape=[oshape, oshape, oshape],
        compiler_params=_cparams("parallel", "parallel"),
        name="gdn_prep",
    )(z3, z3, z3, conv_w, conv_w, conv_w)


def _gdn_kernel(q_ref, k_ref, v_ref, gate_ref, zs_ref, alog_ref, dtb_ref, nw_ref, o_ref, s_ref, *, prec):
    c = GDN_CHUNK

    @pl.when(pl.program_id(1) == 0)
    def _():
        s_ref[...] = jnp.zeros_like(s_ref)

    zs = zs_ref[0]
    beta_all = jax.nn.sigmoid(zs)
    g_all = -jnp.exp(alog_ref[...]) * jax.nn.softplus(zs + dtb_ref[...])
    row, col = _iota2((c, c))
    lower = jnp.where(row >= col, 1.0, 0.0).astype(F32)
    nw = nw_ref[...]

    for h in range(GDN_HEADS):
        sl = slice(h * GDN_D, (h + 1) * GDN_D)
        q = q_ref[0, :, sl]
        k = k_ref[0, :, sl]
        v = v_ref[0, :, sl]
        beta = beta_all[:, h:h + 1]
        g_col = g_all[:, GDN_HEADS + h:GDN_HEADS + h + 1]
        diff = _dot(lower, jnp.where(row > col, jnp.broadcast_to(g_col, (c, c)), 0.0), HI)
        decay = jnp.where(row >= col, jnp.exp(diff), 0.0)
        gc = _dot(lower, jnp.broadcast_to(g_col, (c, GDN_D)), HI)
        gl = gc[c - 1:c, :]
        egc = jnp.exp(gc)
        kb = k * beta
        a = jnp.where(row > col, _dot_nt(kb, k, prec) * decay, 0.0)
        tinv = _unit_lower_inverse(a, row, col, c, prec)
        rhs = jnp.concatenate([v * beta, kb * egc], axis=-1)
        sol = _dot(tinv, rhs, prec)
        u = sol[:, :GDN_D]
        w = sol[:, GDN_D:]
        attn = _dot_nt(q, k, prec) * decay
        s = s_ref[h]
        v_new = u - _dot(w, s, prec)
        o = _dot(q * egc, s, prec) + _dot(attn, v_new, prec)
        s_ref[h] = s * jnp.exp(gl) + _dot_tn(k * jnp.exp(gl - gc), v_new, prec)
        o = o * lax.rsqrt(jnp.mean(o * o, axis=-1, keepdims=True) + EPS) * nw
        gate = gate_ref[0, :, sl]
        o_ref[0, :, sl] = (o * gate * jax.nn.sigmoid(gate)).astype(o_ref.dtype)


def _gdn(q, k, v, z3, zs3, a_log, dt_bias, norm_w, prec):
    b, t, _ = q.shape
    c = GDN_CHUNK
    pad = lambda p: jnp.zeros((1, LANES), F32).at[0, GDN_HEADS:2 * GDN_HEADS].set(p.astype(F32))
    blk = pl.BlockSpec((1, c, HALF), lambda bi, ni: (bi, ni, 0))
    vec = pl.BlockSpec((1, LANES), lambda bi, ni: (0, 0))
    return pl.pallas_call(
        functools.partial(_gdn_kernel, prec=prec),
        grid=(b, t // c),
        in_specs=[blk, blk, blk,
                  pl.BlockSpec((1, c, HALF), lambda bi, ni: (bi, ni, 3)),
                  pl.BlockSpec((1, c, LANES), lambda bi, ni: (bi, ni, 0)),
                  vec, vec, vec],
        out_specs=blk,
        out_shape=jax.ShapeDtypeStruct((b, t, HALF), BF16),
        scratch_shapes=[pltpu.VMEM((GDN_HEADS, GDN_D, GDN_D), F32)],
        compiler_params=_cparams("parallel", "arbitrary"),
        name="gdn_chunk",
    )(q, k, v, z3, zs3, pad(a_log), pad(dt_bias), norm_w.reshape(1, GDN_D).astype(F32))


def _s5_param_kernel(lr_ref, li_ref, dt_ref, br_ref, bi_ref, ar_ref, ai_ref, bbr_ref, bbi_ref):
    lr, li, dt = lr_ref[...], li_ref[...], dt_ref[...]
    step = jnp.exp(dt)
    mag = jnp.exp(lr * step)
    ang = li * step
    ab_re, ab_im = mag * jnp.cos(ang), mag * jnp.sin(ang)
    den = lr * lr + li * li
    nr = ab_re - 1.0
    f_re = (nr * lr + ab_im * li) / den
    f_im = (ab_im * lr - nr * li) / den
    br, bi = br_ref[...], bi_ref[...]
    ar_ref[...] = ab_re
    ai_ref[...] = ab_im
    bbr_ref[...] = f_re * br - f_im * bi
    bbi_ref[...] = f_re * bi + f_im * br


def _s5_params(lam_re, lam_im, log_step, b_re, b_im):
    gp = S5_GROUPS * S5_STATE
    row = lambda x: x.astype(F32).reshape(1, gp)
    bt = lambda x: jnp.transpose(x.astype(F32), (2, 0, 1)).reshape(S5_GROUP, gp)
    vshape = jax.ShapeDtypeStruct((1, gp), F32)
    mshape = jax.ShapeDtypeStruct((S5_GROUP, gp), F32)
    return pl.pallas_call(
        _s5_param_kernel,
        out_shape=[vshape, vshape, mshape, mshape],
        name="s5_params",
    )(row(lam_re), row(lam_im), row(jnp.repeat(log_step[:, None], S5_STATE, axis=1)), bt(b_re), bt(b_im))


def _s5_scan_kernel(u_ref, bdr_ref, bdi_ref, cdr_ref, cdi_ref, ar_ref, ai_ref, d_ref, o_ref, xr_ref, xi_ref):
    t = u_ref.shape[1]
    n = S5_TILE_STATES
    rt = min(512, t)
    for r0 in range(0, t, rt):
        ub = u_ref[0, r0:r0 + rt, :].astype(BF16)
        xr_ref[r0:r0 + rt, :] = _dot(ub, bdr_ref[0])
        xi_ref[r0:r0 + rt, :] = _dot(ub, bdi_ref[0])

    a1r, a1i = ar_ref[...], ai_ref[...]
    cmul = lambda xr, xi, yr, yi: (xr * yr - xi * yi, xr * yi + xi * yr)
    a2r, a2i = cmul(a1r, a1i, a1r, a1i)
    a4r, a4i = cmul(a2r, a2i, a2r, a2i)
    pows = [(a1r, a1i), (a2r, a2i)]
    for _ in range(2, SUBLANES):
        pows.append(cmul(pows[-1][0], pows[-1][1], a1r, a1i))
    row = lax.broadcasted_iota(jnp.int32, (SUBLANES, n), 0)
    pr = jnp.zeros((SUBLANES, n), F32)
    pi = jnp.zeros((SUBLANES, n), F32)
    for r in range(SUBLANES):
        pr = jnp.where(row == r, pows[r][0], pr)
        pi = jnp.where(row == r, pows[r][1], pi)
    levels = [(1, jnp.broadcast_to(a1r, (SUBLANES, n)), jnp.broadcast_to(a1i, (SUBLANES, n))),
              (2, jnp.broadcast_to(a2r, (SUBLANES, n)), jnp.broadcast_to(a2i, (SUBLANES, n))),
              (4, jnp.broadcast_to(a4r, (SUBLANES, n)), jnp.broadcast_to(a4i, (SUBLANES, n)))]

    def body(blk, carry):
        cr, ci = carry
        r0 = pl.multiple_of(blk * SUBLANES, SUBLANES)
        xr = xr_ref[pl.ds(r0, SUBLANES), :]
        xi = xi_ref[pl.ds(r0, SUBLANES), :]
        for d, ar, ai in levels:
            sr = jnp.where(row >= d, pltpu.roll(xr, d, 0), 0.0)
            si = jnp.where(row >= d, pltpu.roll(xi, d, 0), 0.0)
            xr, xi = xr + ar * sr - ai * si, xi + ar * si + ai * sr
        xr, xi = xr + pr * cr - pi * ci, xi + pr * ci + pi * cr
        xr_ref[pl.ds(r0, SUBLANES), :] = xr
        xi_ref[pl.ds(r0, SUBLANES), :] = xi
        return xr[SUBLANES - 1:, :], xi[SUBLANES - 1:, :]

    zero = jnp.zeros((1, n), F32)
    lax.fori_loop(0, t // SUBLANES, body, (zero, zero))

    for r0 in range(0, t, rt):
        u = u_ref[0, r0:r0 + rt, :]
        y = (_dot(xr_ref[r0:r0 + rt, :].astype(BF16), cdr_ref[0])
             - _dot(xi_ref[r0:r0 + rt, :].astype(BF16), cdi_ref[0]) + u * d_ref[...])
        o_ref[0, r0:r0 + rt, :] = jax.nn.gelu(y)


def _s5_scan(z3, u_off, bd_re, bd_im, cd_re, cd_im, a_re, a_im, d_skip):
    b, t, _ = z3.shape
    nt = HALF // LANES
    n = S5_TILE_STATES
    return pl.pallas_call(
        _s5_scan_kernel,
        grid=(b, nt),
        in_specs=[
            pl.BlockSpec((1, t, LANES), lambda bi, j: (bi, 0, j + u_off)),
            pl.BlockSpec((1, LANES, n), lambda bi, j: (j, 0, 0)),
            pl.BlockSpec((1, LANES, n), lambda bi, j: (j, 0, 0)),
            pl.BlockSpec((1, n, LANES), lambda bi, j: (j, 0, 0)),
            pl.BlockSpec((1, n, LANES), lambda bi, j: (j, 0, 0)),
            pl.BlockSpec((1, n), lambda bi, j: (0, j)),
            pl.BlockSpec((1, n), lambda bi, j: (0, j)),
            pl.BlockSpec((1, LANES), lambda bi, j: (0, j)),
        ],
        out_specs=pl.BlockSpec((1, t, LANES), lambda bi, j: (bi, 0, j)),
        out_shape=jax.ShapeDtypeStruct((b, t, HALF), F32),
        scratch_shapes=[pltpu.VMEM((t, n), F32), pltpu.VMEM((t, n), F32)],
        compiler_params=_cparams("parallel", "parallel"),
        name="s5_scan",
    )(z3, bd_re, bd_im, cd_re, cd_im, a_re, a_im, d_skip.reshape(1, HALF).astype(F32))


def _glu_kernel(y_ref, w_ref, yt_ref, o_ref):
    gate = _dot(y_ref[...].astype(BF16), w_ref[...])
    o_ref[...] = (yt_ref[...] * jax.nn.sigmoid(gate)).astype(o_ref.dtype)


def _glu(y, w):
    m, k = y.shape
    tm = min(1024, m)
    tn = 512
    return pl.pallas_call(
        _glu_kernel,
        grid=(m // tm, k // tn),
        in_specs=[pl.BlockSpec((tm, k), lambda i, j: (i, 0)),
                  pl.BlockSpec((k, tn), lambda i, j: (0, j)),
                  pl.BlockSpec((tm, tn), lambda i, j: (i, j))],
        out_specs=pl.BlockSpec((tm, tn), lambda i, j: (i, j)),
        out_shape=jax.ShapeDtypeStruct((m, k), BF16),
        compiler_params=_cparams("parallel", "parallel"),
        name="s5_glu",
    )(y, w, y)


def _s5_block_diag(bb_re, bb_im, c_re, c_im):
    nt, tg = HALF // LANES, S5_TILE_GROUPS
    eye = jnp.eye(tg, dtype=F32)

    def bmap(bb):
        x = bb.reshape(S5_GROUP, nt, tg, S5_STATE)
        x = jnp.einsum('cjgp,gh->jgchp', x, eye)
        return x.reshape(nt, LANES, S5_TILE_STATES).astype(BF16)

    def cmap(cc):
        x = cc.astype(F32).reshape(nt, tg, S5_GROUP, S5_STATE)
        x = jnp.einsum('jgcp,gh->jgphc', x, eye)
        return x.reshape(nt, S5_TILE_STATES, LANES).astype(BF16)

    return bmap(bb_re), bmap(bb_im), cmap(c_re), cmap(c_im)


def _seg_sum(x, seg):
    row, col = _iota2((LANES, LANES))
    shift = seg.bit_length() - 1
    ones = jnp.where((row >> shift) == (col >> shift), 1.0, 0.0).astype(F32)
    return _dot(x, ones, HI)


def _rwkv_prep_kernel(zr_ref, zk_ref, zv_ref, zl_ref, hr_ref, hk_ref, hv_ref, hl_ref,
                      mur_ref, muk_ref, muv_ref, mul_ref, w2_ref, a2_ref, g2_ref,
                      w0_ref, a0_ref, kk_ref, ka_ref,
                      or_ref, ow_ref, ok_ref, ov_ref, okk_ref, ob_ref, og_ref):
    first = pl.program_id(1) == 0

    def shift_mix(z_ref, halo_ref, mu_ref):
        z = z_ref[0]
        row = lax.broadcasted_iota(jnp.int32, z.shape, 0)
        prev = jnp.where(first, 0.0, halo_ref[0, SUBLANES - 1:SUBLANES, :])
        zm1 = jnp.where(row == 0, prev, pltpu.roll(z, 1, 0))
        return z + (zm1 - z) * mu_ref[...]

    r = shift_mix(zr_ref, hr_ref, mur_ref)
    k = shift_mix(zk_ref, hk_ref, muk_ref)
    v = shift_mix(zv_ref, hv_ref, muv_ref)
    zl = shift_mix(zl_ref, hl_ref, mul_ref)
    wa = zl[:, :LANES]
    w = w0_ref[...] + _dot(jnp.tanh(wa).astype(BF16), w2_ref[...])
    w = -jax.nn.softplus(-w) - 0.5
    a = jax.nn.sigmoid(a0_ref[...] + _dot(wa.astype(BF16), a2_ref[...]))
    g = _dot(jax.nn.sigmoid(zl[:, LANES:]).astype(BF16), g2_ref[...])
    kk = k * kk_ref[...]
    kk = kk * lax.rsqrt(_seg_sum(kk * kk, RWKV_HEAD) + 1e-6)
    or_ref[0] = r
    ow_ref[0] = -jnp.exp(w)
    ok_ref[0] = k * (1.0 + (a - 1.0) * ka_ref[...])
    ov_ref[0] = v
    okk_ref[0] = kk
    ob_ref[0] = kk * a
    og_ref[0] = g


def _rwkv_prep(z3, zl3, mu_main, mu_lora, w2p, a2p, g2p, w0, a0, k_k, k_a):
    b, t, _ = z3.shape
    tt = min(512, t)
    nh = tt // SUBLANES
    npair = RWKV_PAIRS
    lw = RWKV_LORA_PAD
    main = lambda off: pl.BlockSpec((1, tt, LANES), lambda bi, ti, j: (bi, ti, j + off))
    halo = lambda off: pl.BlockSpec((1, SUBLANES, LANES),
                                    lambda bi, ti, j: (bi, jnp.maximum(ti * nh - 1, 0), j + off))
    vec = lambda off: pl.BlockSpec((1, LANES), lambda bi, ti, j: (0, j + off))
    ospec = pl.BlockSpec((1, tt, LANES), lambda bi, ti, j: (bi, ti, j))
    oshape = jax.ShapeDtypeStruct((b, t, HALF), F32)
    row = lambda x: x.astype(F32).reshape(1, -1)
    return pl.pallas_call(
        _rwkv_prep_kernel,
        grid=(b, t // tt, npair),
        in_specs=[
            main(0), main(npair), main(2 * npair),
            pl.BlockSpec((1, tt, lw), lambda bi, ti, j: (bi, ti, 0)),
            halo(0), halo(npair), halo(2 * npair),
            pl.BlockSpec((1, SUBLANES, lw), lambda bi, ti, j: (bi, jnp.maximum(ti * nh - 1, 0), 0)),
            vec(0), vec(npair), vec(2 * npair),
            pl.BlockSpec((1, lw), lambda bi, ti, j: (0, 0)),
            pl.BlockSpec((LANES, LANES), lambda bi, ti, j: (0, j)),
            pl.BlockSpec((LANES, LANES), lambda bi, ti, j: (0, j)),
            pl.BlockSpec((lw - LANES, LANES), lambda bi, ti, j: (0, j)),
            vec(0), vec(0), vec(0), vec(0),
        ],
        out_specs=[ospec] * 7,
        out_shape=[oshape] * 7,
        compiler_params=_cparams("parallel", "parallel", "parallel"),
        name="rwkv_prep",
    )(z3, z3, z3, zl3, z3, z3, z3, zl3, mu_main, mu_main, mu_main, mu_lora,
      w2p, a2p, g2p, row(w0), row(a0), row(k_k), row(k_a))


def _rwkv_kernel(r_ref, w_ref, k_ref, v_ref, kk_ref, b_ref, g_ref, rk_ref, lnw_ref, lnb_ref, o_ref, h_ref, *, prec):
    c = RWKV_CHUNK
    c2 = 2 * c

    @pl.when(pl.program_id(1) == 0)
    def _():
        h_ref[...] = jnp.zeros_like(h_ref)

    row1, col1 = _iota2((c, c))
    lower = jnp.where(row1 >= col1, 1.0, 0.0).astype(F32)
    row, col = _iota2((c2, c2))
    same = (row >> 6) == (col >> 6)
    strict = jnp.logical_and(same, row > col)
    incl = jnp.logical_and(same, row >= col)
    lane = lax.broadcasted_iota(jnp.int32, (c, LANES), 1)
    first_head = lane < RWKV_HEAD

    def stack2(x):
        return jnp.concatenate([jnp.where(first_head, x, 0.0), jnp.where(first_head, 0.0, x)], axis=0)

    for p in range(RWKV_PAIRS):
        sl = slice(p * LANES, (p + 1) * LANES)
        r, lw, k, v = r_ref[0, :, sl], w_ref[0, :, sl], k_ref[0, :, sl], v_ref[0, :, sl]
        kk, b = kk_ref[0, :, sl], b_ref[0, :, sl]
        cl = _dot(lower, lw, HI)
        cl_last = cl[c - 1:c, :]
        e_neg = jnp.exp(-cl)
        e_tail = jnp.exp(cl_last - cl)
        kk2 = stack2(kk * jnp.exp(cl - lw))
        r2 = stack2(r * jnp.exp(cl))
        b2 = stack2(b * e_neg)
        k2 = stack2(k * e_neg)
        v2 = stack2(v)
        bd2 = stack2(b * e_tail)
        kd2 = stack2(k * e_tail)
        a_ab = jnp.where(strict, _dot_nt(kk2, b2, prec), 0.0)
        a_ak = jnp.where(strict, _dot_nt(kk2, k2, prec), 0.0)
        r_b = jnp.where(incl, _dot_nt(r2, b2, prec), 0.0)
        r_k = jnp.where(incl, _dot_nt(r2, k2, prec), 0.0)
        tinv = _unit_lower_inverse(a_ab, row, col, c, prec)
        wk = _dot(tinv, kk2, prec)
        tv = _dot(tinv, _dot(a_ak, v2, prec), prec)
        ht = h_ref[p]
        u = -_dot_nt(wk, ht, prec) - tv
        y2 = _dot_nt(r2, ht, prec) + _dot(r_b, u, prec) + _dot(r_k, v2, prec)
        h_ref[p] = ht * jnp.exp(cl_last) + _dot_tn(u, bd2, prec) + _dot_tn(v2, kd2, prec)
        y = y2[:c, :] + y2[c:, :]
        mu = _seg_sum(y, RWKV_HEAD) * (1.0 / RWKV_HEAD)
        d = y - mu
        var = _seg_sum(d * d, RWKV_HEAD) * (1.0 / RWKV_HEAD)
        yn = d * lax.rsqrt(var + RWKV_GN_EPS) * lnw_ref[:, sl] + lnb_ref[:, sl]
        bonus = _seg_sum(r * k * rk_ref[:, sl], RWKV_HEAD) * v
        o_ref[0, :, sl] = ((yn + bonus) * g_ref[0, :, sl]).astype(o_ref.dtype)


def _rwkv(r, w, k, v, kk, b, g, r_k, ln_w, ln_b, prec):
    bsz, t, _ = r.shape
    c = RWKV_CHUNK
    blk = pl.BlockSpec((1, c, HALF), lambda bi, ni: (bi, ni, 0))
    vec = pl.BlockSpec((1, HALF), lambda bi, ni: (0, 0))
    row = lambda x: x.astype(F32).reshape(1, HALF)
    return pl.pallas_call(
        functools.partial(_rwkv_kernel, prec=prec),
        grid=(bsz, t // c),
        in_specs=[blk] * 7 + [vec] * 3,
        out_specs=blk,
        out_shape=jax.ShapeDtypeStruct((bsz, t, HALF), BF16),
        scratch_shapes=[pltpu.VMEM((RWKV_PAIRS, LANES, LANES), F32)],
        compiler_params=_cparams("parallel", "arbitrary"),
        name="rwkv_chunk",
    )(r, w, k, v, kk, b, g, row(r_k), row(ln_w), row(ln_b))


def _rope_kernel(f_ref, cos_ref, sin_ref):
    t = cos_ref.shape[0]
    pos = lax.broadcasted_iota(jnp.int32, (t, LANES), 0).astype(F32)
    lane = lax.broadcasted_iota(jnp.int32, (t, LANES), 1)
    ang = pos * f_ref[...]
    cos_ref[...] = jnp.cos(ang)
    sin_ref[...] = jnp.where(lane < RET_DK // 2, -1.0, 1.0) * jnp.sin(ang)


def _rope_tables(t):
    inv_freq = ROPE_BASE ** (-jnp.linspace(0.0, 1.0, RET_DK // 2, dtype=F32))
    f2 = jnp.concatenate([inv_freq, inv_freq]).reshape(1, RET_DK)
    shape = jax.ShapeDtypeStruct((t, RET_DK), F32)
    return pl.pallas_call(_rope_kernel, out_shape=[shape, shape], name="rope_tables")(f2)


def _ret_kernel(q_ref, k_ref, v_ref, gate_ref, cos_ref, sin_ref, o_ref, s_ref):
    c = RET_CHUNK

    @pl.when(pl.program_id(1) == 0)
    def _():
        s_ref[...] = jnp.zeros_like(s_ref)

    row, col = _iota2((c, c))
    dist = (row - col).astype(F32)
    idx = lax.broadcasted_iota(jnp.int32, (c, 1), 0).astype(F32)
    cos, sin = cos_ref[...], sin_ref[...]
    rot = lambda x: x * cos + pltpu.roll(x, RET_DK // 2, 1) * sin

    for h in range(RET_HEADS):
        log_g = math.log(1.0 - 2.0 ** (-5.0 - h))
        q = rot(q_ref[0, :, h * RET_DK:(h + 1) * RET_DK])
        k = rot(k_ref[0, :, h * RET_DK:(h + 1) * RET_DK]) * (RET_DK ** -0.5)
        v = v_ref[0, :, h * RET_DV:(h + 1) * RET_DV]
        vb = v.astype(BF16)
        dmask = jnp.where(row >= col, jnp.exp(log_g * dist), 0.0)
        sc = _dot_nt(q.astype(BF16), k.astype(BF16)) * dmask
        s = s_ref[h]
        o = _dot(sc.astype(BF16), vb) + _dot((q * jnp.exp(log_g * (idx + 1.0))).astype(BF16), s.astype(BF16))
        kd = k * jnp.exp(log_g * (c - 1.0 - idx))
        s_ref[h] = s * math.exp(log_g * c) + _dot_tn(kd.astype(BF16), vb)
        o = o * lax.rsqrt(jnp.mean(o * o, axis=-1, keepdims=True) + EPS)
        gate = gate_ref[0, :, h * RET_DV:(h + 1) * RET_DV]
        o_ref[0, :, h * RET_DV:(h + 1) * RET_DV] = (o * gate * jax.nn.sigmoid(gate)).astype(o_ref.dtype)


def _retention(z3, q_blk, cos, sin):
    b, t, _ = z3.shape
    c = RET_CHUNK
    qk = RET_HEADS * RET_DK
    v_blk = (q_blk * qk + 2 * qk) // HALF
    return pl.pallas_call(
        _ret_kernel,
        grid=(b, t // c),
        in_specs=[
            pl.BlockSpec((1, c, qk), lambda bi, ni: (bi, ni, q_blk)),
            pl.BlockSpec((1, c, qk), lambda bi, ni: (bi, ni, q_blk + 1)),
            pl.BlockSpec((1, c, HALF), lambda bi, ni: (bi, ni, v_blk)),
            pl.BlockSpec((1, c, HALF), lambda bi, ni: (bi, ni, v_blk + 1)),
            pl.BlockSpec((c, RET_DK), lambda bi, ni: (ni, 0)),
            pl.BlockSpec((c, RET_DK), lambda bi, ni: (ni, 0)),
        ],
        out_specs=pl.BlockSpec((1, c, HALF), lambda bi, ni: (bi, ni, 0)),
        out_shape=jax.ShapeDtypeStruct((b, t, HALF), BF16),
        scratch_shapes=[pltpu.VMEM((RET_HEADS, RET_DK, RET_DV), F32)],
        compiler_params=_cparams("parallel", "arbitrary"),
        name="retention_chunk",
    )(z3, z3, z3, z3, cos, sin)


def _even_mixer(hn, bsz, seq, w_in, w_out, conv_w, a_log, dt_bias, norm_w, lam_re, lam_im, b_re, b_im,
                c_re, c_im, d_skip, log_step, w_glu, res, prec):
    m = bsz * seq
    n_small = 2 * GDN_HEADS
    w_main = jnp.concatenate([w_in[:, :4 * HALF], w_in[:, 4 * HALF + n_small:]], axis=1).astype(BF16)
    w_small = jnp.pad(w_in[:, 4 * HALF:4 * HALF + n_small], ((0, 0), (0, LANES - n_small))).astype(BF16)
    z3 = _matmul([(hn, w_main)]).reshape(bsz, seq, 5 * HALF)
    zs3 = _matmul([(hn, w_small)]).reshape(bsz, seq, LANES)
    q, k, v = _gdn_prep(z3, conv_w.astype(F32))
    ya = _gdn(q, k, v, z3, zs3, a_log, dt_bias, norm_w, prec)
    a_re, a_im, bb_re, bb_im = _s5_params(lam_re, lam_im, log_step, b_re, b_im)
    bd_re, bd_im, cd_re, cd_im = _s5_block_diag(bb_re, bb_im, c_re, c_im)
    yg = _s5_scan(z3, 4 * HALF // LANES, bd_re, bd_im, cd_re, cd_im, a_re, a_im, d_skip)
    yb = _glu(yg.reshape(m, HALF), w_glu.astype(BF16))
    w_out = w_out.astype(BF16)
    return _matmul([(ya.reshape(m, HALF), w_out[:HALF]), (yb, w_out[HALF:])], res=res)


def _odd_mixer(hn, bsz, seq, w_in, w_out, shift_mu, w0, w2, a0, a2, g2, k_k, k_a, r_k, ln_w, ln_b, res, prec):
    m = bsz * seq
    n_main = 3 * HALF
    n_lora = 64 + 64 + 160
    w_main = jnp.concatenate([w_in[:, :n_main], w_in[:, n_main + n_lora:]], axis=1).astype(BF16)
    w_lora = jnp.pad(w_in[:, n_main:n_main + n_lora], ((0, 0), (0, RWKV_LORA_PAD - n_lora))).astype(BF16)
    z3 = _matmul([(hn, w_main)]).reshape(bsz, seq, 6 * HALF)
    zl3 = _matmul([(hn, w_lora)]).reshape(bsz, seq, RWKV_LORA_PAD)
    mu_main = shift_mu[:n_main].astype(F32).reshape(1, n_main)
    mu_lora = jnp.pad(shift_mu[n_main:], (0, RWKV_LORA_PAD - n_lora)).astype(F32).reshape(1, RWKV_LORA_PAD)
    w2p = jnp.pad(w2, ((0, LANES - 64), (0, 0))).astype(BF16)
    a2p = jnp.pad(a2, ((64, 0), (0, 0))).astype(BF16)
    g2p = jnp.pad(g2, ((0, RWKV_LORA_PAD - LANES - 160), (0, 0))).astype(BF16)
    r, lw, k, v, kk, b, g = _rwkv_prep(z3, zl3, mu_main, mu_lora, w2p, a2p, g2p, w0, a0, k_k, k_a)
    yc = _rwkv(r, lw, k, v, kk, b, g, r_k, ln_w, ln_b, prec)
    cos, sin = _rope_tables(seq)
    yd = _retention(z3, n_main // (RET_HEADS * RET_DK), cos, sin)
    w_out = w_out.astype(BF16)
    return _matmul([(yc.reshape(m, HALF), w_out[:HALF]), (yd.reshape(m, HALF), w_out[HALF:])], res=res)


def kernel(x, norm_mix, norm_ffn, norm_final, ev_w_in, ev_w_out, gdn_conv_w, gdn_a_log, gdn_dt_bias, gdn_norm_w, s5_lam_re, s5_lam_im, s5_b_re, s5_b_im, s5_c_re, s5_c_im, s5_d, s5_log_step, s5_w_glu, od_w_in, od_w_out, rwkv_shift_mu, rwkv_w0, rwkv_w2, rwkv_a0, rwkv_a2, rwkv_g2, rwkv_k_k, rwkv_k_a, rwkv_r_k, rwkv_ln_w, rwkv_ln_b, ffn_w_up, ffn_conv_w, ffn_w_down):
    bsz, seq, d = x.shape
    m = bsz * seq
    depth = norm_mix.shape[0]
    prec = HI
    h = x.reshape(m, d).astype(F32)
    for layer in range(depth):
        hn = _rmsnorm(h, norm_mix[layer], BF16)
        i = layer // 2
        if layer % 2 == 0:
            h = _even_mixer(hn, bsz, seq, ev_w_in[i], ev_w_out[i], gdn_conv_w[i], gdn_a_log[i], gdn_dt_bias[i],
                            gdn_norm_w[i], s5_lam_re[i], s5_lam_im[i], s5_b_re[i], s5_b_im[i], s5_c_re[i],
                            s5_c_im[i], s5_d[i], s5_log_step[i], s5_w_glu[i], h, prec)
        else:
            h = _odd_mixer(hn, bsz, seq, od_w_in[i], od_w_out[i], rwkv_shift_mu[i], rwkv_w0[i], rwkv_w2[i],
                           rwkv_a0[i], rwkv_a2[i], rwkv_g2[i], rwkv_k_k[i], rwkv_k_a[i], rwkv_r_k[i],
                           rwkv_ln_w[i], rwkv_ln_b[i], h, prec)
        hn = _rmsnorm(h, norm_ffn[layer], BF16)
        act = _ffn_up(hn, ffn_w_up[layer].astype(BF16), ffn_conv_w[layer].astype(F32), seq)
        h = _matmul([(act, ffn_w_down[layer].astype(BF16))], res=h, tm=512, tn=512)
    return _rmsnorm(h, norm_final, x.dtype).reshape(bsz, seq, d)
```

```python
import functools
import math

import jax
import jax.numpy as jnp
from jax import lax
from jax.experimental import pallas as pl
from jax.experimental.pallas import tpu as pltpu

F32 = jnp.float32
BF16 = jnp.bfloat16
HI = lax.Precision.HIGHEST

V7X_VMEM_LIMIT_BYTES = 56 * 1024 * 1024
LANES = 128
SUBLANES = 8

EPS = 1e-6
HALF = 1024
GDN_HEADS = 8
GDN_D = 128
GDN_CONV = 4
GDN_CHUNK = 64
S5_GROUP = 16
S5_GROUPS = 64
S5_STATE = 64
S5_TILE_GROUPS = LANES // S5_GROUP
S5_TILE_STATES = S5_TILE_GROUPS * S5_STATE
RWKV_HEAD = 64
RWKV_PAIRS = HALF // LANES
RWKV_CHUNK = 64
RWKV_LORA_PAD = 384
RWKV_GN_EPS = 64e-5
RET_HEADS = 4
RET_DK = 128
RET_DV = 256
RET_CHUNK = 128
ROPE_BASE = 10000.0
INV_BLOCK = 16


def _cparams(*sem):
    return pltpu.CompilerParams(dimension_semantics=sem, vmem_limit_bytes=V7X_VMEM_LIMIT_BYTES)


def _dot_dims(a, b, dims, precision):
    if precision is None:
        a, b = a.astype(BF16), b.astype(BF16)
    batch = ((), ())
    if a.ndim == 3:
        dims = tuple(tuple(d + 1 for d in side) for side in dims)
        batch = ((0,), (0,))
    return lax.dot_general(a, b, (dims, batch), preferred_element_type=F32, precision=precision)


def _dot(a, b, precision=None):
    return _dot_dims(a, b, ((1,), (0,)), precision)


def _dot_nt(a, b, precision=None):
    return _dot_dims(a, b, ((1,), (1,)), precision)


def _dot_tn(a, b, precision=None):
    return _dot_dims(a, b, ((0,), (0,)), precision)


def _iota2(shape):
    return lax.broadcasted_iota(jnp.int32, shape, 0), lax.broadcasted_iota(jnp.int32, shape, 1)


def _unit_lower_inverse(a, row, col, sub, precision):
    assert sub == 4 * INV_BLOCK
    shift = INV_BLOCK.bit_length() - 1
    eye = jnp.where(row == col, 1.0, 0.0).astype(F32)
    ad = jnp.where((row >> shift) == (col >> shift), a, 0.0)
    ao = a - ad
    a2 = _dot(ad, ad, precision)
    a4 = _dot(a2, a2, precision)
    a8 = _dot(a4, a4, precision)
    td = _dot(_dot(eye - ad, eye + a2, precision), _dot(eye + a4, eye + a8, precision), precision)
    n = _dot(td, ao, precision)
    n2 = _dot(n, n, precision)
    return _dot(_dot(eye - n, eye + n2, precision), td, precision)


def _rmsnorm_kernel(x_ref, w_ref, o_ref):
    x = x_ref[...]
    ms = jnp.mean(x * x, axis=-1, keepdims=True)
    o_ref[...] = (x * lax.rsqrt(ms + EPS) * w_ref[...]).astype(o_ref.dtype)


def _rmsnorm(x, w, out_dtype):
    m, d = x.shape
    tm = min(512, m)
    return pl.pallas_call(
        _rmsnorm_kernel,
        grid=(m // tm,),
        in_specs=[pl.BlockSpec((tm, d), lambda i: (i, 0)), pl.BlockSpec((1, d), lambda i: (0, 0))],
        out_specs=pl.BlockSpec((tm, d), lambda i: (i, 0)),
        out_shape=jax.ShapeDtypeStruct((m, d), out_dtype),
        compiler_params=_cparams("parallel"),
        name="rmsnorm",
    )(x, w.reshape(1, d).astype(F32))


def _mm_kernel(*refs, n_pairs, has_res):
    n_in = 2 * n_pairs + int(has_res)
    o_ref = refs[n_in]
    wb_refs = refs[n_in + 1:]

    @pl.when(pl.program_id(1) == 0)
    def _():
        for p in range(n_pairs):
            wb_refs[p][...] = refs[2 * p + 1][...].astype(BF16)

    acc = None
    for p in range(n_pairs):
        d = _dot(refs[2 * p][...], wb_refs[p][...])
        acc = d if acc is None else acc + d
    if has_res:
        acc = acc + refs[2 * n_pairs][...]
    o_ref[...] = acc.astype(o_ref.dtype)


def _pick_tile(n, prefs):
    for t in prefs:
        if n % t == 0:
            return t
    return n


def _matmul(pairs, n=None, res=None, out_dtype=F32, tm=1024, tn=None):
    m = pairs[0][0].shape[0]
    n = n or pairs[0][1].shape[1]
    tm = min(tm, m)
    tn = tn or _pick_tile(n, (512, 384, 256, 128))
    in_specs, args, scratch = [], [], []
    for a, w, r in pairs:
        k = a.shape[1]
        in_specs += [pl.BlockSpec((tm, k), lambda j, i: (i, 0)),
                     pl.BlockSpec((k, tn), lambda j, i, r=r: (r, j))]
        args += [a, w]
        scratch.append(pltpu.VMEM((k, tn), BF16))
    if res is not None:
        in_specs.append(pl.BlockSpec((tm, tn), lambda j, i: (i, j)))
        args.append(res)
    return pl.pallas_call(
        functools.partial(_mm_kernel, n_pairs=len(pairs), has_res=res is not None),
        grid=(n // tn, m // tm),
        in_specs=in_specs,
        out_specs=pl.BlockSpec((tm, tn), lambda j, i: (i, j)),
        out_shape=jax.ShapeDtypeStruct((m, n), out_dtype),
        scratch_shapes=scratch,
        compiler_params=_cparams("parallel", "arbitrary"),
        name="matmul",
    )(*args)


def _ffn_up_kernel(h_ref, wg_ref, wv_ref, cg_ref, cv_ref, o_ref, carry_ref, wgb_ref, wvb_ref, *, blocks_per_seq):
    i = pl.program_id(1)

    @pl.when(i == 0)
    def _():
        wgb_ref[...] = wg_ref[...].astype(BF16)
        wvb_ref[...] = wv_ref[...].astype(BF16)

    @pl.when(i % blocks_per_seq == 0)
    def _():
        carry_ref[...] = jnp.zeros_like(carry_ref)

    h = h_ref[...]
    zg = _dot(h, wgb_ref[...])
    zv = _dot(h, wvb_ref[...])
    tm = zg.shape[0]
    row = lax.broadcasted_iota(jnp.int32, zg.shape, 0)

    def conv(z, c_ref, prev):
        m1 = jnp.where(row == 0, prev[7:8, :], pltpu.roll(z, 1, 0))
        m2 = jnp.where(row == 0, prev[6:7, :], jnp.where(row == 1, prev[7:8, :], pltpu.roll(z, 2, 0)))
        c = c_ref[...]
        return c[0:1, :] * m2 + c[1:2, :] * m1 + c[2:3, :] * z

    g = conv(zg, cg_ref, carry_ref[0])
    v = conv(zv, cv_ref, carry_ref[1])
    carry_ref[0] = zg[tm - SUBLANES:, :]
    carry_ref[1] = zv[tm - SUBLANES:, :]
    o_ref[...] = (g * jax.nn.sigmoid(g) * v).astype(o_ref.dtype)


def _ffn_up(hn, w_up, conv_w, seq):
    m, d = hn.shape
    f = w_up.shape[1] // 2
    tm = min(1024, seq)
    tn = _pick_tile(f, (512, 256, 128))
    nj = f // tn
    return pl.pallas_call(
        functools.partial(_ffn_up_kernel, blocks_per_seq=seq // tm),
        grid=(nj, m // tm),
        in_specs=[
            pl.BlockSpec((tm, d), lambda j, i: (i, 0)),
            pl.BlockSpec((d, tn), lambda j, i: (0, j)),
            pl.BlockSpec((d, tn), lambda j, i: (0, j + nj)),
            pl.BlockSpec((3, tn), lambda j, i: (0, j)),
            pl.BlockSpec((3, tn), lambda j, i: (0, j + nj)),
        ],
        out_specs=pl.BlockSpec((tm, tn), lambda j, i: (i, j)),
        out_shape=jax.ShapeDtypeStruct((m, f), BF16),
        scratch_shapes=[pltpu.VMEM((2, SUBLANES, tn), F32), pltpu.VMEM((d, tn), BF16), pltpu.VMEM((d, tn), BF16)],
        compiler_params=_cparams("parallel", "arbitrary"),
        name="ffn_up_conv",
    )(hn, w_up, w_up, conv_w, conv_w)


def _gdn_prep_kernel(zq_ref, zk_ref, zv_ref, cq_ref, ck_ref, cv_ref, oq_ref, ok_ref, ov_ref):
    row = lax.broadcasted_iota(jnp.int32, zq_ref.shape[1:], 0)

    def conv_silu(z_ref, c_ref):
        z = z_ref[0]
        c = c_ref[...]
        acc = c[GDN_CONV - 1:GDN_CONV, :] * z
        for s in range(1, GDN_CONV):
            acc = acc + c[GDN_CONV - 1 - s:GDN_CONV - s, :] * jnp.where(row >= s, pltpu.roll(z, s, 0), 0.0)
        return acc * jax.nn.sigmoid(acc)

    def l2n(t):
        return t * lax.rsqrt(jnp.sum(t * t, axis=-1, keepdims=True) + 1e-6)

    oq_ref[0] = l2n(conv_silu(zq_ref, cq_ref)) * (GDN_D ** -0.5)
    ok_ref[0] = l2n(conv_silu(zk_ref, ck_ref))
    ov_ref[0] = conv_silu(zv_ref, cv_ref)


def _gdn_prep(z3, conv_w):
    b, t, _ = z3.shape
    h = GDN_HEADS
    zspec = lambda off: pl.BlockSpec((1, t, GDN_D), lambda bi, hi: (bi, 0, hi + off))
    cspec = lambda off: pl.BlockSpec((GDN_CONV, GDN_D), lambda bi, hi: (0, hi + off))
    ospec = pl.BlockSpec((1, t, GDN_D), lambda bi, hi: (bi, 0, hi))
    oshape = jax.ShapeDtypeStruct((b, t, HALF), F32)
    return pl.pallas_call(
        _gdn_prep_kernel,
        grid=(b, h),
        in_specs=[zspec(0), zspec(h), zspec(2 * h), cspec(0), cspec(h), cspec(2 * h)],
        out_specs=[ospec, ospec, ospec],
        out_shape=[oshape, oshape, oshape],
        compiler_params=_cparams("parallel", "parallel"),
        name="gdn_prep",
    )(z3, z3, z3, conv_w, conv_w, conv_w)


def _gdn_kernel(q_ref, k_ref, v_ref, gate_ref, zs_ref, alog_ref, dtb_ref, nw_ref, o_ref, s_ref, *, prec):
    c = GDN_CHUNK

    @pl.when(pl.program_id(1) == 0)
    def _():
        s_ref[...] = jnp.zeros_like(s_ref)

    zs = zs_ref[0]
    beta_all = jax.nn.sigmoid(zs)
    g_all = -jnp.exp(alog_ref[...]) * jax.nn.softplus(zs + dtb_ref[...])
    row, col = _iota2((c, c))
    lower = jnp.where(row >= col, 1.0, 0.0).astype(F32)
    heads = range(GDN_HEADS)
    per_head = lambda ref: jnp.stack([ref[0, :, h * GDN_D:(h + 1) * GDN_D] for h in heads])
    q, k, v = per_head(q_ref), per_head(k_ref), per_head(v_ref)
    beta = jnp.stack([beta_all[:, h:h + 1] for h in heads])
    g = jnp.stack([g_all[:, GDN_HEADS + h:GDN_HEADS + h + 1] for h in heads])
    lower3 = jnp.broadcast_to(lower, (GDN_HEADS, c, c))
    diff = _dot(lower3, jnp.where(row > col, jnp.broadcast_to(g, (GDN_HEADS, c, c)), 0.0), HI)
    decay = jnp.where(row >= col, jnp.exp(diff), 0.0)
    gc = _dot(lower3, jnp.broadcast_to(g, (GDN_HEADS, c, GDN_D)), HI)
    gl = gc[:, c - 1:c, :]
    egc = jnp.exp(gc)
    kb = k * beta
    a = jnp.where(row > col, _dot_nt(kb, k, prec) * decay, 0.0)
    tinv = _unit_lower_inverse(a, row, col, c, prec)
    u = _dot(tinv, v * beta, prec)
    w = _dot(tinv, kb * egc, prec)
    attn = _dot_nt(q, k, prec) * decay
    s = s_ref[...]
    v_new = u - _dot(w, s, prec)
    o = _dot(q * egc, s, prec) + _dot(attn, v_new, prec)
    s_ref[...] = s * jnp.exp(gl) + _dot_tn(k * jnp.exp(gl - gc), v_new, prec)
    o = o * lax.rsqrt(jnp.mean(o * o, axis=-1, keepdims=True) + EPS) * nw_ref[...]
    for h in heads:
        gate = gate_ref[0, :, h * GDN_D:(h + 1) * GDN_D]
        o_ref[0, :, h * GDN_D:(h + 1) * GDN_D] = (o[h] * gate * jax.nn.sigmoid(gate)).astype(o_ref.dtype)


def _gdn(q, k, v, z3, zs3, zs_blk, a_log, dt_bias, norm_w, prec):
    b, t, _ = q.shape
    c = GDN_CHUNK
    pad = lambda p: jnp.zeros((1, LANES), F32).at[0, GDN_HEADS:2 * GDN_HEADS].set(p.astype(F32))
    blk = pl.BlockSpec((1, c, HALF), lambda bi, ni: (bi, ni, 0))
    vec = pl.BlockSpec((1, LANES), lambda bi, ni: (0, 0))
    return pl.pallas_call(
        functools.partial(_gdn_kernel, prec=prec),
        grid=(b, t // c),
        in_specs=[blk, blk, blk,
                  pl.BlockSpec((1, c, HALF), lambda bi, ni: (bi, ni, 3)),
                  pl.BlockSpec((1, c, LANES), lambda bi, ni: (bi, ni, zs_blk)),
                  vec, vec, vec],
        out_specs=blk,
        out_shape=jax.ShapeDtypeStruct((b, t, HALF), BF16),
        scratch_shapes=[pltpu.VMEM((GDN_HEADS, GDN_D, GDN_D), F32)],
        compiler_params=_cparams("parallel", "arbitrary"),
        name="gdn_chunk",
    )(q, k, v, z3, zs3, pad(a_log), pad(dt_bias), norm_w.reshape(1, GDN_D).astype(F32))


def _s5_param_kernel(lr_ref, li_ref, dt_ref, br_ref, bi_ref, ar_ref, ai_ref, bbr_ref, bbi_ref):
    lr, li, dt = lr_ref[...], li_ref[...], dt_ref[...]
    step = jnp.exp(dt)
    mag = jnp.exp(lr * step)
    ang = li * step
    ab_re, ab_im = mag * jnp.cos(ang), mag * jnp.sin(ang)
    den = lr * lr + li * li
    nr = ab_re - 1.0
    f_re = (nr * lr + ab_im * li) / den
    f_im = (ab_im * lr - nr * li) / den
    br, bi = br_ref[...], bi_ref[...]
    ar_ref[...] = ab_re
    ai_ref[...] = ab_im
    bbr_ref[...] = f_re * br - f_im * bi
    bbi_ref[...] = f_re * bi + f_im * br


def _s5_params(lam_re, lam_im, log_step, b_re, b_im):
    gp = S5_GROUPS * S5_STATE
    row = lambda x: x.astype(F32).reshape(1, gp)
    bt = lambda x: jnp.transpose(x.astype(F32), (2, 0, 1)).reshape(S5_GROUP, gp)
    vshape = jax.ShapeDtypeStruct((1, gp), F32)
    mshape = jax.ShapeDtypeStruct((S5_GROUP, gp), F32)
    return pl.pallas_call(
        _s5_param_kernel,
        out_shape=[vshape, vshape, mshape, mshape],
        name="s5_params",
    )(row(lam_re), row(lam_im), row(jnp.repeat(log_step[:, None], S5_STATE, axis=1)), bt(b_re), bt(b_im))


def _s5_scan_kernel(u_ref, bdr_ref, bdi_ref, cdr_ref, cdi_ref, ar_ref, ai_ref, d_ref, o_ref, xr_ref, xi_ref):
    t = u_ref.shape[1]
    n = S5_TILE_STATES
    rt = min(512, t)
    for r0 in range(0, t, rt):
        ub = u_ref[0, r0:r0 + rt, :].astype(BF16)
        xr_ref[r0:r0 + rt, :] = _dot(ub, bdr_ref[0])
        xi_ref[r0:r0 + rt, :] = _dot(ub, bdi_ref[0])

    a1r, a1i = ar_ref[...], ai_ref[...]
    cmul = lambda xr, xi, yr, yi: (xr * yr - xi * yi, xr * yi + xi * yr)
    a2r, a2i = cmul(a1r, a1i, a1r, a1i)
    a4r, a4i = cmul(a2r, a2i, a2r, a2i)
    pows = [(a1r, a1i), (a2r, a2i)]
    for _ in range(2, SUBLANES):
        pows.append(cmul(pows[-1][0], pows[-1][1], a1r, a1i))
    row = lax.broadcasted_iota(jnp.int32, (SUBLANES, n), 0)
    pr = jnp.zeros((SUBLANES, n), F32)
    pi = jnp.zeros((SUBLANES, n), F32)
    for r in range(SUBLANES):
        pr = jnp.where(row == r, pows[r][0], pr)
        pi = jnp.where(row == r, pows[r][1], pi)
    levels = [(1, jnp.broadcast_to(a1r, (SUBLANES, n)), jnp.broadcast_to(a1i, (SUBLANES, n))),
              (2, jnp.broadcast_to(a2r, (SUBLANES, n)), jnp.broadcast_to(a2i, (SUBLANES, n))),
              (4, jnp.broadcast_to(a4r, (SUBLANES, n)), jnp.broadcast_to(a4i, (SUBLANES, n)))]

    def body(blk, carry):
        cr, ci = carry
        r0 = pl.multiple_of(blk * SUBLANES, SUBLANES)
        xr = xr_ref[pl.ds(r0, SUBLANES), :]
        xi = xi_ref[pl.ds(r0, SUBLANES), :]
        for d, ar, ai in levels:
            sr = jnp.where(row >= d, pltpu.roll(xr, d, 0), 0.0)
            si = jnp.where(row >= d, pltpu.roll(xi, d, 0), 0.0)
            xr, xi = xr + ar * sr - ai * si, xi + ar * si + ai * sr
        xr, xi = xr + pr * cr - pi * ci, xi + pr * ci + pi * cr
        xr_ref[pl.ds(r0, SUBLANES), :] = xr
        xi_ref[pl.ds(r0, SUBLANES), :] = xi
        return xr[SUBLANES - 1:, :], xi[SUBLANES - 1:, :]

    zero = jnp.zeros((1, n), F32)
    lax.fori_loop(0, t // SUBLANES, body, (zero, zero))

    for r0 in range(0, t, rt):
        u = u_ref[0, r0:r0 + rt, :]
        y = (_dot(xr_ref[r0:r0 + rt, :].astype(BF16), cdr_ref[0])
             - _dot(xi_ref[r0:r0 + rt, :].astype(BF16), cdi_ref[0]) + u * d_ref[...])
        o_ref[0, r0:r0 + rt, :] = jax.nn.gelu(y)


def _s5_scan(z3, u_off, bd_re, bd_im, cd_re, cd_im, a_re, a_im, d_skip):
    b, t, _ = z3.shape
    nt = HALF // LANES
    n = S5_TILE_STATES
    return pl.pallas_call(
        _s5_scan_kernel,
        grid=(b, nt),
        in_specs=[
            pl.BlockSpec((1, t, LANES), lambda bi, j: (bi, 0, j + u_off)),
            pl.BlockSpec((1, LANES, n), lambda bi, j: (j, 0, 0)),
            pl.BlockSpec((1, LANES, n), lambda bi, j: (j, 0, 0)),
            pl.BlockSpec((1, n, LANES), lambda bi, j: (j, 0, 0)),
            pl.BlockSpec((1, n, LANES), lambda bi, j: (j, 0, 0)),
            pl.BlockSpec((1, n), lambda bi, j: (0, j)),
            pl.BlockSpec((1, n), lambda bi, j: (0, j)),
            pl.BlockSpec((1, LANES), lambda bi, j: (0, j)),
        ],
        out_specs=pl.BlockSpec((1, t, LANES), lambda bi, j: (bi, 0, j)),
        out_shape=jax.ShapeDtypeStruct((b, t, HALF), F32),
        scratch_shapes=[pltpu.VMEM((t, n), F32), pltpu.VMEM((t, n), F32)],
        compiler_params=_cparams("parallel", "parallel"),
        name="s5_scan",
    )(z3, bd_re, bd_im, cd_re, cd_im, a_re, a_im, d_skip.reshape(1, HALF).astype(F32))


def _glu_kernel(y_ref, w_ref, yt_ref, o_ref):
    gate = _dot(y_ref[...].astype(BF16), w_ref[...])
    o_ref[...] = (yt_ref[...] * jax.nn.sigmoid(gate)).astype(o_ref.dtype)


def _glu(y, w):
    m, k = y.shape
    tm = min(1024, m)
    tn = 512
    return pl.pallas_call(
        _glu_kernel,
        grid=(m // tm, k // tn),
        in_specs=[pl.BlockSpec((tm, k), lambda i, j: (i, 0)),
                  pl.BlockSpec((k, tn), lambda i, j: (0, j)),
                  pl.BlockSpec((tm, tn), lambda i, j: (i, j))],
        out_specs=pl.BlockSpec((tm, tn), lambda i, j: (i, j)),
        out_shape=jax.ShapeDtypeStruct((m, k), BF16),
        compiler_params=_cparams("parallel", "parallel"),
        name="s5_glu",
    )(y, w, y)


def _s5_block_diag(bb_re, bb_im, c_re, c_im):
    nt, tg = HALF // LANES, S5_TILE_GROUPS
    eye = jnp.eye(tg, dtype=F32)

    def bmap(bb):
        x = bb.reshape(S5_GROUP, nt, tg, S5_STATE)
        x = jnp.einsum('cjgp,gh->jgchp', x, eye)
        return x.reshape(nt, LANES, S5_TILE_STATES).astype(BF16)

    def cmap(cc):
        x = cc.astype(F32).reshape(nt, tg, S5_GROUP, S5_STATE)
        x = jnp.einsum('jgcp,gh->jgphc', x, eye)
        return x.reshape(nt, S5_TILE_STATES, LANES).astype(BF16)

    return bmap(bb_re), bmap(bb_im), cmap(c_re), cmap(c_im)


def _seg_sum(x, seg):
    row, col = _iota2((LANES, LANES))
    shift = seg.bit_length() - 1
    ones = jnp.where((row >> shift) == (col >> shift), 1.0, 0.0).astype(F32)
    return _dot(x, ones, HI)


def _rwkv_prep_kernel(zr_ref, zk_ref, zv_ref, zl_ref, hr_ref, hk_ref, hv_ref, hl_ref,
                      mur_ref, muk_ref, muv_ref, mul_ref, w2_ref, a2_ref, g2_ref,
                      w0_ref, a0_ref, kk_ref, ka_ref,
                      or_ref, ow_ref, ok_ref, ov_ref, okk_ref, ob_ref, og_ref):
    first = pl.program_id(1) == 0

    def shift_mix(z_ref, halo_ref, mu_ref):
        z = z_ref[0]
        row = lax.broadcasted_iota(jnp.int32, z.shape, 0)
        prev = jnp.where(first, 0.0, halo_ref[0, SUBLANES - 1:SUBLANES, :])
        zm1 = jnp.where(row == 0, prev, pltpu.roll(z, 1, 0))
        return z + (zm1 - z) * mu_ref[...]

    r = shift_mix(zr_ref, hr_ref, mur_ref)
    k = shift_mix(zk_ref, hk_ref, muk_ref)
    v = shift_mix(zv_ref, hv_ref, muv_ref)
    zl = shift_mix(zl_ref, hl_ref, mul_ref)
    wa = zl[:, :LANES]
    w = w0_ref[...] + _dot(jnp.tanh(wa).astype(BF16), w2_ref[...])
    w = -jax.nn.softplus(-w) - 0.5
    a = jax.nn.sigmoid(a0_ref[...] + _dot(wa.astype(BF16), a2_ref[...]))
    g = _dot(jax.nn.sigmoid(zl[:, LANES:]).astype(BF16), g2_ref[...])
    kk = k * kk_ref[...]
    kk = kk * lax.rsqrt(_seg_sum(kk * kk, RWKV_HEAD) + 1e-6)
    or_ref[0] = r
    ow_ref[0] = -jnp.exp(w)
    ok_ref[0] = k * (1.0 + (a - 1.0) * ka_ref[...])
    ov_ref[0] = v
    okk_ref[0] = kk
    ob_ref[0] = kk * a
    og_ref[0] = g


def _rwkv_prep(z3, zl3, lora_blk, mu_main, mu_lora, w2p, a2p, g2p, w0, a0, k_k, k_a):
    b, t, _ = z3.shape
    tt = min(512, t)
    nh = tt // SUBLANES
    npair = RWKV_PAIRS
    lw = RWKV_LORA_PAD
    main = lambda off: pl.BlockSpec((1, tt, LANES), lambda bi, ti, j: (bi, ti, j + off))
    halo = lambda off: pl.BlockSpec((1, SUBLANES, LANES),
                                    lambda bi, ti, j: (bi, jnp.maximum(ti * nh - 1, 0), j + off))
    vec = lambda off: pl.BlockSpec((1, LANES), lambda bi, ti, j: (0, j + off))
    ospec = pl.BlockSpec((1, tt, LANES), lambda bi, ti, j: (bi, ti, j))
    oshape = jax.ShapeDtypeStruct((b, t, HALF), F32)
    row = lambda x: x.astype(F32).reshape(1, -1)
    return pl.pallas_call(
        _rwkv_prep_kernel,
        grid=(b, t // tt, npair),
        in_specs=[
            main(0), main(npair), main(2 * npair),
            pl.BlockSpec((1, tt, lw), lambda bi, ti, j: (bi, ti, lora_blk)),
            halo(0), halo(npair), halo(2 * npair),
            pl.BlockSpec((1, SUBLANES, lw), lambda bi, ti, j: (bi, jnp.maximum(ti * nh - 1, 0), lora_blk)),
            vec(0), vec(npair), vec(2 * npair),
            pl.BlockSpec((1, lw), lambda bi, ti, j: (0, 0)),
            pl.BlockSpec((LANES, LANES), lambda bi, ti, j: (0, j)),
            pl.BlockSpec((LANES, LANES), lambda bi, ti, j: (0, j)),
            pl.BlockSpec((lw - LANES, LANES), lambda bi, ti, j: (0, j)),
            vec(0), vec(0), vec(0), vec(0),
        ],
        out_specs=[ospec] * 7,
        out_shape=[oshape] * 7,
        compiler_params=_cparams("parallel", "parallel", "parallel"),
        name="rwkv_prep",
    )(z3, z3, z3, zl3, z3, z3, z3, zl3, mu_main, mu_main, mu_main, mu_lora,
      w2p, a2p, g2p, row(w0), row(a0), row(k_k), row(k_a))


def _rwkv_kernel(r_ref, w_ref, k_ref, v_ref, kk_ref, b_ref, g_ref, rk_ref, lnw_ref, lnb_ref, o_ref, h_ref, *, prec):
    c = RWKV_CHUNK
    c2 = 2 * c

    @pl.when(pl.program_id(1) == 0)
    def _():
        h_ref[...] = jnp.zeros_like(h_ref)

    row1, col1 = _iota2((c, c))
    lower = jnp.where(row1 >= col1, 1.0, 0.0).astype(F32)
    row, col = _iota2((c2, c2))
    same = (row >> 6) == (col >> 6)
    strict = jnp.logical_and(same, row > col)
    incl = jnp.logical_and(same, row >= col)
    lane = lax.broadcasted_iota(jnp.int32, (c, LANES), 1)
    first_head = lane < RWKV_HEAD

    def stack2(x):
        return jnp.concatenate([jnp.where(first_head, x, 0.0), jnp.where(first_head, 0.0, x)], axis=1)

    pairs = range(RWKV_PAIRS)
    npair = RWKV_PAIRS
    per_pair = lambda x: jnp.stack([x[:, p * LANES:(p + 1) * LANES] for p in pairs])
    seg_sum = lambda x: _seg_sum(x.reshape(npair * c, LANES), RWKV_HEAD).reshape(npair, c, LANES)
    r, lw, k, v = per_pair(r_ref[0]), per_pair(w_ref[0]), per_pair(k_ref[0]), per_pair(v_ref[0])
    kk, b = per_pair(kk_ref[0]), per_pair(b_ref[0])
    cl = per_pair(_dot(lower, w_ref[0], HI))
    cl_last = cl[:, c - 1:c, :]
    e_neg = jnp.exp(-cl)
    e_tail = jnp.exp(cl_last - cl)
    kk2 = stack2(kk * jnp.exp(cl - lw))
    r2 = stack2(r * jnp.exp(cl))
    b2 = stack2(b * e_neg)
    k2 = stack2(k * e_neg)
    v2 = stack2(v)
    bd2 = stack2(b * e_tail)
    kd2 = stack2(k * e_tail)
    a_ab = jnp.where(strict, _dot_nt(kk2, b2, prec), 0.0)
    a_ak = jnp.where(strict, _dot_nt(kk2, k2, prec), 0.0)
    r_b = jnp.where(incl, _dot_nt(r2, b2, prec), 0.0)
    r_k = jnp.where(incl, _dot_nt(r2, k2, prec), 0.0)
    tinv = _unit_lower_inverse(a_ab, row, col, c, prec)
    wk = _dot(tinv, kk2, prec)
    tv = _dot(tinv, _dot(a_ak, v2, prec), prec)
    ht = h_ref[...]
    u = -_dot_nt(wk, ht, prec) - tv
    y2 = _dot_nt(r2, ht, prec) + _dot(r_b, u, prec) + _dot(r_k, v2, prec)
    h_ref[...] = ht * jnp.exp(cl_last) + _dot_tn(u, bd2, prec) + _dot_tn(v2, kd2, prec)
    y = y2[:, :c, :] + y2[:, c:, :]
    mu = seg_sum(y) * (1.0 / RWKV_HEAD)
    d = y - mu
    var = seg_sum(d * d) * (1.0 / RWKV_HEAD)
    yn = d * lax.rsqrt(var + RWKV_GN_EPS) * per_pair(lnw_ref[...]) + per_pair(lnb_ref[...])
    out = (yn + seg_sum(r * k * per_pair(rk_ref[...])) * v) * per_pair(g_ref[0])
    for p in pairs:
        o_ref[0, :, p * LANES:(p + 1) * LANES] = out[p].astype(o_ref.dtype)


def _rwkv(r, w, k, v, kk, b, g, r_k, ln_w, ln_b, prec):
    bsz, t, _ = r.shape
    c = RWKV_CHUNK
    blk = pl.BlockSpec((1, c, HALF), lambda bi, ni: (bi, ni, 0))
    vec = pl.BlockSpec((1, HALF), lambda bi, ni: (0, 0))
    row = lambda x: x.astype(F32).reshape(1, HALF)
    return pl.pallas_call(
        functools.partial(_rwkv_kernel, prec=prec),
        grid=(bsz, t // c),
        in_specs=[blk] * 7 + [vec] * 3,
        out_specs=blk,
        out_shape=jax.ShapeDtypeStruct((bsz, t, HALF), BF16),
        scratch_shapes=[pltpu.VMEM((RWKV_PAIRS, LANES, LANES), F32)],
        compiler_params=_cparams("parallel", "arbitrary"),
        name="rwkv_chunk",
    )(r, w, k, v, kk, b, g, row(r_k), row(ln_w), row(ln_b))


def _rope_kernel(f_ref, cos_ref, sin_ref):
    t = cos_ref.shape[0]
    pos = lax.broadcasted_iota(jnp.int32, (t, LANES), 0).astype(F32)
    lane = lax.broadcasted_iota(jnp.int32, (t, LANES), 1)
    ang = pos * f_ref[...]
    cos_ref[...] = jnp.cos(ang)
    sin_ref[...] = jnp.where(lane < RET_DK // 2, -1.0, 1.0) * jnp.sin(ang)


def _rope_tables(t):
    inv_freq = ROPE_BASE ** (-jnp.linspace(0.0, 1.0, RET_DK // 2, dtype=F32))
    f2 = jnp.concatenate([inv_freq, inv_freq]).reshape(1, RET_DK)
    shape = jax.ShapeDtypeStruct((t, RET_DK), F32)
    return pl.pallas_call(_rope_kernel, out_shape=[shape, shape], name="rope_tables")(f2)


def _ret_kernel(q_ref, k_ref, v_ref, gate_ref, cos_ref, sin_ref, o_ref, s_ref):
    c = RET_CHUNK

    @pl.when(pl.program_id(1) == 0)
    def _():
        s_ref[...] = jnp.zeros_like(s_ref)

    row, col = _iota2((c, c))
    dist = (row - col).astype(F32)
    idx = lax.broadcasted_iota(jnp.int32, (c, 1), 0).astype(F32)
    cos, sin = cos_ref[...], sin_ref[...]
    rot = lambda x: x * cos + pltpu.roll(x, RET_DK // 2, 1) * sin

    for h in range(RET_HEADS):
        log_g = math.log(1.0 - 2.0 ** (-5.0 - h))
        q = rot(q_ref[0, :, h * RET_DK:(h + 1) * RET_DK])
        k = rot(k_ref[0, :, h * RET_DK:(h + 1) * RET_DK]) * (RET_DK ** -0.5)
        v = v_ref[0, :, h * RET_DV:(h + 1) * RET_DV]
        vb = v.astype(BF16)
        dmask = jnp.where(row >= col, jnp.exp(log_g * dist), 0.0)
        sc = _dot_nt(q.astype(BF16), k.astype(BF16)) * dmask
        s = s_ref[h]
        o = _dot(sc.astype(BF16), vb) + _dot((q * jnp.exp(log_g * (idx + 1.0))).astype(BF16), s.astype(BF16))
        kd = k * jnp.exp(log_g * (c - 1.0 - idx))
        s_ref[h] = s * math.exp(log_g * c) + _dot_tn(kd.astype(BF16), vb)
        o = o * lax.rsqrt(jnp.mean(o * o, axis=-1, keepdims=True) + EPS)
        gate = gate_ref[0, :, h * RET_DV:(h + 1) * RET_DV]
        o_ref[0, :, h * RET_DV:(h + 1) * RET_DV] = (o * gate * jax.nn.sigmoid(gate)).astype(o_ref.dtype)


def _retention(z3, q_blk, cos, sin):
    b, t, _ = z3.shape
    c = RET_CHUNK
    qk = RET_HEADS * RET_DK
    v_blk = (q_blk * qk + 2 * qk) // HALF
    return pl.pallas_call(
        _ret_kernel,
        grid=(b, t // c),
        in_specs=[
            pl.BlockSpec((1, c, qk), lambda bi, ni: (bi, ni, q_blk)),
            pl.BlockSpec((1, c, qk), lambda bi, ni: (bi, ni, q_blk + 1)),
            pl.BlockSpec((1, c, HALF), lambda bi, ni: (bi, ni, v_blk)),
            pl.BlockSpec((1, c, HALF), lambda bi, ni: (bi, ni, v_blk + 1)),
            pl.BlockSpec((c, RET_DK), lambda bi, ni: (ni, 0)),
            pl.BlockSpec((c, RET_DK), lambda bi, ni: (ni, 0)),
        ],
        out_specs=pl.BlockSpec((1, c, HALF), lambda bi, ni: (bi, ni, 0)),
        out_shape=jax.ShapeDtypeStruct((b, t, HALF), BF16),
        scratch_shapes=[pltpu.VMEM((RET_HEADS, RET_DK, RET_DV), F32)],
        compiler_params=_cparams("parallel", "arbitrary"),
        name="retention_chunk",
    )(z3, z3, z3, z3, cos, sin)


def _even_mixer(hn, bsz, seq, w_in, w_out, conv_w, a_log, dt_bias, norm_w, lam_re, lam_im, b_re, b_im,
                c_re, c_im, d_skip, log_step, w_glu, res, prec):
    m = bsz * seq
    n_main = 4 * HALF
    n_small = 2 * GDN_HEADS
    w_b = jnp.concatenate([w_in[:, n_main + n_small:],
                           jnp.pad(w_in[:, n_main:n_main + n_small], ((0, 0), (0, LANES - n_small)))], axis=1)
    za = _matmul([(hn, w_in, 0)], n=n_main).reshape(bsz, seq, n_main)
    zb = _matmul([(hn, w_b, 0)]).reshape(bsz, seq, HALF + LANES)
    q, k, v = _gdn_prep(za, conv_w.astype(F32))
    ya = _gdn(q, k, v, za, zb, HALF // LANES, a_log, dt_bias, norm_w, prec)
    a_re, a_im, bb_re, bb_im = _s5_params(lam_re, lam_im, log_step, b_re, b_im)
    bd_re, bd_im, cd_re, cd_im = _s5_block_diag(bb_re, bb_im, c_re, c_im)
    yg = _s5_scan(zb, 0, bd_re, bd_im, cd_re, cd_im, a_re, a_im, d_skip)
    yb = _glu(yg.reshape(m, HALF), w_glu.astype(BF16))
    return _matmul([(ya.reshape(m, HALF), w_out, 0), (yb, w_out, 1)], res=res)


def _odd_mixer(hn, bsz, seq, w_in, w_out, shift_mu, w0, w2, a0, a2, g2, k_k, k_a, r_k, ln_w, ln_b, res, prec):
    m = bsz * seq
    n_main = 3 * HALF
    n_lora = 64 + 64 + 160
    w_b = jnp.concatenate([w_in[:, n_main + n_lora:],
                           jnp.pad(w_in[:, n_main:n_main + n_lora], ((0, 0), (0, RWKV_LORA_PAD - n_lora)))], axis=1)
    za = _matmul([(hn, w_in, 0)], n=n_main).reshape(bsz, seq, n_main)
    zb = _matmul([(hn, w_b, 0)]).reshape(bsz, seq, n_main + RWKV_LORA_PAD)
    mu_main = shift_mu[:n_main].astype(F32).reshape(1, n_main)
    mu_lora = jnp.pad(shift_mu[n_main:], (0, RWKV_LORA_PAD - n_lora)).astype(F32).reshape(1, RWKV_LORA_PAD)
    w2p = jnp.pad(w2, ((0, LANES - 64), (0, 0))).astype(BF16)
    a2p = jnp.pad(a2, ((64, 0), (0, 0))).astype(BF16)
    g2p = jnp.pad(g2, ((0, RWKV_LORA_PAD - LANES - 160), (0, 0))).astype(BF16)
    r, lw, k, v, kk, b, g = _rwkv_prep(za, zb, n_main // RWKV_LORA_PAD, mu_main, mu_lora, w2p, a2p, g2p,
                                       w0, a0, k_k, k_a)
    yc = _rwkv(r, lw, k, v, kk, b, g, r_k, ln_w, ln_b, prec)
    cos, sin = _rope_tables(seq)
    yd = _retention(zb, 0, cos, sin)
    return _matmul([(yc.reshape(m, HALF), w_out, 0), (yd.reshape(m, HALF), w_out, 1)], res=res)


def kernel(x, norm_mix, norm_ffn, norm_final, ev_w_in, ev_w_out, gdn_conv_w, gdn_a_log, gdn_dt_bias, gdn_norm_w, s5_lam_re, s5_lam_im, s5_b_re, s5_b_im, s5_c_re, s5_c_im, s5_d, s5_log_step, s5_w_glu, od_w_in, od_w_out, rwkv_shift_mu, rwkv_w0, rwkv_w2, rwkv_a0, rwkv_a2, rwkv_g2, rwkv_k_k, rwkv_k_a, rwkv_r_k, rwkv_ln_w, rwkv_ln_b, ffn_w_up, ffn_conv_w, ffn_w_down):
    bsz, seq, d = x.shape
    m = bsz * seq
    depth = norm_mix.shape[0]
    prec = None
    h = x.reshape(m, d).astype(F32)
    for layer in range(depth):
        hn = _rmsnorm(h, norm_mix[layer], BF16)
        i = layer // 2
        if layer % 2 == 0:
            h = _even_mixer(hn, bsz, seq, ev_w_in[i], ev_w_out[i], gdn_conv_w[i], gdn_a_log[i], gdn_dt_bias[i],
                            gdn_norm_w[i], s5_lam_re[i], s5_lam_im[i], s5_b_re[i], s5_b_im[i], s5_c_re[i],
                            s5_c_im[i], s5_d[i], s5_log_step[i], s5_w_glu[i], h, prec)
        else:
            h = _odd_mixer(hn, bsz, seq, od_w_in[i], od_w_out[i], rwkv_shift_mu[i], rwkv_w0[i], rwkv_w2[i],
                           rwkv_a0[i], rwkv_a2[i], rwkv_g2[i], rwkv_k_k[i], rwkv_k_a[i], rwkv_r_k[i],
                           rwkv_ln_w[i], rwkv_ln_b[i], h, prec)
        hn = _rmsnorm(h, norm_ffn[layer], BF16)
        act = _ffn_up(hn, ffn_w_up[layer], ffn_conv_w[layer].astype(F32), seq)
        h = _matmul([(act, ffn_w_down[layer], 0)], res=h, tm=512, tn=512)
    return _rmsnorm(h, norm_final, x.dtype).reshape(bsz, seq, d)
```

```python
import functools
import math

import jax
import jax.numpy as jnp
from jax import lax
from jax.experimental import pallas as pl
from jax.experimental.pallas import tpu as pltpu

F32 = jnp.float32
BF16 = jnp.bfloat16
HI = lax.Precision.HIGHEST

V7X_VMEM_LIMIT_BYTES = 56 * 1024 * 1024
LANES = 128
SUBLANES = 8

EPS = 1e-6
HALF = 1024
GDN_HEADS = 8
GDN_D = 128
GDN_CONV = 4
GDN_CHUNK = 64
S5_GROUP = 16
S5_GROUPS = 64
S5_STATE = 64
S5_TILE_GROUPS = LANES // S5_GROUP
S5_TILE_STATES = S5_TILE_GROUPS * S5_STATE
RWKV_HEAD = 64
RWKV_PAIRS = HALF // LANES
RWKV_CHUNK = 64
RWKV_LORA_PAD = 384
RWKV_GN_EPS = 64e-5
RET_HEADS = 4
RET_DK = 128
RET_DV = 256
RET_CHUNK = 128
ROPE_BASE = 10000.0
INV_BLOCK = 16


def _cparams(*sem):
    return pltpu.CompilerParams(dimension_semantics=sem, vmem_limit_bytes=V7X_VMEM_LIMIT_BYTES)


def _dot_dims(a, b, dims, precision):
    if precision is None:
        a, b = a.astype(BF16), b.astype(BF16)
    batch = ((), ())
    if a.ndim == 3:
        dims = tuple(tuple(d + 1 for d in side) for side in dims)
        batch = ((0,), (0,))
    return lax.dot_general(a, b, (dims, batch), preferred_element_type=F32, precision=precision)


def _dot(a, b, precision=None):
    return _dot_dims(a, b, ((1,), (0,)), precision)


def _dot_nt(a, b, precision=None):
    return _dot_dims(a, b, ((1,), (1,)), precision)


def _dot_tn(a, b, precision=None):
    return _dot_dims(a, b, ((0,), (0,)), precision)


def _bf16_terms(x):
    terms = []
    for _ in range(3):
        t = x.astype(BF16)
        terms.append(t)
        x = x - t.astype(F32)
    return terms


def _dot_01_left(m01, x):
    mb = m01.astype(BF16)
    return sum(_dot(mb, t) for t in _bf16_terms(x))


def _dot_01_right(x, m01):
    mb = m01.astype(BF16)
    return sum(_dot(t, mb) for t in _bf16_terms(x))


def _iota2(shape):
    return lax.broadcasted_iota(jnp.int32, shape, 0), lax.broadcasted_iota(jnp.int32, shape, 1)


def _unit_lower_inverse(a, row, col, sub, precision):
    assert sub == 4 * INV_BLOCK
    shift = INV_BLOCK.bit_length() - 1
    eye = jnp.where(row == col, 1.0, 0.0).astype(F32)
    ad = jnp.where((row >> shift) == (col >> shift), a, 0.0)
    ao = a - ad
    a2 = _dot(ad, ad, precision)
    a4 = _dot(a2, a2, precision)
    a8 = _dot(a4, a4, precision)
    td = _dot(_dot(eye - ad, eye + a2, precision), _dot(eye + a4, eye + a8, precision), precision)
    n = _dot(td, ao, precision)
    n2 = _dot(n, n, precision)
    return _dot(_dot(eye - n, eye + n2, precision), td, precision)


def _rmsnorm_kernel(x_ref, w_ref, o_ref):
    x = x_ref[...]
    ms = jnp.mean(x * x, axis=-1, keepdims=True)
    o_ref[...] = (x * lax.rsqrt(ms + EPS) * w_ref[...]).astype(o_ref.dtype)


def _rmsnorm(x, w, out_dtype):
    m, d = x.shape
    tm = min(512, m)
    return pl.pallas_call(
        _rmsnorm_kernel,
        grid=(m // tm,),
        in_specs=[pl.BlockSpec((tm, d), lambda i: (i, 0)), pl.BlockSpec((1, d), lambda i: (0, 0))],
        out_specs=pl.BlockSpec((tm, d), lambda i: (i, 0)),
        out_shape=jax.ShapeDtypeStruct((m, d), out_dtype),
        compiler_params=_cparams("parallel"),
        name="rmsnorm",
    )(x, w.reshape(1, d).astype(F32))


def _mm_kernel(*refs, n_pairs, has_res):
    n_in = 2 * n_pairs + int(has_res)
    o_ref = refs[n_in]
    wb_refs = refs[n_in + 1:]

    @pl.when(pl.program_id(1) == 0)
    def _():
        for p in range(n_pairs):
            wb_refs[p][...] = refs[2 * p + 1][...].astype(BF16)

    acc = None
    for p in range(n_pairs):
        d = _dot(refs[2 * p][...], wb_refs[p][...])
        acc = d if acc is None else acc + d
    if has_res:
        acc = acc + refs[2 * n_pairs][...]
    o_ref[...] = acc.astype(o_ref.dtype)


def _pick_tile(n, prefs):
    for t in prefs:
        if n % t == 0:
            return t
    return n


def _matmul(pairs, n=None, res=None, out_dtype=F32, tm=1024, tn=None):
    m = pairs[0][0].shape[0]
    n = n or pairs[0][1].shape[2]
    tm = min(tm, m)
    tn = tn or _pick_tile(n, (1024, 1152, 512, 384, 256, 128))
    in_specs, args, scratch = [], [], []
    for a, w, l, r in pairs:
        k = a.shape[1]
        in_specs += [pl.BlockSpec((tm, k), lambda j, i: (i, 0)),
                     pl.BlockSpec((None, k, tn), lambda j, i, l=l, r=r: (l, r, j))]
        args += [a, w]
        scratch.append(pltpu.VMEM((k, tn), BF16))
    if res is not None:
        in_specs.append(pl.BlockSpec((tm, tn), lambda j, i: (i, j)))
        args.append(res)
    return pl.pallas_call(
        functools.partial(_mm_kernel, n_pairs=len(pairs), has_res=res is not None),
        grid=(n // tn, m // tm),
        in_specs=in_specs,
        out_specs=pl.BlockSpec((tm, tn), lambda j, i: (i, j)),
        out_shape=jax.ShapeDtypeStruct((m, n), out_dtype),
        scratch_shapes=scratch,
        compiler_params=_cparams("parallel", "arbitrary"),
        name="matmul",
    )(*args)


def _mm_rows_kernel(a_ref, w_ref, res_ref, o_ref):
    o_ref[...] = (_dot(a_ref[...], w_ref[...]) + res_ref[...]).astype(o_ref.dtype)


def _matmul_rows(a, w, res, tm=1024, tn=512):
    m, k = a.shape
    n = w.shape[1]
    tm = min(tm, m)
    return pl.pallas_call(
        _mm_rows_kernel,
        grid=(m // tm, n // tn),
        in_specs=[pl.BlockSpec((tm, k), lambda i, j: (i, 0)),
                  pl.BlockSpec((k, tn), lambda i, j: (0, j)),
                  pl.BlockSpec((tm, tn), lambda i, j: (i, j))],
        out_specs=pl.BlockSpec((tm, tn), lambda i, j: (i, j)),
        out_shape=jax.ShapeDtypeStruct((m, n), res.dtype),
        compiler_params=_cparams("parallel", "parallel"),
        name="matmul_rows",
    )(a, w, res)


def _ffn_up_kernel(h_ref, wg_ref, wv_ref, cg_ref, cv_ref, o_ref, carry_ref, wgb_ref, wvb_ref, *, blocks_per_seq):
    i = pl.program_id(1)

    @pl.when(i == 0)
    def _():
        wgb_ref[...] = wg_ref[...].astype(BF16)
        wvb_ref[...] = wv_ref[...].astype(BF16)

    @pl.when(i % blocks_per_seq == 0)
    def _():
        carry_ref[...] = jnp.zeros_like(carry_ref)

    h = h_ref[...]
    zg = _dot(h, wgb_ref[...])
    zv = _dot(h, wvb_ref[...])
    tm = zg.shape[0]
    row = lax.broadcasted_iota(jnp.int32, zg.shape, 0)

    def conv(z, c_ref, prev):
        m1 = jnp.where(row == 0, prev[7:8, :], pltpu.roll(z, 1, 0))
        m2 = jnp.where(row == 0, prev[6:7, :], jnp.where(row == 1, prev[7:8, :], pltpu.roll(z, 2, 0)))
        c = c_ref[...]
        return c[0:1, :] * m2 + c[1:2, :] * m1 + c[2:3, :] * z

    g = conv(zg, cg_ref, carry_ref[0])
    v = conv(zv, cv_ref, carry_ref[1])
    carry_ref[0] = zg[tm - SUBLANES:, :]
    carry_ref[1] = zv[tm - SUBLANES:, :]
    o_ref[...] = (g * jax.nn.sigmoid(g) * v).astype(o_ref.dtype)


def _ffn_up(hn, w_up, conv_w, layer, seq):
    m, d = hn.shape
    f = w_up.shape[2] // 2
    tm = min(1024, seq)
    tn = _pick_tile(f, (512, 256, 128))
    nj = f // tn
    return pl.pallas_call(
        functools.partial(_ffn_up_kernel, blocks_per_seq=seq // tm),
        grid=(nj, m // tm),
        in_specs=[
            pl.BlockSpec((tm, d), lambda j, i: (i, 0)),
            pl.BlockSpec((None, d, tn), lambda j, i: (layer, 0, j)),
            pl.BlockSpec((None, d, tn), lambda j, i: (layer, 0, j + nj)),
            pl.BlockSpec((None, 3, tn), lambda j, i: (layer, 0, j)),
            pl.BlockSpec((None, 3, tn), lambda j, i: (layer, 0, j + nj)),
        ],
        out_specs=pl.BlockSpec((tm, tn), lambda j, i: (i, j)),
        out_shape=jax.ShapeDtypeStruct((m, f), BF16),
        scratch_shapes=[pltpu.VMEM((2, SUBLANES, tn), F32), pltpu.VMEM((d, tn), BF16), pltpu.VMEM((d, tn), BF16)],
        compiler_params=_cparams("parallel", "arbitrary"),
        name="ffn_up_conv",
    )(hn, w_up, w_up, conv_w, conv_w)


def _gdn_prep_kernel(zq_ref, zk_ref, zv_ref, cq_ref, ck_ref, cv_ref, oq_ref, ok_ref, ov_ref):
    row = lax.broadcasted_iota(jnp.int32, zq_ref.shape[1:], 0)

    def conv_silu(z_ref, c_ref):
        z = z_ref[0]
        c = c_ref[...]
        acc = c[GDN_CONV - 1:GDN_CONV, :] * z
        for s in range(1, GDN_CONV):
            acc = acc + c[GDN_CONV - 1 - s:GDN_CONV - s, :] * jnp.where(row >= s, pltpu.roll(z, s, 0), 0.0)
        return acc * jax.nn.sigmoid(acc)

    def l2n(t):
        return t * lax.rsqrt(jnp.sum(t * t, axis=-1, keepdims=True) + 1e-6)

    oq_ref[0] = l2n(conv_silu(zq_ref, cq_ref)) * (GDN_D ** -0.5)
    ok_ref[0] = l2n(conv_silu(zk_ref, ck_ref))
    ov_ref[0] = conv_silu(zv_ref, cv_ref)


def _gdn_prep(z3, conv_w):
    b, t, _ = z3.shape
    h = GDN_HEADS
    zspec = lambda off: pl.BlockSpec((1, t, GDN_D), lambda bi, hi: (bi, 0, hi + off))
    cspec = lambda off: pl.BlockSpec((GDN_CONV, GDN_D), lambda bi, hi: (0, hi + off))
    ospec = pl.BlockSpec((1, t, GDN_D), lambda bi, hi: (bi, 0, hi))
    oshape = jax.ShapeDtypeStruct((b, t, HALF), F32)
    return pl.pallas_call(
        _gdn_prep_kernel,
        grid=(b, h),
        in_specs=[zspec(0), zspec(h), zspec(2 * h), cspec(0), cspec(h), cspec(2 * h)],
        out_specs=[ospec, ospec, ospec],
        out_shape=[oshape, oshape, oshape],
        compiler_params=_cparams("parallel", "parallel"),
        name="gdn_prep",
    )(z3, z3, z3, conv_w, conv_w, conv_w)


def _gdn_kernel(q_ref, k_ref, v_ref, gate_ref, zs_ref, alog_ref, dtb_ref, nw_ref, o_ref, s_ref, *, prec):
    c = GDN_CHUNK

    @pl.when(pl.program_id(1) == 0)
    def _():
        s_ref[...] = jnp.zeros_like(s_ref)

    zs = zs_ref[0]
    beta_all = jax.nn.sigmoid(zs)
    g_all = -jnp.exp(alog_ref[...]) * jax.nn.softplus(zs + dtb_ref[...])
    row, col = _iota2((c, c))
    lower = jnp.where(row >= col, 1.0, 0.0).astype(F32)
    heads = range(GDN_HEADS)
    per_head = lambda ref: jnp.stack([ref[0, :, h * GDN_D:(h + 1) * GDN_D] for h in heads])
    q, k, v = per_head(q_ref), per_head(k_ref), per_head(v_ref)
    beta = jnp.stack([beta_all[:, h:h + 1] for h in heads])
    g = jnp.stack([g_all[:, GDN_HEADS + h:GDN_HEADS + h + 1] for h in heads])
    lower3 = jnp.broadcast_to(lower, (GDN_HEADS, c, c))
    diff = _dot_01_left(lower3, jnp.where(row > col, jnp.broadcast_to(g, (GDN_HEADS, c, c)), 0.0))
    decay = jnp.where(row >= col, jnp.exp(diff), 0.0)
    gc = _dot_01_left(lower3, jnp.broadcast_to(g, (GDN_HEADS, c, GDN_D)))
    gl = gc[:, c - 1:c, :]
    egc = jnp.exp(gc)
    kb = k * beta
    kk = _dot_nt(jnp.concatenate([kb, q], axis=1), k, prec)
    a = jnp.where(row > col, kk[:, :c, :] * decay, 0.0)
    attn = kk[:, c:, :] * decay
    tinv = _unit_lower_inverse(a, row, col, c, prec)
    sol = _dot(tinv, jnp.concatenate([v * beta, kb * egc], axis=2), prec)
    u, w = sol[:, :, :GDN_D], sol[:, :, GDN_D:]
    s = s_ref[...]
    v_new = u - _dot(w, s, prec)
    o = _dot(jnp.concatenate([q * egc, attn], axis=2), jnp.concatenate([s, v_new], axis=1), prec)
    s_ref[...] = s * jnp.exp(gl) + _dot_tn(k * jnp.exp(gl - gc), v_new, prec)
    o = o * lax.rsqrt(jnp.mean(o * o, axis=-1, keepdims=True) + EPS) * nw_ref[...]
    for h in heads:
        gate = gate_ref[0, :, h * GDN_D:(h + 1) * GDN_D]
        o_ref[0, :, h * GDN_D:(h + 1) * GDN_D] = (o[h] * gate * jax.nn.sigmoid(gate)).astype(o_ref.dtype)


def _gdn(q, k, v, z3, zs3, zs_blk, a_log, dt_bias, norm_w, prec):
    b, t, _ = q.shape
    c = GDN_CHUNK
    pad = lambda p: jnp.zeros((1, LANES), F32).at[0, GDN_HEADS:2 * GDN_HEADS].set(p.astype(F32))
    blk = pl.BlockSpec((1, c, HALF), lambda bi, ni: (bi, ni, 0))
    vec = pl.BlockSpec((1, LANES), lambda bi, ni: (0, 0))
    return pl.pallas_call(
        functools.partial(_gdn_kernel, prec=prec),
        grid=(b, t // c),
        in_specs=[blk, blk, blk,
                  pl.BlockSpec((1, c, HALF), lambda bi, ni: (bi, ni, 3)),
                  pl.BlockSpec((1, c, LANES), lambda bi, ni: (bi, ni, zs_blk)),
                  vec, vec, vec],
        out_specs=blk,
        out_shape=jax.ShapeDtypeStruct((b, t, HALF), BF16),
        scratch_shapes=[pltpu.VMEM((GDN_HEADS, GDN_D, GDN_D), F32)],
        compiler_params=_cparams("parallel", "arbitrary"),
        name="gdn_chunk",
    )(q, k, v, z3, zs3, pad(a_log), pad(dt_bias), norm_w.reshape(1, GDN_D).astype(F32))


def _s5_param_kernel(lr_ref, li_ref, dt_ref, br_ref, bi_ref, ar_ref, ai_ref, bbr_ref, bbi_ref):
    lr, li, dt = lr_ref[...], li_ref[...], dt_ref[...]
    step = jnp.exp(dt)
    mag = jnp.exp(lr * step)
    ang = li * step
    ab_re, ab_im = mag * jnp.cos(ang), mag * jnp.sin(ang)
    den = lr * lr + li * li
    nr = ab_re - 1.0
    f_re = (nr * lr + ab_im * li) / den
    f_im = (ab_im * lr - nr * li) / den
    br, bi = br_ref[...], bi_ref[...]
    bbr_ref[...] = f_re * br - f_im * bi
    bbi_ref[...] = f_re * bi + f_im * br
    n = ab_re.shape[1]
    cmul = lambda xr, xi, yr, yi: (xr * yr - xi * yi, xr * yi + xi * yr)
    row = lax.broadcasted_iota(jnp.int32, (SUBLANES, n), 0)
    cur = (ab_re, ab_im)
    pr = jnp.broadcast_to(ab_re, (SUBLANES, n))
    pi = jnp.broadcast_to(ab_im, (SUBLANES, n))
    for r in range(1, SUBLANES):
        cur = cmul(cur[0], cur[1], ab_re, ab_im)
        pr = jnp.where(row == r, cur[0], pr)
        pi = jnp.where(row == r, cur[1], pi)
    rows = SUBLANES
    while rows < ar_ref.shape[0]:
        tr, ti = cmul(pr, pi, pr[rows - 1:rows, :], pi[rows - 1:rows, :])
        pr = jnp.concatenate([pr, tr], axis=0)
        pi = jnp.concatenate([pi, ti], axis=0)
        rows *= 2
    ar_ref[...] = pr
    ai_ref[...] = pi


def _s5_params(lam_re, lam_im, log_step, b_re, b_im, n_pow):
    assert n_pow >= SUBLANES and n_pow & (n_pow - 1) == 0
    gp = S5_GROUPS * S5_STATE
    row = lambda x: x.astype(F32).reshape(1, gp)
    bt = lambda x: jnp.transpose(x.astype(F32), (2, 0, 1)).reshape(S5_GROUP, gp)
    pshape = jax.ShapeDtypeStruct((n_pow, gp), F32)
    mshape = jax.ShapeDtypeStruct((S5_GROUP, gp), F32)
    return pl.pallas_call(
        _s5_param_kernel,
        out_shape=[pshape, pshape, mshape, mshape],
        name="s5_params",
    )(row(lam_re), row(lam_im), row(jnp.repeat(log_step[:, None], S5_STATE, axis=1)), bt(b_re), bt(b_im))


def _s5_scan_kernel(u_ref, bdr_ref, bdi_ref, cdr_ref, cdi_ref, pr_ref, pi_ref, d_ref, o_ref,
                    up_ref, xr_ref, xi_ref, y_ref):
    t = u_ref.shape[1]
    n = S5_TILE_STATES
    nk = t // SUBLANES
    rt = min(512, t)

    def permute_in(k, _):
        r0 = pl.multiple_of(k * SUBLANES, SUBLANES)
        up_ref[pl.ds(r0, SUBLANES), :] = u_ref[0, pl.ds(k, SUBLANES, stride=nk), :]
        return 0

    lax.fori_loop(0, nk, permute_in, 0, unroll=SUBLANES)

    for r0 in range(0, t, rt):
        ub = up_ref[r0:r0 + rt, :].astype(BF16)
        xr_ref[r0:r0 + rt, :] = _dot(ub, bdr_ref[0])
        xi_ref[r0:r0 + rt, :] = _dot(ub, bdi_ref[0])

    ar = jnp.broadcast_to(pr_ref[0:1, :], (SUBLANES, n))
    ai = jnp.broadcast_to(pi_ref[0:1, :], (SUBLANES, n))

    def local_scan(k, carry):
        cr, ci = carry
        r0 = pl.multiple_of(k * SUBLANES, SUBLANES)
        xr = xr_ref[pl.ds(r0, SUBLANES), :] + (ar * cr - ai * ci)
        xi = xi_ref[pl.ds(r0, SUBLANES), :] + (ar * ci + ai * cr)
        xr_ref[pl.ds(r0, SUBLANES), :] = xr
        xi_ref[pl.ds(r0, SUBLANES), :] = xi
        return xr, xi

    zero = jnp.zeros((SUBLANES, n), F32)
    fr, fi = lax.fori_loop(0, nk, local_scan, (zero, zero), unroll=SUBLANES)

    row = lax.broadcasted_iota(jnp.int32, (SUBLANES, n), 0)
    cmul = lambda xr, xi, yr, yi: (xr * yr - xi * yi, xr * yi + xi * yr)
    gr = jnp.broadcast_to(pr_ref[nk - 1:nk, :], (SUBLANES, n))
    gi = jnp.broadcast_to(pi_ref[nk - 1:nk, :], (SUBLANES, n))
    for d in (1, 2, 4):
        sr = jnp.where(row >= d, pltpu.roll(fr, d, 0), 0.0)
        si = jnp.where(row >= d, pltpu.roll(fi, d, 0), 0.0)
        tr, ti = cmul(gr, gi, sr, si)
        fr, fi = fr + tr, fi + ti
        gr, gi = cmul(gr, gi, gr, gi)
    cr = jnp.where(row >= 1, pltpu.roll(fr, 1, 0), 0.0)
    ci = jnp.where(row >= 1, pltpu.roll(fi, 1, 0), 0.0)

    def add_carry(k8, _):
        p0 = pl.multiple_of(k8 * SUBLANES, SUBLANES)
        pr8 = pr_ref[pl.ds(p0, SUBLANES), :]
        pi8 = pi_ref[pl.ds(p0, SUBLANES), :]
        for j in range(SUBLANES):
            r0 = pl.multiple_of((k8 * SUBLANES + j) * SUBLANES, SUBLANES)
            pr = jnp.broadcast_to(pr8[j:j + 1, :], (SUBLANES, n))
            pi = jnp.broadcast_to(pi8[j:j + 1, :], (SUBLANES, n))
            xr_ref[pl.ds(r0, SUBLANES), :] = xr_ref[pl.ds(r0, SUBLANES), :] + (pr * cr - pi * ci)
            xi_ref[pl.ds(r0, SUBLANES), :] = xi_ref[pl.ds(r0, SUBLANES), :] + (pr * ci + pi * cr)
        return 0

    lax.fori_loop(0, nk // SUBLANES, add_carry, 0)

    for r0 in range(0, t, rt):
        y = (_dot(xr_ref[r0:r0 + rt, :].astype(BF16), cdr_ref[0])
             - _dot(xi_ref[r0:r0 + rt, :].astype(BF16), cdi_ref[0]) + up_ref[r0:r0 + rt, :] * d_ref[...])
        y_ref[r0:r0 + rt, :] = jax.nn.gelu(y)

    def permute_out(k, _):
        r0 = pl.multiple_of(k * SUBLANES, SUBLANES)
        o_ref[0, pl.ds(k, SUBLANES, stride=nk), :] = y_ref[pl.ds(r0, SUBLANES), :]
        return 0

    lax.fori_loop(0, nk, permute_out, 0, unroll=SUBLANES)


def _s5_scan(z3, u_off, bd_re, bd_im, cd_re, cd_im, p_re, p_im, d_skip):
    b, t, _ = z3.shape
    nt = HALF // LANES
    n = S5_TILE_STATES
    nk = t // SUBLANES
    assert p_re.shape[0] == nk
    return pl.pallas_call(
        _s5_scan_kernel,
        grid=(b, nt),
        in_specs=[
            pl.BlockSpec((1, t, LANES), lambda bi, j: (bi, 0, j + u_off)),
            pl.BlockSpec((1, LANES, n), lambda bi, j: (j, 0, 0)),
            pl.BlockSpec((1, LANES, n), lambda bi, j: (j, 0, 0)),
            pl.BlockSpec((1, n, LANES), lambda bi, j: (j, 0, 0)),
            pl.BlockSpec((1, n, LANES), lambda bi, j: (j, 0, 0)),
            pl.BlockSpec((nk, n), lambda bi, j: (0, j)),
            pl.BlockSpec((nk, n), lambda bi, j: (0, j)),
            pl.BlockSpec((1, LANES), lambda bi, j: (0, j)),
        ],
        out_specs=pl.BlockSpec((1, t, LANES), lambda bi, j: (bi, 0, j)),
        out_shape=jax.ShapeDtypeStruct((b, t, HALF), F32),
        scratch_shapes=[pltpu.VMEM((t, LANES), F32), pltpu.VMEM((t, n), F32), pltpu.VMEM((t, n), F32),
                        pltpu.VMEM((t, LANES), F32)],
        compiler_params=_cparams("parallel", "parallel"),
        name="s5_scan",
    )(z3, bd_re, bd_im, cd_re, cd_im, p_re, p_im, d_skip.reshape(1, HALF).astype(F32))


def _glu_kernel(y_ref, w_ref, yt_ref, o_ref):
    gate = _dot(y_ref[...].astype(BF16), w_ref[...])
    o_ref[...] = (yt_ref[...] * jax.nn.sigmoid(gate)).astype(o_ref.dtype)


def _glu(y, w):
    m, k = y.shape
    tm = min(1024, m)
    tn = 512
    return pl.pallas_call(
        _glu_kernel,
        grid=(m // tm, k // tn),
        in_specs=[pl.BlockSpec((tm, k), lambda i, j: (i, 0)),
                  pl.BlockSpec((k, tn), lambda i, j: (0, j)),
                  pl.BlockSpec((tm, tn), lambda i, j: (i, j))],
        out_specs=pl.BlockSpec((tm, tn), lambda i, j: (i, j)),
        out_shape=jax.ShapeDtypeStruct((m, k), BF16),
        compiler_params=_cparams("parallel", "parallel"),
        name="s5_glu",
    )(y, w, y)


def _s5_block_diag(bb_re, bb_im, c_re, c_im):
    nt, tg = HALF // LANES, S5_TILE_GROUPS
    eye = jnp.eye(tg, dtype=F32)

    def bmap(bb):
        x = bb.reshape(S5_GROUP, nt, tg, S5_STATE)
        x = jnp.einsum('cjgp,gh->jgchp', x, eye)
        return x.reshape(nt, LANES, S5_TILE_STATES).astype(BF16)

    def cmap(cc):
        x = cc.astype(F32).reshape(nt, tg, S5_GROUP, S5_STATE)
        x = jnp.einsum('jgcp,gh->jgphc', x, eye)
        return x.reshape(nt, S5_TILE_STATES, LANES).astype(BF16)

    return bmap(bb_re), bmap(bb_im), cmap(c_re), cmap(c_im)


def _seg_sum(x, seg):
    row, col = _iota2((LANES, LANES))
    shift = seg.bit_length() - 1
    ones = jnp.where((row >> shift) == (col >> shift), 1.0, 0.0).astype(F32)
    return _dot_01_right(x, ones)


def _rwkv_prep_kernel(zr_ref, zk_ref, zv_ref, zl_ref, hr_ref, hk_ref, hv_ref, hl_ref,
                      mur_ref, muk_ref, muv_ref, mul_ref, w2_ref, a2_ref, g2_ref,
                      w0_ref, a0_ref, kk_ref, ka_ref,
                      or_ref, ow_ref, ok_ref, ov_ref, okk_ref, ob_ref, og_ref):
    first = pl.program_id(1) == 0

    def shift_mix(z_ref, halo_ref, mu_ref):
        z = z_ref[0]
        row = lax.broadcasted_iota(jnp.int32, z.shape, 0)
        prev = jnp.where(first, 0.0, halo_ref[0, SUBLANES - 1:SUBLANES, :])
        zm1 = jnp.where(row == 0, prev, pltpu.roll(z, 1, 0))
        return z + (zm1 - z) * mu_ref[...]

    r = shift_mix(zr_ref, hr_ref, mur_ref)
    k = shift_mix(zk_ref, hk_ref, muk_ref)
    v = shift_mix(zv_ref, hv_ref, muv_ref)
    zl = shift_mix(zl_ref, hl_ref, mul_ref)
    wa = zl[:, :LANES]
    w = w0_ref[...] + _dot(jnp.tanh(wa).astype(BF16), w2_ref[...])
    w = -jax.nn.softplus(-w) - 0.5
    a = jax.nn.sigmoid(a0_ref[...] + _dot(wa.astype(BF16), a2_ref[...]))
    g = _dot(jax.nn.sigmoid(zl[:, LANES:]).astype(BF16), g2_ref[...])
    kk = k * kk_ref[...]
    kk = kk * lax.rsqrt(_seg_sum(kk * kk, RWKV_HEAD) + 1e-6)
    or_ref[0] = r
    ow_ref[0] = -jnp.exp(w)
    ok_ref[0] = k * (1.0 + (a - 1.0) * ka_ref[...])
    ov_ref[0] = v
    okk_ref[0] = kk
    ob_ref[0] = kk * a
    og_ref[0] = g


def _rwkv_prep(z3, zl3, lora_blk, mu_main, mu_lora, w2p, a2p, g2p, w0, a0, k_k, k_a):
    b, t, _ = z3.shape
    tt = min(512, t)
    nh = tt // SUBLANES
    npair = RWKV_PAIRS
    lw = RWKV_LORA_PAD
    main = lambda off: pl.BlockSpec((1, tt, LANES), lambda bi, ti, j: (bi, ti, j + off))
    halo = lambda off: pl.BlockSpec((1, SUBLANES, LANES),
                                    lambda bi, ti, j: (bi, jnp.maximum(ti * nh - 1, 0), j + off))
    vec = lambda off: pl.BlockSpec((1, LANES), lambda bi, ti, j: (0, j + off))
    ospec = pl.BlockSpec((1, tt, LANES), lambda bi, ti, j: (bi, ti, j))
    oshape = jax.ShapeDtypeStruct((b, t, HALF), F32)
    row = lambda x: x.astype(F32).reshape(1, -1)
    return pl.pallas_call(
        _rwkv_prep_kernel,
        grid=(b, t // tt, npair),
        in_specs=[
            main(0), main(npair), main(2 * npair),
            pl.BlockSpec((1, tt, lw), lambda bi, ti, j: (bi, ti, lora_blk)),
            halo(0), halo(npair), halo(2 * npair),
            pl.BlockSpec((1, SUBLANES, lw), lambda bi, ti, j: (bi, jnp.maximum(ti * nh - 1, 0), lora_blk)),
            vec(0), vec(npair), vec(2 * npair),
            pl.BlockSpec((1, lw), lambda bi, ti, j: (0, 0)),
            pl.BlockSpec((LANES, LANES), lambda bi, ti, j: (0, j)),
            pl.BlockSpec((LANES, LANES), lambda bi, ti, j: (0, j)),
            pl.BlockSpec((lw - LANES, LANES), lambda bi, ti, j: (0, j)),
            vec(0), vec(0), vec(0), vec(0),
        ],
        out_specs=[ospec] * 7,
        out_shape=[oshape] * 7,
        compiler_params=_cparams("parallel", "parallel", "parallel"),
        name="rwkv_prep",
    )(z3, z3, z3, zl3, z3, z3, z3, zl3, mu_main, mu_main, mu_main, mu_lora,
      w2p, a2p, g2p, row(w0), row(a0), row(k_k), row(k_a))


def _rwkv_kernel(r_ref, w_ref, k_ref, v_ref, kk_ref, b_ref, g_ref, rk_ref, lnw_ref, lnb_ref, o_ref, h_ref, *, prec):
    c = RWKV_CHUNK
    c2 = 2 * c

    @pl.when(pl.program_id(1) == 0)
    def _():
        h_ref[...] = jnp.zeros_like(h_ref)

    row1, col1 = _iota2((c, c))
    lower = jnp.where(row1 >= col1, 1.0, 0.0).astype(F32)
    row, col = _iota2((c2, c2))
    same = (row >> 6) == (col >> 6)
    strict = jnp.logical_and(same, row > col)
    incl = jnp.logical_and(same, row >= col)
    lane = lax.broadcasted_iota(jnp.int32, (c, LANES), 1)
    first_head = lane < RWKV_HEAD

    def stack2(x):
        return jnp.concatenate([jnp.where(first_head, x, 0.0), jnp.where(first_head, 0.0, x)], axis=1)

    pairs = range(RWKV_PAIRS)
    npair = RWKV_PAIRS
    per_pair = lambda x: jnp.stack([x[:, p * LANES:(p + 1) * LANES] for p in pairs])
    seg_sum = lambda x: _seg_sum(x.reshape(npair * c, LANES), RWKV_HEAD).reshape(npair, c, LANES)
    r, lw, k, v = per_pair(r_ref[0]), per_pair(w_ref[0]), per_pair(k_ref[0]), per_pair(v_ref[0])
    kk, b = per_pair(kk_ref[0]), per_pair(b_ref[0])
    cl = per_pair(_dot_01_left(lower, w_ref[0]))
    cl_last = cl[:, c - 1:c, :]
    e_neg = jnp.exp(-cl)
    e_tail = jnp.exp(cl_last - cl)
    kk2 = stack2(kk * jnp.exp(cl - lw))
    r2 = stack2(r * jnp.exp(cl))
    b2 = stack2(b * e_neg)
    k2 = stack2(k * e_neg)
    v2 = stack2(v)
    bd2 = stack2(b * e_tail)
    kd2 = stack2(k * e_tail)
    sc = _dot_nt(jnp.concatenate([kk2, r2], axis=1), jnp.concatenate([b2, k2], axis=1), prec)
    a_ab = jnp.where(strict, sc[:, :c2, :c2], 0.0)
    a_ak = jnp.where(strict, sc[:, :c2, c2:], 0.0)
    r_bk = jnp.concatenate([jnp.where(incl, sc[:, c2:, :c2], 0.0), jnp.where(incl, sc[:, c2:, c2:], 0.0)], axis=2)
    tinv = _unit_lower_inverse(a_ab, row, col, c, prec)
    wt = _dot(tinv, jnp.concatenate([kk2, _dot(a_ak, v2, prec)], axis=2), prec)
    wk, tv = wt[:, :, :LANES], wt[:, :, LANES:]
    ht = h_ref[...]
    hp = _dot_nt(jnp.concatenate([wk, r2], axis=1), ht, prec)
    u = -hp[:, :c2, :] - tv
    uv = jnp.concatenate([u, v2], axis=1)
    y2 = hp[:, c2:, :] + _dot(r_bk, uv, prec)
    h_ref[...] = ht * jnp.exp(cl_last) + _dot_tn(uv, jnp.concatenate([bd2, kd2], axis=1), prec)
    y = y2[:, :c, :] + y2[:, c:, :]
    mu = seg_sum(y) * (1.0 / RWKV_HEAD)
    d = y - mu
    var = seg_sum(d * d) * (1.0 / RWKV_HEAD)
    yn = d * lax.rsqrt(var + RWKV_GN_EPS) * per_pair(lnw_ref[...]) + per_pair(lnb_ref[...])
    out = (yn + seg_sum(r * k * per_pair(rk_ref[...])) * v) * per_pair(g_ref[0])
    for p in pairs:
        o_ref[0, :, p * LANES:(p + 1) * LANES] = out[p].astype(o_ref.dtype)


def _rwkv(r, w, k, v, kk, b, g, r_k, ln_w, ln_b, prec):
    bsz, t, _ = r.shape
    c = RWKV_CHUNK
    blk = pl.BlockSpec((1, c, HALF), lambda bi, ni: (bi, ni, 0))
    vec = pl.BlockSpec((1, HALF), lambda bi, ni: (0, 0))
    row = lambda x: x.astype(F32).reshape(1, HALF)
    return pl.pallas_call(
        functools.partial(_rwkv_kernel, prec=prec),
        grid=(bsz, t // c),
        in_specs=[blk] * 7 + [vec] * 3,
        out_specs=blk,
        out_shape=jax.ShapeDtypeStruct((bsz, t, HALF), BF16),
        scratch_shapes=[pltpu.VMEM((RWKV_PAIRS, LANES, LANES), F32)],
        compiler_params=_cparams("parallel", "arbitrary"),
        name="rwkv_chunk",
    )(r, w, k, v, kk, b, g, row(r_k), row(ln_w), row(ln_b))


def _rope_kernel(f_ref, cos_ref, sin_ref):
    t = cos_ref.shape[0]
    pos = lax.broadcasted_iota(jnp.int32, (t, LANES), 0).astype(F32)
    lane = lax.broadcasted_iota(jnp.int32, (t, LANES), 1)
    ang = pos * f_ref[...]
    cos_ref[...] = jnp.cos(ang)
    sin_ref[...] = jnp.where(lane < RET_DK // 2, -1.0, 1.0) * jnp.sin(ang)


def _rope_tables(t):
    inv_freq = ROPE_BASE ** (-jnp.linspace(0.0, 1.0, RET_DK // 2, dtype=F32))
    f2 = jnp.concatenate([inv_freq, inv_freq]).reshape(1, RET_DK)
    shape = jax.ShapeDtypeStruct((t, RET_DK), F32)
    return pl.pallas_call(_rope_kernel, out_shape=[shape, shape], name="rope_tables")(f2)


def _ret_kernel(q_ref, k_ref, v_ref, gate_ref, cos_ref, sin_ref, o_ref, s_ref):
    c = RET_CHUNK

    @pl.when(pl.program_id(1) == 0)
    def _():
        s_ref[...] = jnp.zeros_like(s_ref)

    row, col = _iota2((c, c))
    dist = (row - col).astype(F32)
    idx = lax.broadcasted_iota(jnp.int32, (c, 1), 0).astype(F32)
    cos, sin = cos_ref[...], sin_ref[...]
    rot = lambda x: x * cos + pltpu.roll(x, RET_DK // 2, 1) * sin

    for h in range(RET_HEADS):
        log_g = math.log(1.0 - 2.0 ** (-5.0 - h))
        q = rot(q_ref[0, :, h * RET_DK:(h + 1) * RET_DK])
        k = rot(k_ref[0, :, h * RET_DK:(h + 1) * RET_DK]) * (RET_DK ** -0.5)
        v = v_ref[0, :, h * RET_DV:(h + 1) * RET_DV]
        vb = v.astype(BF16)
        dmask = jnp.where(row >= col, jnp.exp(log_g * dist), 0.0)
        sc = _dot_nt(q.astype(BF16), k.astype(BF16)) * dmask
        s = s_ref[h]
        o = _dot(sc.astype(BF16), vb) + _dot((q * jnp.exp(log_g * (idx + 1.0))).astype(BF16), s.astype(BF16))
        kd = k * jnp.exp(log_g * (c - 1.0 - idx))
        s_ref[h] = s * math.exp(log_g * c) + _dot_tn(kd.astype(BF16), vb)
        o = o * lax.rsqrt(jnp.mean(o * o, axis=-1, keepdims=True) + EPS)
        gate = gate_ref[0, :, h * RET_DV:(h + 1) * RET_DV]
        o_ref[0, :, h * RET_DV:(h + 1) * RET_DV] = (o * gate * jax.nn.sigmoid(gate)).astype(o_ref.dtype)


def _retention(z3, q_blk, cos, sin):
    b, t, _ = z3.shape
    c = RET_CHUNK
    qk = RET_HEADS * RET_DK
    v_blk = (q_blk * qk + 2 * qk) // HALF
    return pl.pallas_call(
        _ret_kernel,
        grid=(b, t // c),
        in_specs=[
            pl.BlockSpec((1, c, qk), lambda bi, ni: (bi, ni, q_blk)),
            pl.BlockSpec((1, c, qk), lambda bi, ni: (bi, ni, q_blk + 1)),
            pl.BlockSpec((1, c, HALF), lambda bi, ni: (bi, ni, v_blk)),
            pl.BlockSpec((1, c, HALF), lambda bi, ni: (bi, ni, v_blk + 1)),
            pl.BlockSpec((c, RET_DK), lambda bi, ni: (ni, 0)),
            pl.BlockSpec((c, RET_DK), lambda bi, ni: (ni, 0)),
        ],
        out_specs=pl.BlockSpec((1, c, HALF), lambda bi, ni: (bi, ni, 0)),
        out_shape=jax.ShapeDtypeStruct((b, t, HALF), BF16),
        scratch_shapes=[pltpu.VMEM((RET_HEADS, RET_DK, RET_DV), F32)],
        compiler_params=_cparams("parallel", "arbitrary"),
        name="retention_chunk",
    )(z3, z3, z3, z3, cos, sin)


def _even_mixer(hn, bsz, seq, idx, w_in, w_out, conv_w, a_log, dt_bias, norm_w, lam_re, lam_im, b_re, b_im,
                c_re, c_im, d_skip, log_step, w_glu, res, prec):
    m = bsz * seq
    n_main = 4 * HALF
    n_small = 2 * GDN_HEADS
    w_b = jnp.concatenate([w_in[idx, :, n_main + n_small:],
                           jnp.pad(w_in[idx, :, n_main:n_main + n_small], ((0, 0), (0, LANES - n_small)))], axis=1)
    za = _matmul([(hn, w_in, idx, 0)], n=n_main).reshape(bsz, seq, n_main)
    zb = _matmul([(hn, w_b[None], 0, 0)]).reshape(bsz, seq, HALF + LANES)
    q, k, v = _gdn_prep(za, conv_w.astype(F32))
    ya = _gdn(q, k, v, za, zb, HALF // LANES, a_log, dt_bias, norm_w, prec)
    p_re, p_im, bb_re, bb_im = _s5_params(lam_re, lam_im, log_step, b_re, b_im, seq // SUBLANES)
    bd_re, bd_im, cd_re, cd_im = _s5_block_diag(bb_re, bb_im, c_re, c_im)
    yg = _s5_scan(zb, 0, bd_re, bd_im, cd_re, cd_im, p_re, p_im, d_skip)
    yb = _glu(yg.reshape(m, HALF), w_glu.astype(BF16))
    return _matmul([(ya.reshape(m, HALF), w_out, idx, 0), (yb, w_out, idx, 1)], res=res)


def _odd_mixer(hn, bsz, seq, idx, w_in, w_out, shift_mu, w0, w2, a0, a2, g2, k_k, k_a, r_k, ln_w, ln_b, res, prec):
    m = bsz * seq
    n_main = 3 * HALF
    n_lora = 64 + 64 + 160
    w_b = jnp.concatenate([w_in[idx, :, n_main + n_lora:],
                           jnp.pad(w_in[idx, :, n_main:n_main + n_lora], ((0, 0), (0, RWKV_LORA_PAD - n_lora)))],
                          axis=1)
    za = _matmul([(hn, w_in, idx, 0)], n=n_main).reshape(bsz, seq, n_main)
    zb = _matmul([(hn, w_b[None], 0, 0)]).reshape(bsz, seq, n_main + RWKV_LORA_PAD)
    mu_main = shift_mu[:n_main].astype(F32).reshape(1, n_main)
    mu_lora = jnp.pad(shift_mu[n_main:], (0, RWKV_LORA_PAD - n_lora)).astype(F32).reshape(1, RWKV_LORA_PAD)
    w2p = jnp.pad(w2, ((0, LANES - 64), (0, 0))).astype(BF16)
    a2p = jnp.pad(a2, ((64, 0), (0, 0))).astype(BF16)
    g2p = jnp.pad(g2, ((0, RWKV_LORA_PAD - LANES - 160), (0, 0))).astype(BF16)
    r, lw, k, v, kk, b, g = _rwkv_prep(za, zb, n_main // RWKV_LORA_PAD, mu_main, mu_lora, w2p, a2p, g2p,
                                       w0, a0, k_k, k_a)
    yc = _rwkv(r, lw, k, v, kk, b, g, r_k, ln_w, ln_b, prec)
    cos, sin = _rope_tables(seq)
    yd = _retention(zb, 0, cos, sin)
    return _matmul([(yc.reshape(m, HALF), w_out, idx, 0), (yd.reshape(m, HALF), w_out, idx, 1)], res=res)


def kernel(x, norm_mix, norm_ffn, norm_final, ev_w_in, ev_w_out, gdn_conv_w, gdn_a_log, gdn_dt_bias, gdn_norm_w, s5_lam_re, s5_lam_im, s5_b_re, s5_b_im, s5_c_re, s5_c_im, s5_d, s5_log_step, s5_w_glu, od_w_in, od_w_out, rwkv_shift_mu, rwkv_w0, rwkv_w2, rwkv_a0, rwkv_a2, rwkv_g2, rwkv_k_k, rwkv_k_a, rwkv_r_k, rwkv_ln_w, rwkv_ln_b, ffn_w_up, ffn_conv_w, ffn_w_down):
    bsz, seq, d = x.shape
    m = bsz * seq
    depth = norm_mix.shape[0]
    prec = None
    h = x.reshape(m, d).astype(F32)
    for layer in range(depth):
        hn = _rmsnorm(h, norm_mix[layer], BF16)
        i = layer // 2
        if layer % 2 == 0:
            h = _even_mixer(hn, bsz, seq, i, ev_w_in, ev_w_out, gdn_conv_w[i], gdn_a_log[i], gdn_dt_bias[i],
                            gdn_norm_w[i], s5_lam_re[i], s5_lam_im[i], s5_b_re[i], s5_b_im[i], s5_c_re[i],
                            s5_c_im[i], s5_d[i], s5_log_step[i], s5_w_glu[i], h, prec)
        else:
            h = _odd_mixer(hn, bsz, seq, i, od_w_in, od_w_out, rwkv_shift_mu[i], rwkv_w0[i], rwkv_w2[i],
                           rwkv_a0[i], rwkv_a2[i], rwkv_g2[i], rwkv_k_k[i], rwkv_k_a[i], rwkv_r_k[i],
                           rwkv_ln_w[i], rwkv_ln_b[i], h, prec)
        hn = _rmsnorm(h, norm_ffn[layer], BF16)
        act = _ffn_up(hn, ffn_w_up, ffn_conv_w, layer, seq)
        h = _matmul_rows(act, ffn_w_down[layer].astype(BF16), h)
    return _rmsnorm(h, norm_final, x.dtype).reshape(bsz, seq, d)
```

```python
import functools
import math

import jax
import jax.numpy as jnp
from jax import lax
from jax.experimental import pallas as pl
from jax.experimental.pallas import tpu as pltpu

F32 = jnp.float32
BF16 = jnp.bfloat16
HI = lax.Precision.HIGHEST

V7X_VMEM_LIMIT_BYTES = 56 * 1024 * 1024
LANES = 128
SUBLANES = 8

EPS = 1e-6
HALF = 1024
GDN_HEADS = 8
GDN_D = 128
GDN_CONV = 4
GDN_CHUNK = 64
S5_GROUP = 16
S5_GROUPS = 64
S5_STATE = 64
S5_TILE_GROUPS = LANES // S5_GROUP
S5_TILE_STATES = S5_TILE_GROUPS * S5_STATE
RWKV_HEAD = 64
RWKV_PAIRS = HALF // LANES
RWKV_CHUNK = 64
RWKV_LORA_PAD = 384
RWKV_GN_EPS = 64e-5
RET_HEADS = 4
RET_DK = 128
RET_DV = 256
RET_CHUNK = 256
ROPE_BASE = 10000.0
INV_BLOCK = 16


def _cparams(*sem):
    return pltpu.CompilerParams(dimension_semantics=sem, vmem_limit_bytes=V7X_VMEM_LIMIT_BYTES)


def _dot_dims(a, b, dims, precision):
    if precision is None:
        a, b = a.astype(BF16), b.astype(BF16)
    batch = ((), ())
    if a.ndim == 3:
        dims = tuple(tuple(d + 1 for d in side) for side in dims)
        batch = ((0,), (0,))
    return lax.dot_general(a, b, (dims, batch), preferred_element_type=F32, precision=precision)


def _dot(a, b, precision=None):
    return _dot_dims(a, b, ((1,), (0,)), precision)


def _dot_nt(a, b, precision=None):
    return _dot_dims(a, b, ((1,), (1,)), precision)


def _dot_tn(a, b, precision=None):
    return _dot_dims(a, b, ((0,), (0,)), precision)


def _bf16_terms(x):
    terms = []
    for _ in range(3):
        t = x.astype(BF16)
        terms.append(t)
        x = x - t.astype(F32)
    return terms


def _dot_01_left(m01, x):
    mb = m01.astype(BF16)
    return sum(_dot(mb, t) for t in _bf16_terms(x))


def _dot_01_right(x, m01):
    mb = m01.astype(BF16)
    return sum(_dot(t, mb) for t in _bf16_terms(x))


def _iota2(shape):
    return lax.broadcasted_iota(jnp.int32, shape, 0), lax.broadcasted_iota(jnp.int32, shape, 1)


def _unit_lower_inverse(a, row, col, sub, precision):
    assert sub == 4 * INV_BLOCK
    shift = INV_BLOCK.bit_length() - 1
    eye = jnp.where(row == col, 1.0, 0.0).astype(F32)
    ad = jnp.where((row >> shift) == (col >> shift), a, 0.0)
    ao = a - ad
    a2 = _dot(ad, ad, precision)
    a4 = _dot(a2, a2, precision)
    a8 = _dot(a4, a4, precision)
    td = _dot(_dot(eye - ad, eye + a2, precision), _dot(eye + a4, eye + a8, precision), precision)
    n = _dot(td, ao, precision)
    n2 = _dot(n, n, precision)
    return _dot(_dot(eye - n, eye + n2, precision), td, precision)


def _rmsnorm_kernel(x_ref, w_ref, o_ref):
    x = x_ref[...]
    ms = jnp.mean(x * x, axis=-1, keepdims=True)
    o_ref[...] = (x * lax.rsqrt(ms + EPS) * w_ref[...]).astype(o_ref.dtype)


def _rmsnorm(x, w, out_dtype):
    m, d = x.shape
    tm = min(512, m)
    return pl.pallas_call(
        _rmsnorm_kernel,
        grid=(m // tm,),
        in_specs=[pl.BlockSpec((tm, d), lambda i: (i, 0)), pl.BlockSpec((1, d), lambda i: (0, 0))],
        out_specs=pl.BlockSpec((tm, d), lambda i: (i, 0)),
        out_shape=jax.ShapeDtypeStruct((m, d), out_dtype),
        compiler_params=_cparams("parallel"),
        name="rmsnorm",
    )(x, w.reshape(1, d).astype(F32))


def _mm_kernel(*refs, n_pairs, has_res, w_transposed):
    n_in = 2 * n_pairs + int(has_res)
    o_ref = refs[n_in]
    wb_refs = refs[n_in + 1:]

    @pl.when(pl.program_id(1) == 0)
    def _():
        for p in range(n_pairs):
            w = refs[2 * p + 1][...]
            wb_refs[p][...] = (w.T if w_transposed else w).astype(BF16)

    acc = None
    for p in range(n_pairs):
        d = _dot(refs[2 * p][...], wb_refs[p][...])
        acc = d if acc is None else acc + d
    if has_res:
        acc = acc + refs[2 * n_pairs][...]
    o_ref[...] = acc.astype(o_ref.dtype)


def _pick_tile(n, prefs):
    for t in prefs:
        if n % t == 0:
            return t
    return n


def _matmul(pairs, n=None, res=None, out_dtype=F32, tm=1024, tn=None, w_transposed=False):
    m = pairs[0][0].shape[0]
    n = n or pairs[0][1].shape[1 if w_transposed else 2]
    tm = min(tm, m)
    tn = tn or _pick_tile(n, (1024, 1152, 512, 384, 256, 128))
    in_specs, args, scratch = [], [], []
    for a, w, l, r in pairs:
        k = a.shape[1]
        wspec = (pl.BlockSpec((None, tn, k), lambda j, i, l=l, r=r: (l, j, r)) if w_transposed else
                 pl.BlockSpec((None, k, tn), lambda j, i, l=l, r=r: (l, r, j)))
        in_specs += [pl.BlockSpec((tm, k), lambda j, i: (i, 0)), wspec]
        args += [a, w]
        scratch.append(pltpu.VMEM((k, tn), BF16))
    if res is not None:
        in_specs.append(pl.BlockSpec((tm, tn), lambda j, i: (i, j)))
        args.append(res)
    return pl.pallas_call(
        functools.partial(_mm_kernel, n_pairs=len(pairs), has_res=res is not None, w_transposed=w_transposed),
        grid=(n // tn, m // tm),
        in_specs=in_specs,
        out_specs=pl.BlockSpec((tm, tn), lambda j, i: (i, j)),
        out_shape=jax.ShapeDtypeStruct((m, n), out_dtype),
        scratch_shapes=scratch,
        compiler_params=_cparams("parallel", "arbitrary"),
        name="matmul",
    )(*args)


def _mm_rows_kernel(a_ref, w_ref, res_ref, o_ref):
    o_ref[...] = (_dot(a_ref[...], w_ref[...]) + res_ref[...]).astype(o_ref.dtype)


def _matmul_rows(a, w, layer, res, tm=1024, tn=512):
    m, k = a.shape
    n = w.shape[2]
    tm = min(tm, m)
    return pl.pallas_call(
        _mm_rows_kernel,
        grid=(m // tm, n // tn),
        in_specs=[pl.BlockSpec((tm, k), lambda i, j: (i, 0)),
                  pl.BlockSpec((None, k, tn), lambda i, j: (layer, 0, j)),
                  pl.BlockSpec((tm, tn), lambda i, j: (i, j))],
        out_specs=pl.BlockSpec((tm, tn), lambda i, j: (i, j)),
        out_shape=jax.ShapeDtypeStruct((m, n), res.dtype),
        compiler_params=_cparams("parallel", "parallel"),
        name="matmul_rows",
    )(a, w, res)


def _ffn_up_kernel(h_ref, wg_ref, wv_ref, cg_ref, cv_ref, o_ref, carry_ref, wgb_ref, wvb_ref, *, blocks_per_seq):
    i = pl.program_id(1)

    @pl.when(i == 0)
    def _():
        wgb_ref[...] = wg_ref[...].astype(BF16)
        wvb_ref[...] = wv_ref[...].astype(BF16)

    @pl.when(i % blocks_per_seq == 0)
    def _():
        carry_ref[...] = jnp.zeros_like(carry_ref)

    h = h_ref[...]
    zg = _dot(h, wgb_ref[...])
    zv = _dot(h, wvb_ref[...])
    tm = zg.shape[0]
    row = lax.broadcasted_iota(jnp.int32, zg.shape, 0)

    def conv(z, c_ref, prev):
        m1 = jnp.where(row == 0, prev[7:8, :], pltpu.roll(z, 1, 0))
        m2 = jnp.where(row == 0, prev[6:7, :], jnp.where(row == 1, prev[7:8, :], pltpu.roll(z, 2, 0)))
        c = c_ref[...]
        return c[0:1, :] * m2 + c[1:2, :] * m1 + c[2:3, :] * z

    g = conv(zg, cg_ref, carry_ref[0])
    v = conv(zv, cv_ref, carry_ref[1])
    carry_ref[0] = zg[tm - SUBLANES:, :]
    carry_ref[1] = zv[tm - SUBLANES:, :]
    o_ref[...] = (g * jax.nn.sigmoid(g) * v).astype(o_ref.dtype)


def _ffn_up(hn, w_up, conv_w, layer, seq):
    m, d = hn.shape
    f = w_up.shape[2] // 2
    tm = min(1024, seq)
    tn = _pick_tile(f, (512, 256, 128))
    nj = f // tn
    return pl.pallas_call(
        functools.partial(_ffn_up_kernel, blocks_per_seq=seq // tm),
        grid=(nj, m // tm),
        in_specs=[
            pl.BlockSpec((tm, d), lambda j, i: (i, 0)),
            pl.BlockSpec((None, d, tn), lambda j, i: (layer, 0, j)),
            pl.BlockSpec((None, d, tn), lambda j, i: (layer, 0, j + nj)),
            pl.BlockSpec((None, 3, tn), lambda j, i: (layer, 0, j)),
            pl.BlockSpec((None, 3, tn), lambda j, i: (layer, 0, j + nj)),
        ],
        out_specs=pl.BlockSpec((tm, tn), lambda j, i: (i, j)),
        out_shape=jax.ShapeDtypeStruct((m, f), BF16),
        scratch_shapes=[pltpu.VMEM((2, SUBLANES, tn), F32), pltpu.VMEM((d, tn), BF16), pltpu.VMEM((d, tn), BF16)],
        compiler_params=_cparams("parallel", "arbitrary"),
        name="ffn_up_conv",
    )(hn, w_up, w_up, conv_w, conv_w)


def _gdn_kernel(z_ref, zs_ref, cw_ref, alog_ref, dtb_ref, nw_ref, o_ref, s_ref, prev_ref, *, prec):
    c = GDN_CHUNK
    nqkv = 3 * HALF

    @pl.when(pl.program_id(1) == 0)
    def _():
        s_ref[...] = jnp.zeros_like(s_ref)
        prev_ref[...] = jnp.zeros_like(prev_ref)

    z = z_ref[0, :, :nqkv]
    prev = prev_ref[...]
    cw = cw_ref[...]
    row8 = lax.broadcasted_iota(jnp.int32, (SUBLANES, nqkv), 0)
    acc = cw[GDN_CONV - 1:GDN_CONV, :] * z
    for tap in range(1, GDN_CONV):
        zr = pltpu.roll(z, tap, 0)
        head = jnp.where(row8 < tap, pltpu.roll(prev, tap, 0), zr[:SUBLANES, :])
        acc = acc + cw[GDN_CONV - 1 - tap:GDN_CONV - tap, :] * jnp.concatenate([head, zr[SUBLANES:, :]], axis=0)
    prev_ref[...] = z[c - SUBLANES:, :]
    x = acc * jax.nn.sigmoid(acc)

    zs = zs_ref[0]
    beta_all = jax.nn.sigmoid(zs)
    g_all = -jnp.exp(alog_ref[...]) * jax.nn.softplus(zs + dtb_ref[...])
    row, col = _iota2((c, c))
    lower = jnp.where(row >= col, 1.0, 0.0).astype(F32)
    heads = range(GDN_HEADS)
    per_head = lambda off: jnp.stack([x[:, off + h * GDN_D:off + (h + 1) * GDN_D] for h in heads])
    l2n = lambda t: t * lax.rsqrt(jnp.sum(t * t, axis=-1, keepdims=True) + 1e-6)
    q = l2n(per_head(0)) * (GDN_D ** -0.5)
    k = l2n(per_head(HALF))
    v = per_head(2 * HALF)
    beta = jnp.stack([beta_all[:, h:h + 1] for h in heads])
    g = jnp.stack([g_all[:, GDN_HEADS + h:GDN_HEADS + h + 1] for h in heads])
    lower3 = jnp.broadcast_to(lower, (GDN_HEADS, c, c))
    diff = _dot_01_left(lower3, jnp.where(row > col, jnp.broadcast_to(g, (GDN_HEADS, c, c)), 0.0))
    decay = jnp.where(row >= col, jnp.exp(diff), 0.0)
    gc = _dot_01_left(lower3, jnp.broadcast_to(g, (GDN_HEADS, c, GDN_D)))
    gl = gc[:, c - 1:c, :]
    egc = jnp.exp(gc)
    kb = k * beta
    kk = _dot_nt(jnp.concatenate([kb, q], axis=1), k, prec)
    a = jnp.where(row > col, kk[:, :c, :] * decay, 0.0)
    attn = kk[:, c:, :] * decay
    tinv = _unit_lower_inverse(a, row, col, c, prec)
    sol = _dot(tinv, jnp.concatenate([v * beta, kb * egc], axis=2), prec)
    u, w = sol[:, :, :GDN_D], sol[:, :, GDN_D:]
    s = s_ref[...]
    v_new = u - _dot(w, s, prec)
    o = _dot(jnp.concatenate([q * egc, attn], axis=2), jnp.concatenate([s, v_new], axis=1), prec)
    s_ref[...] = s * jnp.exp(gl) + _dot_tn(k * jnp.exp(gl - gc), v_new, prec)
    o = o * lax.rsqrt(jnp.mean(o * o, axis=-1, keepdims=True) + EPS) * nw_ref[...]
    for h in heads:
        gate = z_ref[0, :, nqkv + h * GDN_D:nqkv + (h + 1) * GDN_D]
        o_ref[0, :, h * GDN_D:(h + 1) * GDN_D] = (o[h] * gate * jax.nn.sigmoid(gate)).astype(o_ref.dtype)


def _gdn(z3, zs3, zs_blk, conv_w, a_log, dt_bias, norm_w, prec):
    b, t, nz = z3.shape
    c = GDN_CHUNK
    pad = lambda p: jnp.zeros((1, LANES), F32).at[0, GDN_HEADS:2 * GDN_HEADS].set(p.astype(F32))
    vec = pl.BlockSpec((1, LANES), lambda bi, ni: (0, 0))
    return pl.pallas_call(
        functools.partial(_gdn_kernel, prec=prec),
        grid=(b, t // c),
        in_specs=[pl.BlockSpec((1, c, nz), lambda bi, ni: (bi, ni, 0)),
                  pl.BlockSpec((1, c, LANES), lambda bi, ni: (bi, ni, zs_blk)),
                  pl.BlockSpec((GDN_CONV, 3 * HALF), lambda bi, ni: (0, 0)),
                  vec, vec, vec],
        out_specs=pl.BlockSpec((1, c, HALF), lambda bi, ni: (bi, ni, 0)),
        out_shape=jax.ShapeDtypeStruct((b, t, HALF), BF16),
        scratch_shapes=[pltpu.VMEM((GDN_HEADS, GDN_D, GDN_D), F32), pltpu.VMEM((SUBLANES, 3 * HALF), F32)],
        compiler_params=_cparams("parallel", "arbitrary"),
        name="gdn_chunk",
    )(z3, zs3, conv_w.astype(F32), pad(a_log), pad(dt_bias), norm_w.reshape(1, GDN_D).astype(F32))


def _s5_param_kernel(lr_ref, li_ref, dt_ref, br_ref, bi_ref, ar_ref, ai_ref, bbr_ref, bbi_ref):
    lr, li, dt = lr_ref[...], li_ref[...], dt_ref[...]
    step = jnp.exp(dt)
    mag = jnp.exp(lr * step)
    ang = li * step
    ab_re, ab_im = mag * jnp.cos(ang), mag * jnp.sin(ang)
    den = lr * lr + li * li
    nr = ab_re - 1.0
    f_re = (nr * lr + ab_im * li) / den
    f_im = (ab_im * lr - nr * li) / den
    br, bi = br_ref[...], bi_ref[...]
    bbr_ref[...] = f_re * br - f_im * bi
    bbi_ref[...] = f_re * bi + f_im * br
    n = ab_re.shape[1]
    cmul = lambda xr, xi, yr, yi: (xr * yr - xi * yi, xr * yi + xi * yr)
    row = lax.broadcasted_iota(jnp.int32, (SUBLANES, n), 0)
    cur = (ab_re, ab_im)
    pr = jnp.broadcast_to(ab_re, (SUBLANES, n))
    pi = jnp.broadcast_to(ab_im, (SUBLANES, n))
    for r in range(1, SUBLANES):
        cur = cmul(cur[0], cur[1], ab_re, ab_im)
        pr = jnp.where(row == r, cur[0], pr)
        pi = jnp.where(row == r, cur[1], pi)
    rows = SUBLANES
    while rows < ar_ref.shape[0]:
        tr, ti = cmul(pr, pi, pr[rows - 1:rows, :], pi[rows - 1:rows, :])
        pr = jnp.concatenate([pr, tr], axis=0)
        pi = jnp.concatenate([pi, ti], axis=0)
        rows *= 2
    ar_ref[...] = pr
    ai_ref[...] = pi


def _s5_params(lam_re, lam_im, log_step, b_re, b_im, n_pow):
    assert n_pow >= SUBLANES and n_pow & (n_pow - 1) == 0
    gp = S5_GROUPS * S5_STATE
    row = lambda x: x.astype(F32).reshape(1, gp)
    bt = lambda x: jnp.transpose(x.astype(F32), (2, 0, 1)).reshape(S5_GROUP, gp)
    pshape = jax.ShapeDtypeStruct((n_pow, gp), F32)
    mshape = jax.ShapeDtypeStruct((S5_GROUP, gp), F32)
    return pl.pallas_call(
        _s5_param_kernel,
        out_shape=[pshape, pshape, mshape, mshape],
        name="s5_params",
    )(row(lam_re), row(lam_im), row(jnp.repeat(log_step[:, None], S5_STATE, axis=1)), bt(b_re), bt(b_im))


def _s5_scan_kernel(u_ref, bdr_ref, bdi_ref, cdr_ref, cdi_ref, pr_ref, pi_ref, d_ref, o_ref,
                    up_ref, xr_ref, xi_ref, y_ref):
    t = u_ref.shape[1]
    n = S5_TILE_STATES
    nk = t // SUBLANES
    rt = min(512, t)

    def permute_in(k, _):
        r0 = pl.multiple_of(k * SUBLANES, SUBLANES)
        up_ref[pl.ds(r0, SUBLANES), :] = u_ref[0, pl.ds(k, SUBLANES, stride=nk), :]
        return 0

    lax.fori_loop(0, nk, permute_in, 0, unroll=SUBLANES)

    for r0 in range(0, t, rt):
        ub = up_ref[r0:r0 + rt, :].astype(BF16)
        xr_ref[r0:r0 + rt, :] = _dot(ub, bdr_ref[0])
        xi_ref[r0:r0 + rt, :] = _dot(ub, bdi_ref[0])

    ar = jnp.broadcast_to(pr_ref[0:1, :], (SUBLANES, n))
    ai = jnp.broadcast_to(pi_ref[0:1, :], (SUBLANES, n))

    def local_scan(k, carry):
        cr, ci = carry
        r0 = pl.multiple_of(k * SUBLANES, SUBLANES)
        xr = xr_ref[pl.ds(r0, SUBLANES), :] + (ar * cr - ai * ci)
        xi = xi_ref[pl.ds(r0, SUBLANES), :] + (ar * ci + ai * cr)
        xr_ref[pl.ds(r0, SUBLANES), :] = xr
        xi_ref[pl.ds(r0, SUBLANES), :] = xi
        return xr, xi

    zero = jnp.zeros((SUBLANES, n), F32)
    fr, fi = lax.fori_loop(0, nk, local_scan, (zero, zero), unroll=SUBLANES)

    row = lax.broadcasted_iota(jnp.int32, (SUBLANES, n), 0)
    cmul = lambda xr, xi, yr, yi: (xr * yr - xi * yi, xr * yi + xi * yr)
    gr = jnp.broadcast_to(pr_ref[nk - 1:nk, :], (SUBLANES, n))
    gi = jnp.broadcast_to(pi_ref[nk - 1:nk, :], (SUBLANES, n))
    for d in (1, 2, 4):
        sr = jnp.where(row >= d, pltpu.roll(fr, d, 0), 0.0)
        si = jnp.where(row >= d, pltpu.roll(fi, d, 0), 0.0)
        tr, ti = cmul(gr, gi, sr, si)
        fr, fi = fr + tr, fi + ti
        gr, gi = cmul(gr, gi, gr, gi)
    cr = jnp.where(row >= 1, pltpu.roll(fr, 1, 0), 0.0)
    ci = jnp.where(row >= 1, pltpu.roll(fi, 1, 0), 0.0)

    def add_carry(k8, _):
        p0 = pl.multiple_of(k8 * SUBLANES, SUBLANES)
        pr8 = pr_ref[pl.ds(p0, SUBLANES), :]
        pi8 = pi_ref[pl.ds(p0, SUBLANES), :]
        for j in range(SUBLANES):
            r0 = pl.multiple_of((k8 * SUBLANES + j) * SUBLANES, SUBLANES)
            pr = jnp.broadcast_to(pr8[j:j + 1, :], (SUBLANES, n))
            pi = jnp.broadcast_to(pi8[j:j + 1, :], (SUBLANES, n))
            xr_ref[pl.ds(r0, SUBLANES), :] = xr_ref[pl.ds(r0, SUBLANES), :] + (pr * cr - pi * ci)
            xi_ref[pl.ds(r0, SUBLANES), :] = xi_ref[pl.ds(r0, SUBLANES), :] + (pr * ci + pi * cr)
        return 0

    lax.fori_loop(0, nk // SUBLANES, add_carry, 0)

    for r0 in range(0, t, rt):
        y = (_dot(xr_ref[r0:r0 + rt, :].astype(BF16), cdr_ref[0])
             - _dot(xi_ref[r0:r0 + rt, :].astype(BF16), cdi_ref[0]) + up_ref[r0:r0 + rt, :] * d_ref[...])
        y_ref[r0:r0 + rt, :] = jax.nn.gelu(y)

    def permute_out(k, _):
        r0 = pl.multiple_of(k * SUBLANES, SUBLANES)
        o_ref[0, pl.ds(k, SUBLANES, stride=nk), :] = y_ref[pl.ds(r0, SUBLANES), :]
        return 0

    lax.fori_loop(0, nk, permute_out, 0, unroll=SUBLANES)


def _s5_scan(z3, u_off, bd_re, bd_im, cd_re, cd_im, p_re, p_im, d_skip):
    b, t, _ = z3.shape
    nt = HALF // LANES
    n = S5_TILE_STATES
    nk = t // SUBLANES
    assert p_re.shape[0] == nk
    return pl.pallas_call(
        _s5_scan_kernel,
        grid=(b, nt),
        in_specs=[
            pl.BlockSpec((1, t, LANES), lambda bi, j: (bi, 0, j + u_off)),
            pl.BlockSpec((1, LANES, n), lambda bi, j: (j, 0, 0)),
            pl.BlockSpec((1, LANES, n), lambda bi, j: (j, 0, 0)),
            pl.BlockSpec((1, n, LANES), lambda bi, j: (j, 0, 0)),
            pl.BlockSpec((1, n, LANES), lambda bi, j: (j, 0, 0)),
            pl.BlockSpec((nk, n), lambda bi, j: (0, j)),
            pl.BlockSpec((nk, n), lambda bi, j: (0, j)),
            pl.BlockSpec((1, LANES), lambda bi, j: (0, j)),
        ],
        out_specs=pl.BlockSpec((1, t, LANES), lambda bi, j: (bi, 0, j)),
        out_shape=jax.ShapeDtypeStruct((b, t, HALF), F32),
        scratch_shapes=[pltpu.VMEM((t, LANES), F32), pltpu.VMEM((t, n), F32), pltpu.VMEM((t, n), F32),
                        pltpu.VMEM((t, LANES), F32)],
        compiler_params=_cparams("parallel", "parallel"),
        name="s5_scan",
    )(z3, bd_re, bd_im, cd_re, cd_im, p_re, p_im, d_skip.reshape(1, HALF).astype(F32))


def _glu_kernel(y_ref, w_ref, yt_ref, o_ref):
    gate = _dot(y_ref[...].astype(BF16), w_ref[...])
    o_ref[...] = (yt_ref[...] * jax.nn.sigmoid(gate)).astype(o_ref.dtype)


def _glu(y, w):
    m, k = y.shape
    tm = min(1024, m)
    tn = 512
    return pl.pallas_call(
        _glu_kernel,
        grid=(m // tm, k // tn),
        in_specs=[pl.BlockSpec((tm, k), lambda i, j: (i, 0)),
                  pl.BlockSpec((k, tn), lambda i, j: (0, j)),
                  pl.BlockSpec((tm, tn), lambda i, j: (i, j))],
        out_specs=pl.BlockSpec((tm, tn), lambda i, j: (i, j)),
        out_shape=jax.ShapeDtypeStruct((m, k), BF16),
        compiler_params=_cparams("parallel", "parallel"),
        name="s5_glu",
    )(y, w, y)


def _s5_block_diag(bb_re, bb_im, c_re, c_im):
    nt, tg = HALF // LANES, S5_TILE_GROUPS
    eye = jnp.eye(tg, dtype=F32)

    def bmap(bb):
        x = bb.reshape(S5_GROUP, nt, tg, S5_STATE)
        x = jnp.einsum('cjgp,gh->jgchp', x, eye)
        return x.reshape(nt, LANES, S5_TILE_STATES).astype(BF16)

    def cmap(cc):
        x = cc.astype(F32).reshape(nt, tg, S5_GROUP, S5_STATE)
        x = jnp.einsum('jgcp,gh->jgphc', x, eye)
        return x.reshape(nt, S5_TILE_STATES, LANES).astype(BF16)

    return bmap(bb_re), bmap(bb_im), cmap(c_re), cmap(c_im)


def _seg_sum(x, seg):
    row, col = _iota2((LANES, LANES))
    shift = seg.bit_length() - 1
    ones = jnp.where((row >> shift) == (col >> shift), 1.0, 0.0).astype(F32)
    return _dot_01_right(x, ones)


def _rwkv_kernel(z_ref, zl_ref, mum_ref, mul_ref, w2_ref, a2_ref, g2_ref, w0_ref, a0_ref, kkp_ref, ka_ref,
                 rk_ref, lnw_ref, lnb_ref, o_ref, h_ref, prevm_ref, prevl_ref, *, prec):
    c = RWKV_CHUNK
    c2 = 2 * c

    @pl.when(pl.program_id(1) == 0)
    def _():
        h_ref[...] = jnp.zeros_like(h_ref)
        prevm_ref[...] = jnp.zeros_like(prevm_ref)
        prevl_ref[...] = jnp.zeros_like(prevl_ref)

    def shift_mix(z, prev_ref, mu_ref):
        row0 = lax.broadcasted_iota(jnp.int32, z.shape, 0) == 0
        zm1 = jnp.where(row0, prev_ref[SUBLANES - 1:SUBLANES, :], pltpu.roll(z, 1, 0))
        prev_ref[...] = z[c - SUBLANES:, :]
        return z + (zm1 - z) * mu_ref[...]

    x = shift_mix(z_ref[0], prevm_ref, mum_ref)
    zl = shift_mix(zl_ref[0], prevl_ref, mul_ref)
    r_all, k_raw, v_all = x[:, :HALF], x[:, HALF:2 * HALF], x[:, 2 * HALF:]
    wa = zl[:, :LANES]
    w = w0_ref[...] + _dot(jnp.tanh(wa), w2_ref[...])
    w = -jax.nn.softplus(-w) - 0.5
    lw_all = -jnp.exp(w)
    a = jax.nn.sigmoid(a0_ref[...] + _dot(wa, a2_ref[...]))
    g_all = _dot(jax.nn.sigmoid(zl[:, LANES:]), g2_ref[...])
    k_all = k_raw * (1.0 + (a - 1.0) * ka_ref[...])

    row1, col1 = _iota2((c, c))
    lower = jnp.where(row1 >= col1, 1.0, 0.0).astype(F32)
    row, col = _iota2((c2, c2))
    same = (row >> 6) == (col >> 6)
    strict = jnp.logical_and(same, row > col)
    incl = jnp.logical_and(same, row >= col)
    lane = lax.broadcasted_iota(jnp.int32, (c, LANES), 1)
    first_head = lane < RWKV_HEAD

    def stack2(x):
        return jnp.concatenate([jnp.where(first_head, x, 0.0), jnp.where(first_head, 0.0, x)], axis=1)

    pairs = range(RWKV_PAIRS)
    npair = RWKV_PAIRS
    per_pair = lambda x: jnp.stack([x[:, p * LANES:(p + 1) * LANES] for p in pairs])
    seg_sum = lambda x: _seg_sum(x.reshape(npair * c, LANES), RWKV_HEAD).reshape(npair, c, LANES)
    r, lw, k, v = per_pair(r_all), per_pair(lw_all), per_pair(k_all), per_pair(v_all)
    kk = per_pair(k_raw * kkp_ref[...])
    kk = kk * lax.rsqrt(seg_sum(kk * kk) + 1e-6)
    b = kk * per_pair(a)
    cl = per_pair(_dot_01_left(lower, lw_all))
    cl_last = cl[:, c - 1:c, :]
    e_neg = jnp.exp(-cl)
    e_tail = jnp.exp(cl_last - cl)
    kk2 = stack2(kk * jnp.exp(cl - lw))
    r2 = stack2(r * jnp.exp(cl))
    b2 = stack2(b * e_neg)
    k2 = stack2(k * e_neg)
    v2 = stack2(v)
    bd2 = stack2(b * e_tail)
    kd2 = stack2(k * e_tail)
    sc = _dot_nt(jnp.concatenate([kk2, r2], axis=1), jnp.concatenate([b2, k2], axis=1), prec)
    a_ab = jnp.where(strict, sc[:, :c2, :c2], 0.0)
    a_ak = jnp.where(strict, sc[:, :c2, c2:], 0.0)
    r_bk = jnp.concatenate([jnp.where(incl, sc[:, c2:, :c2], 0.0), jnp.where(incl, sc[:, c2:, c2:], 0.0)], axis=2)
    tinv = _unit_lower_inverse(a_ab, row, col, c, prec)
    wt = _dot(tinv, jnp.concatenate([kk2, _dot(a_ak, v2, prec)], axis=2), prec)
    wk, tv = wt[:, :, :LANES], wt[:, :, LANES:]
    ht = h_ref[...]
    hp = _dot_nt(jnp.concatenate([wk, r2], axis=1), ht, prec)
    u = -hp[:, :c2, :] - tv
    uv = jnp.concatenate([u, v2], axis=1)
    y2 = hp[:, c2:, :] + _dot(r_bk, uv, prec)
    h_ref[...] = ht * jnp.exp(cl_last) + _dot_tn(uv, jnp.concatenate([bd2, kd2], axis=1), prec)
    y = y2[:, :c, :] + y2[:, c:, :]
    mu = seg_sum(y) * (1.0 / RWKV_HEAD)
    d = y - mu
    var = seg_sum(d * d) * (1.0 / RWKV_HEAD)
    yn = d * lax.rsqrt(var + RWKV_GN_EPS) * per_pair(lnw_ref[...]) + per_pair(lnb_ref[...])
    out = (yn + seg_sum(r * k * per_pair(rk_ref[...])) * v) * per_pair(g_all)
    for p in pairs:
        o_ref[0, :, p * LANES:(p + 1) * LANES] = out[p].astype(o_ref.dtype)


def _rwkv(z3, zl3, lora_blk, mu_main, mu_lora, w2p, a2p, g2p, w0, a0, k_k, k_a, r_k, ln_w, ln_b, prec):
    bsz, t, nz = z3.shape
    c = RWKV_CHUNK
    lw = RWKV_LORA_PAD
    vec = pl.BlockSpec((1, HALF), lambda bi, ni: (0, 0))
    full = lambda rows: pl.BlockSpec((rows, HALF), lambda bi, ni: (0, 0))
    row = lambda x: x.astype(F32).reshape(1, HALF)
    return pl.pallas_call(
        functools.partial(_rwkv_kernel, prec=prec),
        grid=(bsz, t // c),
        in_specs=[pl.BlockSpec((1, c, nz), lambda bi, ni: (bi, ni, 0)),
                  pl.BlockSpec((1, c, lw), lambda bi, ni: (bi, ni, lora_blk)),
                  pl.BlockSpec((1, nz), lambda bi, ni: (0, 0)),
                  pl.BlockSpec((1, lw), lambda bi, ni: (0, 0)),
                  full(LANES), full(LANES), full(lw - LANES)] + [vec] * 7,
        out_specs=pl.BlockSpec((1, c, HALF), lambda bi, ni: (bi, ni, 0)),
        out_shape=jax.ShapeDtypeStruct((bsz, t, HALF), BF16),
        scratch_shapes=[pltpu.VMEM((RWKV_PAIRS, LANES, LANES), F32), pltpu.VMEM((SUBLANES, nz), F32),
                        pltpu.VMEM((SUBLANES, lw), F32)],
        compiler_params=_cparams("parallel", "arbitrary"),
        name="rwkv_chunk",
    )(z3, zl3, mu_main, mu_lora, w2p, a2p, g2p, row(w0), row(a0), row(k_k), row(k_a), row(r_k), row(ln_w),
      row(ln_b))


def _rope_kernel(f_ref, cos_ref, sin_ref):
    t = cos_ref.shape[0]
    pos = lax.broadcasted_iota(jnp.int32, (t, LANES), 0).astype(F32)
    lane = lax.broadcasted_iota(jnp.int32, (t, LANES), 1)
    ang = pos * f_ref[...]
    cos_ref[...] = jnp.cos(ang)
    sin_ref[...] = jnp.where(lane < RET_DK // 2, -1.0, 1.0) * jnp.sin(ang)


def _rope_tables(t):
    inv_freq = ROPE_BASE ** (-jnp.linspace(0.0, 1.0, RET_DK // 2, dtype=F32))
    f2 = jnp.concatenate([inv_freq, inv_freq]).reshape(1, RET_DK)
    shape = jax.ShapeDtypeStruct((t, RET_DK), F32)
    return pl.pallas_call(_rope_kernel, out_shape=[shape, shape], name="rope_tables")(f2)


def _ret_kernel(q_ref, k_ref, v_ref, gate_ref, cos_ref, sin_ref, o_ref, s_ref):
    c = RET_CHUNK

    @pl.when(pl.program_id(1) == 0)
    def _():
        s_ref[...] = jnp.zeros_like(s_ref)

    row, col = _iota2((c, c))
    dist = (row - col).astype(F32)
    idx = lax.broadcasted_iota(jnp.int32, (c, 1), 0).astype(F32)
    cos, sin = cos_ref[...], sin_ref[...]
    rot = lambda x: x * cos + pltpu.roll(x, RET_DK // 2, 1) * sin

    for h in range(RET_HEADS):
        log_g = math.log(1.0 - 2.0 ** (-5.0 - h))
        q = rot(q_ref[0, :, h * RET_DK:(h + 1) * RET_DK])
        k = rot(k_ref[0, :, h * RET_DK:(h + 1) * RET_DK]) * (RET_DK ** -0.5)
        v = v_ref[0, :, h * RET_DV:(h + 1) * RET_DV]
        vb = v.astype(BF16)
        dmask = jnp.where(row >= col, jnp.exp(log_g * dist), 0.0)
        sc = _dot_nt(q.astype(BF16), k.astype(BF16)) * dmask
        s = s_ref[h]
        o = _dot(sc.astype(BF16), vb) + _dot((q * jnp.exp(log_g * (idx + 1.0))).astype(BF16), s.astype(BF16))
        kd = k * jnp.exp(log_g * (c - 1.0 - idx))
        s_ref[h] = s * math.exp(log_g * c) + _dot_tn(kd.astype(BF16), vb)
        o = o * lax.rsqrt(jnp.mean(o * o, axis=-1, keepdims=True) + EPS)
        gate = gate_ref[0, :, h * RET_DV:(h + 1) * RET_DV]
        o_ref[0, :, h * RET_DV:(h + 1) * RET_DV] = (o * gate * jax.nn.sigmoid(gate)).astype(o_ref.dtype)


def _retention(z3, q_blk, cos, sin):
    b, t, _ = z3.shape
    c = RET_CHUNK
    qk = RET_HEADS * RET_DK
    v_blk = (q_blk * qk + 2 * qk) // HALF
    return pl.pallas_call(
        _ret_kernel,
        grid=(b, t // c),
        in_specs=[
            pl.BlockSpec((1, c, qk), lambda bi, ni: (bi, ni, q_blk)),
            pl.BlockSpec((1, c, qk), lambda bi, ni: (bi, ni, q_blk + 1)),
            pl.BlockSpec((1, c, HALF), lambda bi, ni: (bi, ni, v_blk)),
            pl.BlockSpec((1, c, HALF), lambda bi, ni: (bi, ni, v_blk + 1)),
            pl.BlockSpec((c, RET_DK), lambda bi, ni: (ni, 0)),
            pl.BlockSpec((c, RET_DK), lambda bi, ni: (ni, 0)),
        ],
        out_specs=pl.BlockSpec((1, c, HALF), lambda bi, ni: (bi, ni, 0)),
        out_shape=jax.ShapeDtypeStruct((b, t, HALF), BF16),
        scratch_shapes=[pltpu.VMEM((RET_HEADS, RET_DK, RET_DV), F32)],
        compiler_params=_cparams("parallel", "arbitrary"),
        name="retention_chunk",
    )(z3, z3, z3, z3, cos, sin)


def _even_mixer(hn, bsz, seq, idx, w_in, w_out, conv_w, a_log, dt_bias, norm_w, lam_re, lam_im, b_re, b_im,
                c_re, c_im, d_skip, log_step, w_glu, res, prec):
    m = bsz * seq
    n_main = 4 * HALF
    n_small = 2 * GDN_HEADS
    w_b = jnp.concatenate([w_in[idx, n_main + n_small:, :],
                           jnp.pad(w_in[idx, n_main:n_main + n_small, :], ((0, LANES - n_small), (0, 0)))], axis=0)
    za = _matmul([(hn, w_in, idx, 0)], n=n_main, w_transposed=True).reshape(bsz, seq, n_main)
    zb = _matmul([(hn, w_b[None], 0, 0)], w_transposed=True).reshape(bsz, seq, HALF + LANES)
    ya = _gdn(za, zb, HALF // LANES, conv_w, a_log, dt_bias, norm_w, prec)
    p_re, p_im, bb_re, bb_im = _s5_params(lam_re, lam_im, log_step, b_re, b_im, seq // SUBLANES)
    bd_re, bd_im, cd_re, cd_im = _s5_block_diag(bb_re, bb_im, c_re, c_im)
    yg = _s5_scan(zb, 0, bd_re, bd_im, cd_re, cd_im, p_re, p_im, d_skip)
    yb = _glu(yg.reshape(m, HALF), w_glu.astype(BF16))
    return _matmul([(ya.reshape(m, HALF), w_out, idx, 0), (yb, w_out, idx, 1)], res=res)


def _odd_mixer(hn, bsz, seq, idx, w_in, w_out, shift_mu, w0, w2, a0, a2, g2, k_k, k_a, r_k, ln_w, ln_b, res, prec):
    m = bsz * seq
    n_main = 3 * HALF
    n_lora = 64 + 64 + 160
    w_b = jnp.concatenate([w_in[idx, n_main + n_lora:, :],
                           jnp.pad(w_in[idx, n_main:n_main + n_lora, :], ((0, RWKV_LORA_PAD - n_lora), (0, 0)))],
                          axis=0)
    za = _matmul([(hn, w_in, idx, 0)], n=n_main, w_transposed=True).reshape(bsz, seq, n_main)
    zb = _matmul([(hn, w_b[None], 0, 0)], w_transposed=True).reshape(bsz, seq, n_main + RWKV_LORA_PAD)
    mu_main = shift_mu[:n_main].astype(F32).reshape(1, n_main)
    mu_lora = jnp.pad(shift_mu[n_main:], (0, RWKV_LORA_PAD - n_lora)).astype(F32).reshape(1, RWKV_LORA_PAD)
    w2p = jnp.pad(w2, ((0, LANES - 64), (0, 0))).astype(BF16)
    a2p = jnp.pad(a2, ((64, 0), (0, 0))).astype(BF16)
    g2p = jnp.pad(g2, ((0, RWKV_LORA_PAD - LANES - 160), (0, 0))).astype(BF16)
    yc = _rwkv(za, zb, n_main // RWKV_LORA_PAD, mu_main, mu_lora, w2p, a2p, g2p, w0, a0, k_k, k_a,
               r_k, ln_w, ln_b, prec)
    cos, sin = _rope_tables(seq)
    yd = _retention(zb, 0, cos, sin)
    return _matmul([(yc.reshape(m, HALF), w_out, idx, 0), (yd.reshape(m, HALF), w_out, idx, 1)], res=res)


def kernel(x, norm_mix, norm_ffn, norm_final, ev_w_in, ev_w_out, gdn_conv_w, gdn_a_log, gdn_dt_bias, gdn_norm_w, s5_lam_re, s5_lam_im, s5_b_re, s5_b_im, s5_c_re, s5_c_im, s5_d, s5_log_step, s5_w_glu, od_w_in, od_w_out, rwkv_shift_mu, rwkv_w0, rwkv_w2, rwkv_a0, rwkv_a2, rwkv_g2, rwkv_k_k, rwkv_k_a, rwkv_r_k, rwkv_ln_w, rwkv_ln_b, ffn_w_up, ffn_conv_w, ffn_w_down):
    bsz, seq, d = x.shape
    m = bsz * seq
    depth = norm_mix.shape[0]
    prec = None
    w_down = ffn_w_down.astype(BF16)
    ev_w_in_t = jnp.swapaxes(ev_w_in, 1, 2)
    od_w_in_t = jnp.swapaxes(od_w_in, 1, 2)
    h = x.reshape(m, d).astype(F32)
    for layer in range(depth):
        hn = _rmsnorm(h, norm_mix[layer], BF16)
        i = layer // 2
        if layer % 2 == 0:
            h = _even_mixer(hn, bsz, seq, i, ev_w_in_t, ev_w_out, gdn_conv_w[i], gdn_a_log[i], gdn_dt_bias[i],
                            gdn_norm_w[i], s5_lam_re[i], s5_lam_im[i], s5_b_re[i], s5_b_im[i], s5_c_re[i],
                            s5_c_im[i], s5_d[i], s5_log_step[i], s5_w_glu[i], h, prec)
        else:
            h = _odd_mixer(hn, bsz, seq, i, od_w_in_t, od_w_out, rwkv_shift_mu[i], rwkv_w0[i], rwkv_w2[i],
                           rwkv_a0[i], rwkv_a2[i], rwkv_g2[i], rwkv_k_k[i], rwkv_k_a[i], rwkv_r_k[i],
                           rwkv_ln_w[i], rwkv_ln_b[i], h, prec)
        hn = _rmsnorm(h, norm_ffn[layer], BF16)
        act = _ffn_up(hn, ffn_w_up, ffn_conv_w, layer, seq)
        h = _matmul_rows(act, w_down, layer, h)
    return _rmsnorm(h, norm_final, x.dtype).reshape(bsz, seq, d)
```

```python
import functools
import math

import jax
import jax.numpy as jnp
from jax import lax
from jax.experimental import pallas as pl
from jax.experimental.pallas import tpu as pltpu

F32 = jnp.float32
BF16 = jnp.bfloat16
HI = lax.Precision.HIGHEST

V7X_VMEM_LIMIT_BYTES = 56 * 1024 * 1024
LANES = 128
SUBLANES = 8

EPS = 1e-6
HALF = 1024
GDN_HEADS = 8
GDN_D = 128
GDN_CONV = 4
GDN_CHUNK = 64
S5_GROUP = 16
S5_GROUPS = 64
S5_STATE = 64
S5_TILE_GROUPS = LANES // S5_GROUP
S5_TILE_STATES = S5_TILE_GROUPS * S5_STATE
RWKV_HEAD = 64
RWKV_PAIRS = HALF // LANES
RWKV_CHUNK = 64
RWKV_LORA_PAD = 384
RWKV_GN_EPS = 64e-5
RET_HEADS = 4
RET_DK = 128
RET_DV = 256
RET_CHUNK = 256
ROPE_BASE = 10000.0
INV_BLOCK = 16
CHUNKS_PER_STEP = 4


def _cparams(*sem):
    return pltpu.CompilerParams(dimension_semantics=sem, vmem_limit_bytes=V7X_VMEM_LIMIT_BYTES)


def _dot_dims(a, b, dims, precision):
    if precision is None:
        a, b = a.astype(BF16), b.astype(BF16)
    batch = ((), ())
    if a.ndim == 3:
        dims = tuple(tuple(d + 1 for d in side) for side in dims)
        batch = ((0,), (0,))
    return lax.dot_general(a, b, (dims, batch), preferred_element_type=F32, precision=precision)


def _dot(a, b, precision=None):
    return _dot_dims(a, b, ((1,), (0,)), precision)


def _dot_nt(a, b, precision=None):
    return _dot_dims(a, b, ((1,), (1,)), precision)


def _dot_tn(a, b, precision=None):
    return _dot_dims(a, b, ((0,), (0,)), precision)


def _dot_01_right(x, m01):
    mb = m01.astype(BF16)
    hi = x.astype(BF16)
    lo = (x - hi.astype(F32)).astype(BF16)
    return _dot(hi, mb) + _dot(lo, mb)


def _time_cumsum(x):
    row = lax.broadcasted_iota(jnp.int32, x.shape, 0)
    d = 1
    while d < x.shape[0]:
        x = x + jnp.where(row >= d, pltpu.roll(x, d, 0), 0.0)
        d *= 2
    return x


def _iota2(shape):
    return lax.broadcasted_iota(jnp.int32, shape, 0), lax.broadcasted_iota(jnp.int32, shape, 1)


def _unit_lower_solve(a, rhs, row, col, sub, precision):
    assert sub == 4 * INV_BLOCK
    shift = INV_BLOCK.bit_length() - 1
    eye = jnp.where(row == col, 1.0, 0.0).astype(F32)
    ad = jnp.where((row >> shift) == (col >> shift), a, 0.0)
    ao = a - ad
    a2 = _dot(ad, ad, precision)
    a4 = _dot(a2, a2, precision)
    t2 = _dot(eye - ad, eye + a2, precision)
    a8 = _dot(a4, a4, precision)
    t4 = _dot(t2, eye + a4, precision)
    td = _dot(t4, eye + a8, precision)
    n = _dot(td, ao, precision)
    r = _dot(td, rhs, precision)
    n2 = _dot(n, n, precision)
    return _dot(_dot(eye - n, eye + n2, precision), r, precision)


def _rmsnorm_kernel(x_ref, w_ref, o_ref):
    x = x_ref[...]
    ms = jnp.mean(x * x, axis=-1, keepdims=True)
    o_ref[...] = (x * lax.rsqrt(ms + EPS) * w_ref[...]).astype(o_ref.dtype)


def _rmsnorm(x, w, out_dtype):
    m, d = x.shape
    tm = min(512, m)
    return pl.pallas_call(
        _rmsnorm_kernel,
        grid=(m // tm,),
        in_specs=[pl.BlockSpec((tm, d), lambda i: (i, 0)), pl.BlockSpec((1, d), lambda i: (0, 0))],
        out_specs=pl.BlockSpec((tm, d), lambda i: (i, 0)),
        out_shape=jax.ShapeDtypeStruct((m, d), out_dtype),
        compiler_params=_cparams("parallel"),
        name="rmsnorm",
    )(x, w.reshape(1, d).astype(F32))


def _mm_kernel(*refs, n_pairs, has_res, w_transposed):
    n_in = 2 * n_pairs + int(has_res)
    o_ref = refs[n_in]
    wb_refs = refs[n_in + 1:]

    @pl.when(pl.program_id(1) == 0)
    def _():
        for p in range(n_pairs):
            w = refs[2 * p + 1][...]
            wb_refs[p][...] = (w.T if w_transposed else w).astype(BF16)

    acc = None
    for p in range(n_pairs):
        d = _dot(refs[2 * p][...], wb_refs[p][...])
        acc = d if acc is None else acc + d
    if has_res:
        acc = acc + refs[2 * n_pairs][...]
    o_ref[...] = acc.astype(o_ref.dtype)


def _pick_tile(n, prefs):
    for t in prefs:
        if n % t == 0:
            return t
    return n


def _matmul(pairs, n=None, res=None, out_dtype=F32, tm=1024, tn=None, w_transposed=False):
    m = pairs[0][0].shape[0]
    n = n or pairs[0][1].shape[1 if w_transposed else 2]
    tm = min(tm, m)
    tn = tn or _pick_tile(n, (1024, 1152, 512, 384, 256, 128))
    in_specs, args, scratch = [], [], []
    for a, w, l, r in pairs:
        k = a.shape[1]
        wspec = (pl.BlockSpec((None, tn, k), lambda j, i, l=l, r=r: (l, j, r)) if w_transposed else
                 pl.BlockSpec((None, k, tn), lambda j, i, l=l, r=r: (l, r, j)))
        in_specs += [pl.BlockSpec((tm, k), lambda j, i: (i, 0)), wspec]
        args += [a, w]
        scratch.append(pltpu.VMEM((k, tn), BF16))
    if res is not None:
        in_specs.append(pl.BlockSpec((tm, tn), lambda j, i: (i, j)))
        args.append(res)
    return pl.pallas_call(
        functools.partial(_mm_kernel, n_pairs=len(pairs), has_res=res is not None, w_transposed=w_transposed),
        grid=(n // tn, m // tm),
        in_specs=in_specs,
        out_specs=pl.BlockSpec((tm, tn), lambda j, i: (i, j)),
        out_shape=jax.ShapeDtypeStruct((m, n), out_dtype),
        scratch_shapes=scratch,
        compiler_params=_cparams("parallel", "arbitrary"),
        name="matmul",
    )(*args)


def _mm_rows_kernel(a_ref, w_ref, res_ref, o_ref):
    o_ref[...] = (_dot(a_ref[...], w_ref[...]) + res_ref[...]).astype(o_ref.dtype)


def _matmul_rows(a, w, layer, res, tm=1024, tn=512):
    m, k = a.shape
    n = w.shape[2]
    tm = min(tm, m)
    return pl.pallas_call(
        _mm_rows_kernel,
        grid=(m // tm, n // tn),
        in_specs=[pl.BlockSpec((tm, k), lambda i, j: (i, 0)),
                  pl.BlockSpec((None, k, tn), lambda i, j: (layer, 0, j)),
                  pl.BlockSpec((tm, tn), lambda i, j: (i, j))],
        out_specs=pl.BlockSpec((tm, tn), lambda i, j: (i, j)),
        out_shape=jax.ShapeDtypeStruct((m, n), res.dtype),
        compiler_params=_cparams("parallel", "parallel"),
        name="matmul_rows",
    )(a, w, res)


def _ffn_up_kernel(h_ref, wg_ref, wv_ref, cg_ref, cv_ref, o_ref, carry_ref, wgb_ref, wvb_ref, *, blocks_per_seq):
    i = pl.program_id(1)

    @pl.when(i == 0)
    def _():
        wgb_ref[...] = wg_ref[...].astype(BF16)
        wvb_ref[...] = wv_ref[...].astype(BF16)

    @pl.when(i % blocks_per_seq == 0)
    def _():
        carry_ref[...] = jnp.zeros_like(carry_ref)

    h = h_ref[...]
    zg = _dot(h, wgb_ref[...])
    zv = _dot(h, wvb_ref[...])
    tm = zg.shape[0]
    row = lax.broadcasted_iota(jnp.int32, zg.shape, 0)

    def conv(z, c_ref, prev):
        m1 = jnp.where(row == 0, prev[7:8, :], pltpu.roll(z, 1, 0))
        m2 = jnp.where(row == 0, prev[6:7, :], jnp.where(row == 1, prev[7:8, :], pltpu.roll(z, 2, 0)))
        c = c_ref[...]
        return c[0:1, :] * m2 + c[1:2, :] * m1 + c[2:3, :] * z

    g = conv(zg, cg_ref, carry_ref[0])
    v = conv(zv, cv_ref, carry_ref[1])
    carry_ref[0] = zg[tm - SUBLANES:, :]
    carry_ref[1] = zv[tm - SUBLANES:, :]
    o_ref[...] = (g * jax.nn.sigmoid(g) * v).astype(o_ref.dtype)


def _ffn_up(hn, w_up, conv_w, layer, seq):
    m, d = hn.shape
    f = w_up.shape[2] // 2
    tm = min(1024, seq)
    tn = _pick_tile(f, (512, 256, 128))
    nj = f // tn
    return pl.pallas_call(
        functools.partial(_ffn_up_kernel, blocks_per_seq=seq // tm),
        grid=(nj, m // tm),
        in_specs=[
            pl.BlockSpec((tm, d), lambda j, i: (i, 0)),
            pl.BlockSpec((None, d, tn), lambda j, i: (layer, 0, j)),
            pl.BlockSpec((None, d, tn), lambda j, i: (layer, 0, j + nj)),
            pl.BlockSpec((None, 3, tn), lambda j, i: (layer, 0, j)),
            pl.BlockSpec((None, 3, tn), lambda j, i: (layer, 0, j + nj)),
        ],
        out_specs=pl.BlockSpec((tm, tn), lambda j, i: (i, j)),
        out_shape=jax.ShapeDtypeStruct((m, f), BF16),
        scratch_shapes=[pltpu.VMEM((2, SUBLANES, tn), F32), pltpu.VMEM((d, tn), BF16), pltpu.VMEM((d, tn), BF16)],
        compiler_params=_cparams("parallel", "arbitrary"),
        name="ffn_up_conv",
    )(hn, w_up, w_up, conv_w, conv_w)


def _gdn_kernel(z_ref, zs_ref, cw_ref, alog_ref, dtb_ref, nw_ref, o_ref, s_ref, prev_ref, *, prec):
    c = GDN_CHUNK
    nqkv = 3 * HALF
    rows = z_ref.shape[1]
    nsub = rows // c

    @pl.when(pl.program_id(1) == 0)
    def _():
        s_ref[...] = jnp.zeros_like(s_ref)
        prev_ref[...] = jnp.zeros_like(prev_ref)

    z = z_ref[0, :, :nqkv]
    prev = prev_ref[...]
    cw = cw_ref[...]
    row8 = lax.broadcasted_iota(jnp.int32, (SUBLANES, nqkv), 0)
    acc = cw[GDN_CONV - 1:GDN_CONV, :] * z
    for tap in range(1, GDN_CONV):
        zr = pltpu.roll(z, tap, 0)
        head = jnp.where(row8 < tap, pltpu.roll(prev, tap, 0), zr[:SUBLANES, :])
        acc = acc + cw[GDN_CONV - 1 - tap:GDN_CONV - tap, :] * jnp.concatenate([head, zr[SUBLANES:, :]], axis=0)
    prev_ref[...] = z[rows - SUBLANES:, :]
    x = acc * jax.nn.sigmoid(acc)

    zs = zs_ref[0]
    beta_all = jax.nn.sigmoid(zs)
    g_all = -jnp.exp(alog_ref[...]) * jax.nn.softplus(zs + dtb_ref[...])
    row, col = _iota2((c, c))
    nh = GDN_HEADS
    units = [(sub * c, h) for sub in range(nsub) for h in range(nh)]
    per_unit = lambda off: jnp.stack([x[t0:t0 + c, off + h * GDN_D:off + (h + 1) * GDN_D] for t0, h in units])
    l2n = lambda t: t * lax.rsqrt(jnp.sum(t * t, axis=-1, keepdims=True) + 1e-6)
    q = l2n(per_unit(0)) * (GDN_D ** -0.5)
    k = l2n(per_unit(HALF))
    v = per_unit(2 * HALF)
    beta = jnp.stack([beta_all[t0:t0 + c, h:h + 1] for t0, h in units])
    gcum = [_time_cumsum(g_all[sub * c:(sub + 1) * c, :]) for sub in range(nsub)]
    gcum_t = [jnp.concatenate([gs, jnp.zeros((LANES - c, LANES), F32)], axis=0).T for gs in gcum]
    gc_col = [gcum[t0 // c][:, nh + h:nh + h + 1] for t0, h in units]
    diff = jnp.stack([gc_col[i] - gcum_t[t0 // c][nh + h:nh + h + 1, :c] for i, (t0, h) in enumerate(units)])
    decay = jnp.where(row >= col, jnp.exp(diff), 0.0)
    gc = jnp.stack([jnp.broadcast_to(gcol, (c, GDN_D)) for gcol in gc_col])
    gl = gc[:, c - 1:c, :]
    egc = jnp.exp(gc)
    kb = k * beta
    kk = _dot_nt(jnp.concatenate([kb, q], axis=1), k, prec)
    a = jnp.where(row > col, kk[:, :c, :] * decay, 0.0)
    attn = kk[:, c:, :] * decay
    sol = _unit_lower_solve(a, jnp.concatenate([v * beta, kb * egc], axis=2), row, col, c, prec)
    u, w = sol[:, :, :GDN_D], sol[:, :, GDN_D:]
    qa = jnp.concatenate([q * egc, attn], axis=2)
    kdec = k * jnp.exp(gl - gc)
    egl = jnp.exp(gl)
    s = s_ref[...]
    for sub in range(nsub):
        b0, b1 = sub * nh, (sub + 1) * nh
        v_new = u[b0:b1] - _dot(w[b0:b1], s, prec)
        o = _dot(qa[b0:b1], jnp.concatenate([s, v_new], axis=1), prec)
        s = s * egl[b0:b1] + _dot_tn(kdec[b0:b1], v_new, prec)
        o = o * lax.rsqrt(jnp.mean(o * o, axis=-1, keepdims=True) + EPS) * nw_ref[...]
        for h in range(nh):
            gate = z_ref[0, sub * c:(sub + 1) * c, nqkv + h * GDN_D:nqkv + (h + 1) * GDN_D]
            o_ref[0, sub * c:(sub + 1) * c, h * GDN_D:(h + 1) * GDN_D] = (
                o[h] * gate * jax.nn.sigmoid(gate)).astype(o_ref.dtype)
    s_ref[...] = s


def _gdn(z3, zs3, zs_blk, conv_w, a_log, dt_bias, norm_w, prec):
    b, t, nz = z3.shape
    c = CHUNKS_PER_STEP * GDN_CHUNK
    pad = lambda p: jnp.zeros((1, LANES), F32).at[0, GDN_HEADS:2 * GDN_HEADS].set(p.astype(F32))
    vec = pl.BlockSpec((1, LANES), lambda bi, ni: (0, 0))
    return pl.pallas_call(
        functools.partial(_gdn_kernel, prec=prec),
        grid=(b, t // c),
        in_specs=[pl.BlockSpec((1, c, nz), lambda bi, ni: (bi, ni, 0)),
                  pl.BlockSpec((1, c, LANES), lambda bi, ni: (bi, ni, zs_blk)),
                  pl.BlockSpec((GDN_CONV, 3 * HALF), lambda bi, ni: (0, 0)),
                  vec, vec, vec],
        out_specs=pl.BlockSpec((1, c, HALF), lambda bi, ni: (bi, ni, 0)),
        out_shape=jax.ShapeDtypeStruct((b, t, HALF), BF16),
        scratch_shapes=[pltpu.VMEM((GDN_HEADS, GDN_D, GDN_D), F32), pltpu.VMEM((SUBLANES, 3 * HALF), F32)],
        compiler_params=_cparams("parallel", "arbitrary"),
        name="gdn_chunk",
    )(z3, zs3, conv_w.astype(F32), pad(a_log), pad(dt_bias), norm_w.reshape(1, GDN_D).astype(F32))


def _s5_param_kernel(lr_ref, li_ref, dt_ref, br_ref, bi_ref, ar_ref, ai_ref, bbr_ref, bbi_ref):
    lr, li, dt = lr_ref[...], li_ref[...], dt_ref[...]
    step = jnp.exp(dt)
    mag = jnp.exp(lr * step)
    ang = li * step
    ab_re, ab_im = mag * jnp.cos(ang), mag * jnp.sin(ang)
    den = lr * lr + li * li
    nr = ab_re - 1.0
    f_re = (nr * lr + ab_im * li) / den
    f_im = (ab_im * lr - nr * li) / den
    br, bi = br_ref[...], bi_ref[...]
    bbr_ref[...] = f_re * br - f_im * bi
    bbi_ref[...] = f_re * bi + f_im * br
    n = ab_re.shape[1]
    cmul = lambda xr, xi, yr, yi: (xr * yr - xi * yi, xr * yi + xi * yr)
    row = lax.broadcasted_iota(jnp.int32, (SUBLANES, n), 0)
    cur = (ab_re, ab_im)
    pr = jnp.broadcast_to(ab_re, (SUBLANES, n))
    pi = jnp.broadcast_to(ab_im, (SUBLANES, n))
    for r in range(1, SUBLANES):
        cur = cmul(cur[0], cur[1], ab_re, ab_im)
        pr = jnp.where(row == r, cur[0], pr)
        pi = jnp.where(row == r, cur[1], pi)
    rows = SUBLANES
    while rows < ar_ref.shape[0]:
        tr, ti = cmul(pr, pi, pr[rows - 1:rows, :], pi[rows - 1:rows, :])
        pr = jnp.concatenate([pr, tr], axis=0)
        pi = jnp.concatenate([pi, ti], axis=0)
        rows *= 2
    ar_ref[...] = pr
    ai_ref[...] = pi


def _s5_params(lam_re, lam_im, log_step, b_re, b_im, n_pow):
    assert n_pow >= SUBLANES and n_pow & (n_pow - 1) == 0
    gp = S5_GROUPS * S5_STATE
    row = lambda x: x.astype(F32).reshape(1, gp)
    bt = lambda x: jnp.transpose(x.astype(F32), (2, 0, 1)).reshape(S5_GROUP, gp)
    pshape = jax.ShapeDtypeStruct((n_pow, gp), F32)
    mshape = jax.ShapeDtypeStruct((S5_GROUP, gp), F32)
    return pl.pallas_call(
        _s5_param_kernel,
        out_shape=[pshape, pshape, mshape, mshape],
        name="s5_params",
    )(row(lam_re), row(lam_im), row(jnp.repeat(log_step[:, None], S5_STATE, axis=1)), bt(b_re), bt(b_im))


def _s5_scan_kernel(u_ref, bdr_ref, bdi_ref, cdr_ref, cdi_ref, pr_ref, pi_ref, d_ref, o_ref,
                    up_ref, xr_ref, xi_ref, y_ref):
    t = u_ref.shape[1]
    n = S5_TILE_STATES
    nk = t // SUBLANES
    rt = min(512, t)

    def permute_in(k, _):
        r0 = pl.multiple_of(k * SUBLANES, SUBLANES)
        up_ref[pl.ds(r0, SUBLANES), :] = u_ref[0, pl.ds(k, SUBLANES, stride=nk), :]
        return 0

    lax.fori_loop(0, nk, permute_in, 0, unroll=SUBLANES)

    for r0 in range(0, t, rt):
        ub = up_ref[r0:r0 + rt, :].astype(BF16)
        xr_ref[r0:r0 + rt, :] = _dot(ub, bdr_ref[0])
        xi_ref[r0:r0 + rt, :] = _dot(ub, bdi_ref[0])

    ar = jnp.broadcast_to(pr_ref[0:1, :], (SUBLANES, n))
    ai = jnp.broadcast_to(pi_ref[0:1, :], (SUBLANES, n))

    def local_scan(k, carry):
        cr, ci = carry
        r0 = pl.multiple_of(k * SUBLANES, SUBLANES)
        xr = xr_ref[pl.ds(r0, SUBLANES), :] + (ar * cr - ai * ci)
        xi = xi_ref[pl.ds(r0, SUBLANES), :] + (ar * ci + ai * cr)
        xr_ref[pl.ds(r0, SUBLANES), :] = xr
        xi_ref[pl.ds(r0, SUBLANES), :] = xi
        return xr, xi

    zero = jnp.zeros((SUBLANES, n), F32)
    fr, fi = lax.fori_loop(0, nk, local_scan, (zero, zero), unroll=SUBLANES)

    row = lax.broadcasted_iota(jnp.int32, (SUBLANES, n), 0)
    cmul = lambda xr, xi, yr, yi: (xr * yr - xi * yi, xr * yi + xi * yr)
    gr = jnp.broadcast_to(pr_ref[nk - 1:nk, :], (SUBLANES, n))
    gi = jnp.broadcast_to(pi_ref[nk - 1:nk, :], (SUBLANES, n))
    for d in (1, 2, 4):
        sr = jnp.where(row >= d, pltpu.roll(fr, d, 0), 0.0)
        si = jnp.where(row >= d, pltpu.roll(fi, d, 0), 0.0)
        tr, ti = cmul(gr, gi, sr, si)
        fr, fi = fr + tr, fi + ti
        gr, gi = cmul(gr, gi, gr, gi)
    cr = jnp.where(row >= 1, pltpu.roll(fr, 1, 0), 0.0)
    ci = jnp.where(row >= 1, pltpu.roll(fi, 1, 0), 0.0)

    def add_carry(k8, _):
        p0 = pl.multiple_of(k8 * SUBLANES, SUBLANES)
        pr8 = pr_ref[pl.ds(p0, SUBLANES), :]
        pi8 = pi_ref[pl.ds(p0, SUBLANES), :]
        for j in range(SUBLANES):
            r0 = pl.multiple_of((k8 * SUBLANES + j) * SUBLANES, SUBLANES)
            pr = jnp.broadcast_to(pr8[j:j + 1, :], (SUBLANES, n))
            pi = jnp.broadcast_to(pi8[j:j + 1, :], (SUBLANES, n))
            xr_ref[pl.ds(r0, SUBLANES), :] = xr_ref[pl.ds(r0, SUBLANES), :] + (pr * cr - pi * ci)
            xi_ref[pl.ds(r0, SUBLANES), :] = xi_ref[pl.ds(r0, SUBLANES), :] + (pr * ci + pi * cr)
        return 0

    lax.fori_loop(0, nk // SUBLANES, add_carry, 0)

    for r0 in range(0, t, rt):
        y = (_dot(xr_ref[r0:r0 + rt, :].astype(BF16), cdr_ref[0])
             - _dot(xi_ref[r0:r0 + rt, :].astype(BF16), cdi_ref[0]) + up_ref[r0:r0 + rt, :] * d_ref[...])
        y_ref[r0:r0 + rt, :] = jax.nn.gelu(y)

    def permute_out(k, _):
        r0 = pl.multiple_of(k * SUBLANES, SUBLANES)
        o_ref[0, pl.ds(k, SUBLANES, stride=nk), :] = y_ref[pl.ds(r0, SUBLANES), :]
        return 0

    lax.fori_loop(0, nk, permute_out, 0, unroll=SUBLANES)


def _s5_scan(z3, u_off, bd_re, bd_im, cd_re, cd_im, p_re, p_im, d_skip):
    b, t, _ = z3.shape
    nt = HALF // LANES
    n = S5_TILE_STATES
    nk = t // SUBLANES
    assert p_re.shape[0] == nk
    return pl.pallas_call(
        _s5_scan_kernel,
        grid=(b, nt),
        in_specs=[
            pl.BlockSpec((1, t, LANES), lambda bi, j: (bi, 0, j + u_off)),
            pl.BlockSpec((1, LANES, n), lambda bi, j: (j, 0, 0)),
            pl.BlockSpec((1, LANES, n), lambda bi, j: (j, 0, 0)),
            pl.BlockSpec((1, n, LANES), lambda bi, j: (j, 0, 0)),
            pl.BlockSpec((1, n, LANES), lambda bi, j: (j, 0, 0)),
            pl.BlockSpec((nk, n), lambda bi, j: (0, j)),
            pl.BlockSpec((nk, n), lambda bi, j: (0, j)),
            pl.BlockSpec((1, LANES), lambda bi, j: (0, j)),
        ],
        out_specs=pl.BlockSpec((1, t, LANES), lambda bi, j: (bi, 0, j)),
        out_shape=jax.ShapeDtypeStruct((b, t, HALF), F32),
        scratch_shapes=[pltpu.VMEM((t, LANES), F32), pltpu.VMEM((t, n), F32), pltpu.VMEM((t, n), F32),
                        pltpu.VMEM((t, LANES), F32)],
        compiler_params=_cparams("parallel", "parallel"),
        name="s5_scan",
    )(z3, bd_re, bd_im, cd_re, cd_im, p_re, p_im, d_skip.reshape(1, HALF).astype(F32))


def _glu_kernel(y_ref, w_ref, yt_ref, o_ref):
    gate = _dot(y_ref[...].astype(BF16), w_ref[...])
    o_ref[...] = (yt_ref[...] * jax.nn.sigmoid(gate)).astype(o_ref.dtype)


def _glu(y, w):
    m, k = y.shape
    tm = min(1024, m)
    tn = 512
    return pl.pallas_call(
        _glu_kernel,
        grid=(m // tm, k // tn),
        in_specs=[pl.BlockSpec((tm, k), lambda i, j: (i, 0)),
                  pl.BlockSpec((k, tn), lambda i, j: (0, j)),
                  pl.BlockSpec((tm, tn), lambda i, j: (i, j))],
        out_specs=pl.BlockSpec((tm, tn), lambda i, j: (i, j)),
        out_shape=jax.ShapeDtypeStruct((m, k), BF16),
        compiler_params=_cparams("parallel", "parallel"),
        name="s5_glu",
    )(y, w, y)


def _s5_block_diag(bb_re, bb_im, c_re, c_im):
    nt, tg = HALF // LANES, S5_TILE_GROUPS
    eye = jnp.eye(tg, dtype=F32)

    def bmap(bb):
        x = bb.reshape(S5_GROUP, nt, tg, S5_STATE)
        x = jnp.einsum('cjgp,gh->jgchp', x, eye)
        return x.reshape(nt, LANES, S5_TILE_STATES).astype(BF16)

    def cmap(cc):
        x = cc.astype(F32).reshape(nt, tg, S5_GROUP, S5_STATE)
        x = jnp.einsum('jgcp,gh->jgphc', x, eye)
        return x.reshape(nt, S5_TILE_STATES, LANES).astype(BF16)

    return bmap(bb_re), bmap(bb_im), cmap(c_re), cmap(c_im)


def _seg_sum(x, seg):
    row, col = _iota2((LANES, LANES))
    shift = seg.bit_length() - 1
    ones = jnp.where((row >> shift) == (col >> shift), 1.0, 0.0).astype(F32)
    return _dot_01_right(x, ones)


def _rwkv_kernel(z_ref, zl_ref, mum_ref, mul_ref, w2_ref, a2_ref, g2_ref, w0_ref, a0_ref, kkp_ref, ka_ref,
                 rk_ref, lnw_ref, lnb_ref, o_ref, h_ref, prevm_ref, prevl_ref, *, prec):
    c = RWKV_CHUNK
    c2 = 2 * c
    rows = z_ref.shape[1]
    nsub = rows // c

    @pl.when(pl.program_id(1) == 0)
    def _():
        h_ref[...] = jnp.zeros_like(h_ref)
        prevm_ref[...] = jnp.zeros_like(prevm_ref)
        prevl_ref[...] = jnp.zeros_like(prevl_ref)

    def shift_mix(z, prev_ref, mu_ref):
        row0 = lax.broadcasted_iota(jnp.int32, z.shape, 0) == 0
        zm1 = jnp.where(row0, prev_ref[SUBLANES - 1:SUBLANES, :], pltpu.roll(z, 1, 0))
        prev_ref[...] = z[rows - SUBLANES:, :]
        return z + (zm1 - z) * mu_ref[...]

    x = shift_mix(z_ref[0], prevm_ref, mum_ref)
    zl = shift_mix(zl_ref[0], prevl_ref, mul_ref)
    r_all, k_raw, v_all = x[:, :HALF], x[:, HALF:2 * HALF], x[:, 2 * HALF:]
    wa = zl[:, :LANES]
    w = w0_ref[...] + _dot(jnp.tanh(wa), w2_ref[...])
    w = -jax.nn.softplus(-w) - 0.5
    lw_all = -jnp.exp(w)
    a = jax.nn.sigmoid(a0_ref[...] + _dot(wa, a2_ref[...]))
    g_all = _dot(jax.nn.sigmoid(zl[:, LANES:]), g2_ref[...])
    k_all = k_raw * (1.0 + (a - 1.0) * ka_ref[...])

    row, col = _iota2((c2, c2))
    same = (row >> 6) == (col >> 6)
    strict = jnp.logical_and(same, row > col)
    incl = jnp.logical_and(same, row >= col)
    lane = lax.broadcasted_iota(jnp.int32, (c, LANES), 1)
    first_head = lane < RWKV_HEAD

    def stack2(x):
        return jnp.concatenate([jnp.where(first_head, x, 0.0), jnp.where(first_head, 0.0, x)], axis=1)

    npair = RWKV_PAIRS
    units = [(sub * c, p * LANES) for sub in range(nsub) for p in range(npair)]
    nu = len(units)
    per_pair = lambda x: jnp.stack([x[t0:t0 + c, l0:l0 + LANES] for t0, l0 in units])
    per_pair_vec = lambda x: jnp.stack([x[:, l0:l0 + LANES] for _, l0 in units])
    seg_sum = lambda x: _seg_sum(x.reshape(nu * c, LANES), RWKV_HEAD).reshape(nu, c, LANES)
    r, lw, k, v = per_pair(r_all), per_pair(lw_all), per_pair(k_all), per_pair(v_all)
    kk = per_pair(k_raw * kkp_ref[...])
    kk = kk * lax.rsqrt(seg_sum(kk * kk) + 1e-6)
    b = kk * per_pair(a)
    cl = per_pair(jnp.concatenate([_time_cumsum(lw_all[sub * c:(sub + 1) * c, :]) for sub in range(nsub)], axis=0))
    cl_last = cl[:, c - 1:c, :]
    e_neg = jnp.exp(-cl)
    e_tail = jnp.exp(cl_last - cl)
    kk2 = stack2(kk * jnp.exp(cl - lw))
    r2 = stack2(r * jnp.exp(cl))
    b2 = stack2(b * e_neg)
    k2 = stack2(k * e_neg)
    v2 = stack2(v)
    bd2 = stack2(b * e_tail)
    kd2 = stack2(k * e_tail)
    sc = _dot_nt(jnp.concatenate([kk2, r2], axis=1), jnp.concatenate([b2, k2], axis=1), prec)
    a_ab = jnp.where(strict, sc[:, :c2, :c2], 0.0)
    a_ak = jnp.where(strict, sc[:, :c2, c2:], 0.0)
    r_bk = jnp.concatenate([jnp.where(incl, sc[:, c2:, :c2], 0.0), jnp.where(incl, sc[:, c2:, c2:], 0.0)], axis=2)
    wt = _unit_lower_solve(a_ab, jnp.concatenate([kk2, _dot(a_ak, v2, prec)], axis=2), row, col, c, prec)
    wk, tv = wt[:, :, :LANES], wt[:, :, LANES:]
    wr = jnp.concatenate([wk, r2], axis=1)
    bkd2 = jnp.concatenate([bd2, kd2], axis=1)
    e_last = jnp.exp(cl_last)
    ht = h_ref[...]
    y2 = []
    for sub in range(nsub):
        b0, b1 = sub * npair, (sub + 1) * npair
        hp = _dot_nt(wr[b0:b1], ht, prec)
        u = -hp[:, :c2, :] - tv[b0:b1]
        uv = jnp.concatenate([u, v2[b0:b1]], axis=1)
        y2.append(hp[:, c2:, :] + _dot(r_bk[b0:b1], uv, prec))
        ht = ht * e_last[b0:b1] + _dot_tn(uv, bkd2[b0:b1], prec)
    h_ref[...] = ht
    y2 = jnp.concatenate(y2, axis=0)
    y = y2[:, :c, :] + y2[:, c:, :]
    mu = seg_sum(y) * (1.0 / RWKV_HEAD)
    d = y - mu
    var = seg_sum(d * d) * (1.0 / RWKV_HEAD)
    yn = d * lax.rsqrt(var + RWKV_GN_EPS) * per_pair_vec(lnw_ref[...]) + per_pair_vec(lnb_ref[...])
    out = (yn + seg_sum(r * k * per_pair_vec(rk_ref[...])) * v) * per_pair(g_all)
    for i, (t0, l0) in enumerate(units):
        o_ref[0, t0:t0 + c, l0:l0 + LANES] = out[i].astype(o_ref.dtype)


def _rwkv(z3, zl3, lora_blk, mu_main, mu_lora, w2p, a2p, g2p, w0, a0, k_k, k_a, r_k, ln_w, ln_b, prec):
    bsz, t, nz = z3.shape
    c = CHUNKS_PER_STEP * RWKV_CHUNK
    lw = RWKV_LORA_PAD
    vec = pl.BlockSpec((1, HALF), lambda bi, ni: (0, 0))
    full = lambda rows: pl.BlockSpec((rows, HALF), lambda bi, ni: (0, 0))
    row = lambda x: x.astype(F32).reshape(1, HALF)
    return pl.pallas_call(
        functools.partial(_rwkv_kernel, prec=prec),
        grid=(bsz, t // c),
        in_specs=[pl.BlockSpec((1, c, nz), lambda bi, ni: (bi, ni, 0)),
                  pl.BlockSpec((1, c, lw), lambda bi, ni: (bi, ni, lora_blk)),
                  pl.BlockSpec((1, nz), lambda bi, ni: (0, 0)),
                  pl.BlockSpec((1, lw), lambda bi, ni: (0, 0)),
                  full(LANES), full(LANES), full(lw - LANES)] + [vec] * 7,
        out_specs=pl.BlockSpec((1, c, HALF), lambda bi, ni: (bi, ni, 0)),
        out_shape=jax.ShapeDtypeStruct((bsz, t, HALF), BF16),
        scratch_shapes=[pltpu.VMEM((RWKV_PAIRS, LANES, LANES), F32), pltpu.VMEM((SUBLANES, nz), F32),
                        pltpu.VMEM((SUBLANES, lw), F32)],
        compiler_params=_cparams("parallel", "arbitrary"),
        name="rwkv_chunk",
    )(z3, zl3, mu_main, mu_lora, w2p, a2p, g2p, row(w0), row(a0), row(k_k), row(k_a), row(r_k), row(ln_w),
      row(ln_b))


def _rope_kernel(f_ref, cos_ref, sin_ref):
    t = cos_ref.shape[0]
    pos = lax.broadcasted_iota(jnp.int32, (t, LANES), 0).astype(F32)
    lane = lax.broadcasted_iota(jnp.int32, (t, LANES), 1)
    ang = pos * f_ref[...]
    cos_ref[...] = jnp.cos(ang)
    sin_ref[...] = jnp.where(lane < RET_DK // 2, -1.0, 1.0) * jnp.sin(ang)


def _rope_tables(t):
    inv_freq = ROPE_BASE ** (-jnp.linspace(0.0, 1.0, RET_DK // 2, dtype=F32))
    f2 = jnp.concatenate([inv_freq, inv_freq]).reshape(1, RET_DK)
    shape = jax.ShapeDtypeStruct((t, RET_DK), F32)
    return pl.pallas_call(_rope_kernel, out_shape=[shape, shape], name="rope_tables")(f2)


def _ret_kernel(q_ref, k_ref, v_ref, gate_ref, cos_ref, sin_ref, o_ref, s_ref):
    c = RET_CHUNK

    @pl.when(pl.program_id(1) == 0)
    def _():
        s_ref[...] = jnp.zeros_like(s_ref)

    row, col = _iota2((c, c))
    dist = (row - col).astype(F32)
    idx = lax.broadcasted_iota(jnp.int32, (c, 1), 0).astype(F32)
    cos, sin = cos_ref[...], sin_ref[...]
    rot = lambda x: x * cos + pltpu.roll(x, RET_DK // 2, 1) * sin

    for h in range(RET_HEADS):
        log_g = math.log(1.0 - 2.0 ** (-5.0 - h))
        q = rot(q_ref[0, :, h * RET_DK:(h + 1) * RET_DK])
        k = rot(k_ref[0, :, h * RET_DK:(h + 1) * RET_DK]) * (RET_DK ** -0.5)
        v = v_ref[0, :, h * RET_DV:(h + 1) * RET_DV]
        vb = v.astype(BF16)
        dmask = jnp.where(row >= col, jnp.exp(log_g * dist), 0.0)
        sc = _dot_nt(q.astype(BF16), k.astype(BF16)) * dmask
        s = s_ref[h]
        o = _dot(sc.astype(BF16), vb) + _dot((q * jnp.exp(log_g * (idx + 1.0))).astype(BF16), s.astype(BF16))
        kd = k * jnp.exp(log_g * (c - 1.0 - idx))
        s_ref[h] = s * math.exp(log_g * c) + _dot_tn(kd.astype(BF16), vb)
        o = o * lax.rsqrt(jnp.mean(o * o, axis=-1, keepdims=True) + EPS)
        gate = gate_ref[0, :, h * RET_DV:(h + 1) * RET_DV]
        o_ref[0, :, h * RET_DV:(h + 1) * RET_DV] = (o * gate * jax.nn.sigmoid(gate)).astype(o_ref.dtype)


def _retention(z3, q_blk, cos, sin):
    b, t, _ = z3.shape
    c = RET_CHUNK
    qk = RET_HEADS * RET_DK
    v_blk = (q_blk * qk + 2 * qk) // HALF
    return pl.pallas_call(
        _ret_kernel,
        grid=(b, t // c),
        in_specs=[
            pl.BlockSpec((1, c, qk), lambda bi, ni: (bi, ni, q_blk)),
            pl.BlockSpec((1, c, qk), lambda bi, ni: (bi, ni, q_blk + 1)),
            pl.BlockSpec((1, c, HALF), lambda bi, ni: (bi, ni, v_blk)),
            pl.BlockSpec((1, c, HALF), lambda bi, ni: (bi, ni, v_blk + 1)),
            pl.BlockSpec((c, RET_DK), lambda bi, ni: (ni, 0)),
            pl.BlockSpec((c, RET_DK), lambda bi, ni: (ni, 0)),
        ],
        out_specs=pl.BlockSpec((1, c, HALF), lambda bi, ni: (bi, ni, 0)),
        out_shape=jax.ShapeDtypeStruct((b, t, HALF), BF16),
        scratch_shapes=[pltpu.VMEM((RET_HEADS, RET_DK, RET_DV), F32)],
        compiler_params=_cparams("parallel", "arbitrary"),
        name="retention_chunk",
    )(z3, z3, z3, z3, cos, sin)


def _even_mixer(hn, bsz, seq, idx, w_in, w_out, conv_w, a_log, dt_bias, norm_w, lam_re, lam_im, b_re, b_im,
                c_re, c_im, d_skip, log_step, w_glu, res, prec):
    m = bsz * seq
    n_main = 4 * HALF
    n_small = 2 * GDN_HEADS
    w_b = jnp.concatenate([w_in[idx, n_main + n_small:, :],
                           jnp.pad(w_in[idx, n_main:n_main + n_small, :], ((0, LANES - n_small), (0, 0)))], axis=0)
    za = _matmul([(hn, w_in, idx, 0)], n=n_main, w_transposed=True).reshape(bsz, seq, n_main)
    zb = _matmul([(hn, w_b[None], 0, 0)], w_transposed=True).reshape(bsz, seq, HALF + LANES)
    ya = _gdn(za, zb, HALF // LANES, conv_w, a_log, dt_bias, norm_w, prec)
    p_re, p_im, bb_re, bb_im = _s5_params(lam_re, lam_im, log_step, b_re, b_im, seq // SUBLANES)
    bd_re, bd_im, cd_re, cd_im = _s5_block_diag(bb_re, bb_im, c_re, c_im)
    yg = _s5_scan(zb, 0, bd_re, bd_im, cd_re, cd_im, p_re, p_im, d_skip)
    yb = _glu(yg.reshape(m, HALF), w_glu.astype(BF16))
    return _matmul([(ya.reshape(m, HALF), w_out, idx, 0), (yb, w_out, idx, 1)], res=res)


def _odd_mixer(hn, bsz, seq, idx, w_in, w_out, shift_mu, w0, w2, a0, a2, g2, k_k, k_a, r_k, ln_w, ln_b, res, prec):
    m = bsz * seq
    n_main = 3 * HALF
    n_lora = 64 + 64 + 160
    w_b = jnp.concatenate([w_in[idx, n_main + n_lora:, :],
                           jnp.pad(w_in[idx, n_main:n_main + n_lora, :], ((0, RWKV_LORA_PAD - n_lora), (0, 0)))],
                          axis=0)
    za = _matmul([(hn, w_in, idx, 0)], n=n_main, w_transposed=True).reshape(bsz, seq, n_main)
    zb = _matmul([(hn, w_b[None], 0, 0)], w_transposed=True).reshape(bsz, seq, n_main + RWKV_LORA_PAD)
    mu_main = shift_mu[:n_main].astype(F32).reshape(1, n_main)
    mu_lora = jnp.pad(shift_mu[n_main:], (0, RWKV_LORA_PAD - n_lora)).astype(F32).reshape(1, RWKV_LORA_PAD)
    w2p = jnp.pad(w2, ((0, LANES - 64), (0, 0))).astype(BF16)
    a2p = jnp.pad(a2, ((64, 0), (0, 0))).astype(BF16)
    g2p = jnp.pad(g2, ((0, RWKV_LORA_PAD - LANES - 160), (0, 0))).astype(BF16)
    yc = _rwkv(za, zb, n_main // RWKV_LORA_PAD, mu_main, mu_lora, w2p, a2p, g2p, w0, a0, k_k, k_a,
               r_k, ln_w, ln_b, prec)
    cos, sin = _rope_tables(seq)
    yd = _retention(zb, 0, cos, sin)
    return _matmul([(yc.reshape(m, HALF), w_out, idx, 0), (yd.reshape(m, HALF), w_out, idx, 1)], res=res)


def kernel(x, norm_mix, norm_ffn, norm_final, ev_w_in, ev_w_out, gdn_conv_w, gdn_a_log, gdn_dt_bias, gdn_norm_w, s5_lam_re, s5_lam_im, s5_b_re, s5_b_im, s5_c_re, s5_c_im, s5_d, s5_log_step, s5_w_glu, od_w_in, od_w_out, rwkv_shift_mu, rwkv_w0, rwkv_w2, rwkv_a0, rwkv_a2, rwkv_g2, rwkv_k_k, rwkv_k_a, rwkv_r_k, rwkv_ln_w, rwkv_ln_b, ffn_w_up, ffn_conv_w, ffn_w_down):
    bsz, seq, d = x.shape
    m = bsz * seq
    depth = norm_mix.shape[0]
    prec = None
    w_down = ffn_w_down.astype(BF16)
    ev_w_in_t = jnp.swapaxes(ev_w_in, 1, 2)
    od_w_in_t = jnp.swapaxes(od_w_in, 1, 2)
    h = x.reshape(m, d).astype(F32)
    for layer in range(depth):
        hn = _rmsnorm(h, norm_mix[layer], BF16)
        i = layer // 2
        if layer % 2 == 0:
            h = _even_mixer(hn, bsz, seq, i, ev_w_in_t, ev_w_out, gdn_conv_w[i], gdn_a_log[i], gdn_dt_bias[i],
                            gdn_norm_w[i], s5_lam_re[i], s5_lam_im[i], s5_b_re[i], s5_b_im[i], s5_c_re[i],
                            s5_c_im[i], s5_d[i], s5_log_step[i], s5_w_glu[i], h, prec)
        else:
            h = _odd_mixer(hn, bsz, seq, i, od_w_in_t, od_w_out, rwkv_shift_mu[i], rwkv_w0[i], rwkv_w2[i],
                           rwkv_a0[i], rwkv_a2[i], rwkv_g2[i], rwkv_k_k[i], rwkv_k_a[i], rwkv_r_k[i],
                           rwkv_ln_w[i], rwkv_ln_b[i], h, prec)
        hn = _rmsnorm(h, norm_ffn[layer], BF16)
        act = _ffn_up(hn, ffn_w_up, ffn_conv_w, layer, seq)
        h = _matmul_rows(act, w_down, layer, h)
    return _rmsnorm(h, norm_final, x.dtype).reshape(bsz, seq, d)
```

```python
import functools
import math

import jax
import jax.numpy as jnp
from jax import lax
from jax.experimental import pallas as pl
from jax.experimental.pallas import tpu as pltpu

F32 = jnp.float32
BF16 = jnp.bfloat16

V7X_VMEM_LIMIT_BYTES = 56 * 1024 * 1024
LANES = 128
SUBLANES = 8

EPS = 1e-6
HALF = 1024
GDN_HEADS = 8
GDN_D = 128
GDN_CONV = 4
GDN_CHUNK = 64
S5_GROUP = 16
S5_GROUPS = 64
S5_STATE = 64
S5_TILE_GROUPS = LANES // S5_GROUP
S5_TILE_STATES = S5_TILE_GROUPS * S5_STATE
RWKV_HEAD = 64
RWKV_PAIRS = HALF // LANES
RWKV_CHUNK = 64
RWKV_LORA_PAD = 384
RWKV_GN_EPS = 64e-5
RET_HEADS = 4
RET_DK = 128
RET_DV = 256
RET_CHUNK = 256
ROPE_BASE = 10000.0
INV_BLOCK = 16
GDN_CHUNKS_PER_STEP = 4
RWKV_CHUNKS_PER_STEP = 4


def _cparams(*sem):
    return pltpu.CompilerParams(dimension_semantics=sem, vmem_limit_bytes=V7X_VMEM_LIMIT_BYTES)


def _dot_dims(a, b, dims, precision):
    if precision is None:
        a, b = a.astype(BF16), b.astype(BF16)
    batch = ((), ())
    if a.ndim == 3:
        dims = tuple(tuple(d + 1 for d in side) for side in dims)
        batch = ((0,), (0,))
    return lax.dot_general(a, b, (dims, batch), preferred_element_type=F32, precision=precision)


def _dot(a, b, precision=None):
    return _dot_dims(a, b, ((1,), (0,)), precision)


def _dot_nt(a, b, precision=None):
    return _dot_dims(a, b, ((1,), (1,)), precision)


def _dot_tn(a, b, precision=None):
    return _dot_dims(a, b, ((0,), (0,)), precision)


def _time_cumsum(x):
    row = lax.broadcasted_iota(jnp.int32, x.shape, 0)
    d = 1
    while d < x.shape[0]:
        x = x + jnp.where(row >= d, pltpu.roll(x, d, 0), 0.0)
        d *= 2
    return x


def _iota2(shape):
    return lax.broadcasted_iota(jnp.int32, shape, 0), lax.broadcasted_iota(jnp.int32, shape, 1)


def _unit_lower_solve(a, rhs, row, col, sub, precision):
    assert sub == 4 * INV_BLOCK
    shift = INV_BLOCK.bit_length() - 1
    eye = jnp.where(row == col, 1.0, 0.0).astype(F32)
    ad = jnp.where((row >> shift) == (col >> shift), a, 0.0)
    ao = a - ad
    a2 = _dot(ad, ad, precision)
    a4 = _dot(a2, a2, precision)
    t2 = _dot(eye - ad, eye + a2, precision)
    a8 = _dot(a4, a4, precision)
    t4 = _dot(t2, eye + a4, precision)
    td = _dot(t4, eye + a8, precision)
    n = _dot(td, ao, precision)
    r = _dot(td, rhs, precision)
    n2 = _dot(n, n, precision)
    return _dot(_dot(eye - n, eye + n2, precision), r, precision)


def _rmsnorm_kernel(x_ref, w_ref, o_ref):
    x = x_ref[...]
    ms = jnp.mean(x * x, axis=-1, keepdims=True)
    o_ref[...] = (x * lax.rsqrt(ms + EPS) * w_ref[...]).astype(o_ref.dtype)


def _rmsnorm(x, w, out_dtype):
    m, d = x.shape
    tm = min(512, m)
    return pl.pallas_call(
        _rmsnorm_kernel,
        grid=(m // tm,),
        in_specs=[pl.BlockSpec((tm, d), lambda i: (i, 0)), pl.BlockSpec((1, d), lambda i: (0, 0))],
        out_specs=pl.BlockSpec((tm, d), lambda i: (i, 0)),
        out_shape=jax.ShapeDtypeStruct((m, d), out_dtype),
        compiler_params=_cparams("parallel"),
        name="rmsnorm",
    )(x, w.reshape(1, d).astype(F32))


def _mm_kernel(*refs, n_pairs, has_res, w_transposed):
    n_in = 2 * n_pairs + int(has_res)
    o_ref = refs[n_in]
    wb_refs = refs[n_in + 1:]

    @pl.when(pl.program_id(1) == 0)
    def _():
        for p in range(n_pairs):
            w = refs[2 * p + 1][...]
            wb_refs[p][...] = (w.T if w_transposed else w).astype(BF16)

    acc = None
    for p in range(n_pairs):
        d = _dot(refs[2 * p][...], wb_refs[p][...])
        acc = d if acc is None else acc + d
    if has_res:
        acc = acc + refs[2 * n_pairs][...]
    o_ref[...] = acc.astype(o_ref.dtype)


def _pick_tile(n, prefs):
    for t in prefs:
        if n % t == 0:
            return t
    return n


def _matmul(pairs, n=None, res=None, out_dtype=F32, tm=1024, tn=None, w_transposed=False):
    m = pairs[0][0].shape[0]
    n = n or pairs[0][1].shape[1 if w_transposed else 2]
    tm = min(tm, m)
    tn = tn or _pick_tile(n, (1024, 1152, 512, 384, 256, 128))
    in_specs, args, scratch = [], [], []
    for a, w, l, r in pairs:
        k = a.shape[1]
        wspec = (pl.BlockSpec((None, tn, k), lambda j, i, l=l, r=r: (l, j, r)) if w_transposed else
                 pl.BlockSpec((None, k, tn), lambda j, i, l=l, r=r: (l, r, j)))
        in_specs += [pl.BlockSpec((tm, k), lambda j, i: (i, 0)), wspec]
        args += [a, w]
        scratch.append(pltpu.VMEM((k, tn), BF16))
    if res is not None:
        in_specs.append(pl.BlockSpec((tm, tn), lambda j, i: (i, j)))
        args.append(res)
    return pl.pallas_call(
        functools.partial(_mm_kernel, n_pairs=len(pairs), has_res=res is not None, w_transposed=w_transposed),
        grid=(n // tn, m // tm),
        in_specs=in_specs,
        out_specs=pl.BlockSpec((tm, tn), lambda j, i: (i, j)),
        out_shape=jax.ShapeDtypeStruct((m, n), out_dtype),
        scratch_shapes=scratch,
        compiler_params=_cparams("parallel", "arbitrary"),
        name="matmul",
    )(*args)


def _mm_rows_kernel(a_ref, w_ref, res_ref, o_ref):
    o_ref[...] = (_dot(a_ref[...], w_ref[...]) + res_ref[...]).astype(o_ref.dtype)


def _matmul_rows(a, w, layer, res, tm=1024, tn=512):
    m, k = a.shape
    n = w.shape[2]
    tm = min(tm, m)
    return pl.pallas_call(
        _mm_rows_kernel,
        grid=(m // tm, n // tn),
        in_specs=[pl.BlockSpec((tm, k), lambda i, j: (i, 0)),
                  pl.BlockSpec((None, k, tn), lambda i, j: (layer, 0, j)),
                  pl.BlockSpec((tm, tn), lambda i, j: (i, j))],
        out_specs=pl.BlockSpec((tm, tn), lambda i, j: (i, j)),
        out_shape=jax.ShapeDtypeStruct((m, n), res.dtype),
        compiler_params=_cparams("parallel", "parallel"),
        name="matmul_rows",
    )(a, w, res)


def _ffn_up_kernel(h_ref, wg_ref, wv_ref, cg_ref, cv_ref, o_ref, carry_ref, wgb_ref, wvb_ref, *, blocks_per_seq):
    i = pl.program_id(1)

    @pl.when(i == 0)
    def _():
        wgb_ref[...] = wg_ref[...].astype(BF16)
        wvb_ref[...] = wv_ref[...].astype(BF16)

    @pl.when(i % blocks_per_seq == 0)
    def _():
        carry_ref[...] = jnp.zeros_like(carry_ref)

    h = h_ref[...]
    zg = _dot(h, wgb_ref[...])
    zv = _dot(h, wvb_ref[...])
    tm, tn = zg.shape
    row8 = lax.broadcasted_iota(jnp.int32, (SUBLANES, tn), 0)

    def conv(z, c_ref, prev):
        m1 = pltpu.roll(z, 1, 0)
        m2 = pltpu.roll(z, 2, 0)
        h1 = jnp.where(row8 == 0, prev[7:8, :], m1[:SUBLANES, :])
        h2 = jnp.where(row8 == 0, prev[6:7, :], jnp.where(row8 == 1, prev[7:8, :], m2[:SUBLANES, :]))
        m1 = jnp.concatenate([h1, m1[SUBLANES:, :]], axis=0)
        m2 = jnp.concatenate([h2, m2[SUBLANES:, :]], axis=0)
        c = c_ref[...]
        return c[0:1, :] * m2 + c[1:2, :] * m1 + c[2:3, :] * z

    g = conv(zg, cg_ref, carry_ref[0])
    v = conv(zv, cv_ref, carry_ref[1])
    carry_ref[0] = zg[tm - SUBLANES:, :]
    carry_ref[1] = zv[tm - SUBLANES:, :]
    o_ref[...] = (g * jax.nn.sigmoid(g) * v).astype(o_ref.dtype)


def _ffn_up(hn, w_up, conv_w, layer, seq):
    m, d = hn.shape
    f = w_up.shape[2] // 2
    tm = min(1024, seq)
    tn = _pick_tile(f, (512, 256, 128))
    nj = f // tn
    return pl.pallas_call(
        functools.partial(_ffn_up_kernel, blocks_per_seq=seq // tm),
        grid=(nj, m // tm),
        in_specs=[
            pl.BlockSpec((tm, d), lambda j, i: (i, 0)),
            pl.BlockSpec((None, d, tn), lambda j, i: (layer, 0, j)),
            pl.BlockSpec((None, d, tn), lambda j, i: (layer, 0, j + nj)),
            pl.BlockSpec((None, 3, tn), lambda j, i: (layer, 0, j)),
            pl.BlockSpec((None, 3, tn), lambda j, i: (layer, 0, j + nj)),
        ],
        out_specs=pl.BlockSpec((tm, tn), lambda j, i: (i, j)),
        out_shape=jax.ShapeDtypeStruct((m, f), BF16),
        scratch_shapes=[pltpu.VMEM((2, SUBLANES, tn), F32), pltpu.VMEM((d, tn), BF16), pltpu.VMEM((d, tn), BF16)],
        compiler_params=_cparams("parallel", "arbitrary"),
        name="ffn_up_conv",
    )(hn, w_up, w_up, conv_w, conv_w)


def _gdn_kernel(z_ref, zs_ref, cw_ref, alog_ref, dtb_ref, nw_ref, o_ref, s_ref, prev_ref, *, prec):
    c = GDN_CHUNK
    nqkv = 3 * HALF
    rows = z_ref.shape[1]
    nsub = rows // c

    @pl.when(pl.program_id(1) == 0)
    def _():
        s_ref[...] = jnp.zeros_like(s_ref)
        prev_ref[...] = jnp.zeros_like(prev_ref)

    z = z_ref[0, :, :nqkv]
    prev = prev_ref[...]
    cw = cw_ref[...]
    row8 = lax.broadcasted_iota(jnp.int32, (SUBLANES, nqkv), 0)
    acc = cw[GDN_CONV - 1:GDN_CONV, :] * z
    for tap in range(1, GDN_CONV):
        zr = pltpu.roll(z, tap, 0)
        head = jnp.where(row8 < tap, pltpu.roll(prev, tap, 0), zr[:SUBLANES, :])
        acc = acc + cw[GDN_CONV - 1 - tap:GDN_CONV - tap, :] * jnp.concatenate([head, zr[SUBLANES:, :]], axis=0)
    prev_ref[...] = z[rows - SUBLANES:, :]
    x = acc * jax.nn.sigmoid(acc)

    zs = zs_ref[0]
    beta_all = jax.nn.sigmoid(zs)
    g_all = -jnp.exp(alog_ref[...]) * jax.nn.softplus(zs + dtb_ref[...])
    row, col = _iota2((c, c))
    nh = GDN_HEADS
    units = [(sub * c, h) for sub in range(nsub) for h in range(nh)]
    per_unit = lambda off: jnp.stack([x[t0:t0 + c, off + h * GDN_D:off + (h + 1) * GDN_D] for t0, h in units])
    l2n = lambda t: t * lax.rsqrt(jnp.sum(t * t, axis=-1, keepdims=True) + 1e-6)
    q = l2n(per_unit(0)) * (GDN_D ** -0.5)
    k = l2n(per_unit(HALF))
    v = per_unit(2 * HALF)
    beta = jnp.stack([beta_all[t0:t0 + c, h:h + 1] for t0, h in units])
    gcum = [_time_cumsum(g_all[sub * c:(sub + 1) * c, :]) for sub in range(nsub)]
    gcum_t = [jnp.concatenate([gs, jnp.zeros((LANES - c, LANES), F32)], axis=0).T for gs in gcum]
    gc_col = [gcum[t0 // c][:, nh + h:nh + h + 1] for t0, h in units]
    diff = jnp.stack([gc_col[i] - gcum_t[t0 // c][nh + h:nh + h + 1, :c] for i, (t0, h) in enumerate(units)])
    decay = jnp.where(row >= col, jnp.exp(diff), 0.0)
    gc = jnp.stack([jnp.broadcast_to(gcol, (c, GDN_D)) for gcol in gc_col])
    gl = gc[:, c - 1:c, :]
    egc = jnp.exp(gc)
    kb = k * beta
    kk = _dot_nt(jnp.concatenate([kb, q], axis=1), k, prec)
    a = jnp.where(row > col, kk[:, :c, :] * decay, 0.0)
    attn = kk[:, c:, :] * decay
    sol = _unit_lower_solve(a, jnp.concatenate([v * beta, kb * egc], axis=2), row, col, c, prec)
    u, w = sol[:, :, :GDN_D], sol[:, :, GDN_D:]
    qa = jnp.concatenate([q * egc, attn], axis=2)
    kdec = k * jnp.exp(gl - gc)
    egl = jnp.exp(gl)
    s = s_ref[...]
    for sub in range(nsub):
        b0, b1 = sub * nh, (sub + 1) * nh
        v_new = u[b0:b1] - _dot(w[b0:b1], s, prec)
        o = _dot(qa[b0:b1], jnp.concatenate([s, v_new], axis=1), prec)
        s = s * egl[b0:b1] + _dot_tn(kdec[b0:b1], v_new, prec)
        o = o * lax.rsqrt(jnp.mean(o * o, axis=-1, keepdims=True) + EPS) * nw_ref[...]
        for h in range(nh):
            gate = z_ref[0, sub * c:(sub + 1) * c, nqkv + h * GDN_D:nqkv + (h + 1) * GDN_D]
            o_ref[0, sub * c:(sub + 1) * c, h * GDN_D:(h + 1) * GDN_D] = (
                o[h] * gate * jax.nn.sigmoid(gate)).astype(o_ref.dtype)
    s_ref[...] = s


def _gdn(z3, zs3, zs_blk, conv_w, a_log, dt_bias, norm_w, prec):
    b, t, nz = z3.shape
    c = GDN_CHUNKS_PER_STEP * GDN_CHUNK
    pad = lambda p: jnp.zeros((1, LANES), F32).at[0, GDN_HEADS:2 * GDN_HEADS].set(p.astype(F32))
    vec = pl.BlockSpec((1, LANES), lambda bi, ni: (0, 0))
    return pl.pallas_call(
        functools.partial(_gdn_kernel, prec=prec),
        grid=(b, t // c),
        in_specs=[pl.BlockSpec((1, c, nz), lambda bi, ni: (bi, ni, 0)),
                  pl.BlockSpec((1, c, LANES), lambda bi, ni: (bi, ni, zs_blk)),
                  pl.BlockSpec((GDN_CONV, 3 * HALF), lambda bi, ni: (0, 0)),
                  vec, vec, vec],
        out_specs=pl.BlockSpec((1, c, HALF), lambda bi, ni: (bi, ni, 0)),
        out_shape=jax.ShapeDtypeStruct((b, t, HALF), BF16),
        scratch_shapes=[pltpu.VMEM((GDN_HEADS, GDN_D, GDN_D), F32), pltpu.VMEM((SUBLANES, 3 * HALF), F32)],
        compiler_params=_cparams("parallel", "arbitrary"),
        name="gdn_chunk",
    )(z3, zs3, conv_w.astype(F32), pad(a_log), pad(dt_bias), norm_w.reshape(1, GDN_D).astype(F32))


def _s5_param_kernel(lr_ref, li_ref, dt_ref, br_ref, bi_ref, ar_ref, ai_ref, bbr_ref, bbi_ref):
    lr, li, dt = lr_ref[...], li_ref[...], dt_ref[...]
    step = jnp.exp(dt)
    mag = jnp.exp(lr * step)
    ang = li * step
    ab_re, ab_im = mag * jnp.cos(ang), mag * jnp.sin(ang)
    den = lr * lr + li * li
    nr = ab_re - 1.0
    f_re = (nr * lr + ab_im * li) / den
    f_im = (ab_im * lr - nr * li) / den
    br, bi = br_ref[...], bi_ref[...]
    bbr_ref[...] = f_re * br - f_im * bi
    bbi_ref[...] = f_re * bi + f_im * br
    n = ab_re.shape[1]
    cmul = lambda xr, xi, yr, yi: (xr * yr - xi * yi, xr * yi + xi * yr)
    row = lax.broadcasted_iota(jnp.int32, (SUBLANES, n), 0)
    cur = (ab_re, ab_im)
    pr = jnp.broadcast_to(ab_re, (SUBLANES, n))
    pi = jnp.broadcast_to(ab_im, (SUBLANES, n))
    for r in range(1, SUBLANES):
        cur = cmul(cur[0], cur[1], ab_re, ab_im)
        pr = jnp.where(row == r, cur[0], pr)
        pi = jnp.where(row == r, cur[1], pi)
    rows = SUBLANES
    while rows < ar_ref.shape[0]:
        tr, ti = cmul(pr, pi, pr[rows - 1:rows, :], pi[rows - 1:rows, :])
        pr = jnp.concatenate([pr, tr], axis=0)
        pi = jnp.concatenate([pi, ti], axis=0)
        rows *= 2
    ar_ref[...] = pr
    ai_ref[...] = pi


def _s5_params(lam_re, lam_im, log_step, b_re, b_im, n_pow):
    assert n_pow >= SUBLANES and n_pow & (n_pow - 1) == 0
    gp = S5_GROUPS * S5_STATE
    row = lambda x: x.astype(F32).reshape(1, gp)
    bt = lambda x: jnp.transpose(x.astype(F32), (2, 0, 1)).reshape(S5_GROUP, gp)
    pshape = jax.ShapeDtypeStruct((n_pow, gp), F32)
    mshape = jax.ShapeDtypeStruct((S5_GROUP, gp), F32)
    return pl.pallas_call(
        _s5_param_kernel,
        out_shape=[pshape, pshape, mshape, mshape],
        name="s5_params",
    )(row(lam_re), row(lam_im), row(jnp.repeat(log_step[:, None], S5_STATE, axis=1)), bt(b_re), bt(b_im))


def _s5_scan_kernel(u_ref, bdr_ref, bdi_ref, cdr_ref, cdi_ref, pr_ref, pi_ref, d_ref, o_ref,
                    up_ref, xr_ref, xi_ref, y_ref):
    t = u_ref.shape[1]
    n = S5_TILE_STATES
    nk = t // SUBLANES
    rt = min(512, t)

    def permute_in(k, _):
        r0 = pl.multiple_of(k * SUBLANES, SUBLANES)
        up_ref[pl.ds(r0, SUBLANES), :] = u_ref[0, pl.ds(k, SUBLANES, stride=nk), :]
        return 0

    lax.fori_loop(0, nk, permute_in, 0, unroll=SUBLANES)

    for r0 in range(0, t, rt):
        ub = up_ref[r0:r0 + rt, :].astype(BF16)
        xr_ref[r0:r0 + rt, :] = _dot(ub, bdr_ref[0])
        xi_ref[r0:r0 + rt, :] = _dot(ub, bdi_ref[0])

    ar = jnp.broadcast_to(pr_ref[0:1, :], (SUBLANES, n))
    ai = jnp.broadcast_to(pi_ref[0:1, :], (SUBLANES, n))

    def local_scan(k, carry):
        cr, ci = carry
        r0 = pl.multiple_of(k * SUBLANES, SUBLANES)
        xr = xr_ref[pl.ds(r0, SUBLANES), :] + (ar * cr - ai * ci)
        xi = xi_ref[pl.ds(r0, SUBLANES), :] + (ar * ci + ai * cr)
        xr_ref[pl.ds(r0, SUBLANES), :] = xr
        xi_ref[pl.ds(r0, SUBLANES), :] = xi
        return xr, xi

    zero = jnp.zeros((SUBLANES, n), F32)
    fr, fi = lax.fori_loop(0, nk, local_scan, (zero, zero), unroll=SUBLANES)

    row = lax.broadcasted_iota(jnp.int32, (SUBLANES, n), 0)
    cmul = lambda xr, xi, yr, yi: (xr * yr - xi * yi, xr * yi + xi * yr)
    gr = jnp.broadcast_to(pr_ref[nk - 1:nk, :], (SUBLANES, n))
    gi = jnp.broadcast_to(pi_ref[nk - 1:nk, :], (SUBLANES, n))
    for d in (1, 2, 4):
        sr = jnp.where(row >= d, pltpu.roll(fr, d, 0), 0.0)
        si = jnp.where(row >= d, pltpu.roll(fi, d, 0), 0.0)
        tr, ti = cmul(gr, gi, sr, si)
        fr, fi = fr + tr, fi + ti
        gr, gi = cmul(gr, gi, gr, gi)
    cr = jnp.where(row >= 1, pltpu.roll(fr, 1, 0), 0.0)
    ci = jnp.where(row >= 1, pltpu.roll(fi, 1, 0), 0.0)

    def add_carry(k8, _):
        p0 = pl.multiple_of(k8 * SUBLANES, SUBLANES)
        pr8 = pr_ref[pl.ds(p0, SUBLANES), :]
        pi8 = pi_ref[pl.ds(p0, SUBLANES), :]
        for j in range(SUBLANES):
            r0 = pl.multiple_of((k8 * SUBLANES + j) * SUBLANES, SUBLANES)
            pr = jnp.broadcast_to(pr8[j:j + 1, :], (SUBLANES, n))
            pi = jnp.broadcast_to(pi8[j:j + 1, :], (SUBLANES, n))
            xr_ref[pl.ds(r0, SUBLANES), :] = xr_ref[pl.ds(r0, SUBLANES), :] + (pr * cr - pi * ci)
            xi_ref[pl.ds(r0, SUBLANES), :] = xi_ref[pl.ds(r0, SUBLANES), :] + (pr * ci + pi * cr)
        return 0

    lax.fori_loop(0, nk // SUBLANES, add_carry, 0)

    for r0 in range(0, t, rt):
        y = (_dot(xr_ref[r0:r0 + rt, :].astype(BF16), cdr_ref[0])
             - _dot(xi_ref[r0:r0 + rt, :].astype(BF16), cdi_ref[0]) + up_ref[r0:r0 + rt, :] * d_ref[...])
        y_ref[r0:r0 + rt, :] = jax.nn.gelu(y)

    def permute_out(k, _):
        r0 = pl.multiple_of(k * SUBLANES, SUBLANES)
        o_ref[0, pl.ds(k, SUBLANES, stride=nk), :] = y_ref[pl.ds(r0, SUBLANES), :]
        return 0

    lax.fori_loop(0, nk, permute_out, 0, unroll=SUBLANES)


def _s5_scan(z3, u_off, bd_re, bd_im, cd_re, cd_im, p_re, p_im, d_skip):
    b, t, _ = z3.shape
    nt = HALF // LANES
    n = S5_TILE_STATES
    nk = t // SUBLANES
    assert p_re.shape[0] == nk
    return pl.pallas_call(
        _s5_scan_kernel,
        grid=(b, nt),
        in_specs=[
            pl.BlockSpec((1, t, LANES), lambda bi, j: (bi, 0, j + u_off)),
            pl.BlockSpec((1, LANES, n), lambda bi, j: (j, 0, 0)),
            pl.BlockSpec((1, LANES, n), lambda bi, j: (j, 0, 0)),
            pl.BlockSpec((1, n, LANES), lambda bi, j: (j, 0, 0)),
            pl.BlockSpec((1, n, LANES), lambda bi, j: (j, 0, 0)),
            pl.BlockSpec((nk, n), lambda bi, j: (0, j)),
            pl.BlockSpec((nk, n), lambda bi, j: (0, j)),
            pl.BlockSpec((1, LANES), lambda bi, j: (0, j)),
        ],
        out_specs=pl.BlockSpec((1, t, LANES), lambda bi, j: (bi, 0, j)),
        out_shape=jax.ShapeDtypeStruct((b, t, HALF), F32),
        scratch_shapes=[pltpu.VMEM((t, LANES), F32), pltpu.VMEM((t, n), F32), pltpu.VMEM((t, n), F32),
                        pltpu.VMEM((t, LANES), F32)],
        compiler_params=_cparams("parallel", "parallel"),
        name="s5_scan",
    )(z3, bd_re, bd_im, cd_re, cd_im, p_re, p_im, d_skip.reshape(1, HALF).astype(F32))


def _glu_kernel(y_ref, w_ref, yt_ref, o_ref):
    gate = _dot(y_ref[...].astype(BF16), w_ref[...])
    o_ref[...] = (yt_ref[...] * jax.nn.sigmoid(gate)).astype(o_ref.dtype)


def _glu(y, w):
    m, k = y.shape
    tm = min(1024, m)
    tn = 512
    return pl.pallas_call(
        _glu_kernel,
        grid=(m // tm, k // tn),
        in_specs=[pl.BlockSpec((tm, k), lambda i, j: (i, 0)),
                  pl.BlockSpec((k, tn), lambda i, j: (0, j)),
                  pl.BlockSpec((tm, tn), lambda i, j: (i, j))],
        out_specs=pl.BlockSpec((tm, tn), lambda i, j: (i, j)),
        out_shape=jax.ShapeDtypeStruct((m, k), BF16),
        compiler_params=_cparams("parallel", "parallel"),
        name="s5_glu",
    )(y, w, y)


def _s5_block_diag(bb_re, bb_im, c_re, c_im):
    nt, tg = HALF // LANES, S5_TILE_GROUPS
    eye = jnp.eye(tg, dtype=F32)

    def bmap(bb):
        x = bb.reshape(S5_GROUP, nt, tg, S5_STATE)
        x = jnp.einsum('cjgp,gh->jgchp', x, eye)
        return x.reshape(nt, LANES, S5_TILE_STATES).astype(BF16)

    def cmap(cc):
        x = cc.astype(F32).reshape(nt, tg, S5_GROUP, S5_STATE)
        x = jnp.einsum('jgcp,gh->jgphc', x, eye)
        return x.reshape(nt, S5_TILE_STATES, LANES).astype(BF16)

    return bmap(bb_re), bmap(bb_im), cmap(c_re), cmap(c_im)


def _rwkv_kernel(z_ref, zl_ref, mum_ref, mul_ref, w2_ref, a2_ref, g2_ref, w0_ref, a0_ref, kkp_ref, ka_ref,
                 rk_ref, lnw_ref, lnb_ref, o_ref, h_ref, prevm_ref, prevl_ref, *, prec):
    c = RWKV_CHUNK
    c2 = 2 * c
    rows = z_ref.shape[1]
    nsub = rows // c

    @pl.when(pl.program_id(1) == 0)
    def _():
        h_ref[...] = jnp.zeros_like(h_ref)
        prevm_ref[...] = jnp.zeros_like(prevm_ref)
        prevl_ref[...] = jnp.zeros_like(prevl_ref)

    def shift_mix(z, prev_ref, mu_ref):
        row0 = lax.broadcasted_iota(jnp.int32, z.shape, 0) == 0
        zm1 = jnp.where(row0, prev_ref[SUBLANES - 1:SUBLANES, :], pltpu.roll(z, 1, 0))
        prev_ref[...] = z[rows - SUBLANES:, :]
        return z + (zm1 - z) * mu_ref[...]

    x = shift_mix(z_ref[0], prevm_ref, mum_ref)
    zl = shift_mix(zl_ref[0], prevl_ref, mul_ref)
    r_all, k_raw, v_all = x[:, :HALF], x[:, HALF:2 * HALF], x[:, 2 * HALF:]
    wa = zl[:, :LANES]
    w = w0_ref[...] + _dot(jnp.tanh(wa), w2_ref[...])
    w = -jax.nn.softplus(-w) - 0.5
    lw_all = -jnp.exp(w)
    a = jax.nn.sigmoid(a0_ref[...] + _dot(wa, a2_ref[...]))
    g_all = _dot(jax.nn.sigmoid(zl[:, LANES:]), g2_ref[...])
    k_all = k_raw * (1.0 + (a - 1.0) * ka_ref[...])

    row, col = _iota2((c2, c2))
    same = (row >> 6) == (col >> 6)
    strict = jnp.logical_and(same, row > col)
    incl = jnp.logical_and(same, row >= col)
    lane = lax.broadcasted_iota(jnp.int32, (c, LANES), 1)
    first_head = lane < RWKV_HEAD

    def stack2(x):
        return jnp.concatenate([jnp.where(first_head, x, 0.0), jnp.where(first_head, 0.0, x)], axis=1)

    npair = RWKV_PAIRS
    units = [(sub * c, p * LANES) for sub in range(nsub) for p in range(npair)]
    nu = len(units)
    per_pair = lambda x: jnp.stack([x[t0:t0 + c, l0:l0 + LANES] for t0, l0 in units])
    per_pair_vec = lambda x: jnp.stack([x[:, l0:l0 + LANES] for _, l0 in units])
    def seg_sum(x):
        sa = jnp.sum(jnp.where(first_head, x, 0.0), axis=-1, keepdims=True)
        sb = jnp.sum(jnp.where(first_head, 0.0, x), axis=-1, keepdims=True)
        return jnp.where(first_head, sa, sb)
    r, lw, k, v = per_pair(r_all), per_pair(lw_all), per_pair(k_all), per_pair(v_all)
    kk = per_pair(k_raw * kkp_ref[...])
    kk = kk * lax.rsqrt(seg_sum(kk * kk) + 1e-6)
    b = kk * per_pair(a)
    cl = per_pair(jnp.concatenate([_time_cumsum(lw_all[sub * c:(sub + 1) * c, :]) for sub in range(nsub)], axis=0))
    cl_last = cl[:, c - 1:c, :]
    e_neg = jnp.exp(-cl)
    e_tail = jnp.exp(cl_last - cl)
    kk2 = stack2(kk * jnp.exp(cl - lw))
    r2 = stack2(r * jnp.exp(cl))
    b2 = stack2(b * e_neg)
    k2 = stack2(k * e_neg)
    v2 = stack2(v)
    bd2 = stack2(b * e_tail)
    kd2 = stack2(k * e_tail)
    sc = _dot_nt(jnp.concatenate([kk2, r2], axis=1), jnp.concatenate([b2, k2], axis=1), prec)
    a_ab = jnp.where(strict, sc[:, :c2, :c2], 0.0)
    a_ak = jnp.where(strict, sc[:, :c2, c2:], 0.0)
    r_bk = jnp.concatenate([jnp.where(incl, sc[:, c2:, :c2], 0.0), jnp.where(incl, sc[:, c2:, c2:], 0.0)], axis=2)
    wt = _unit_lower_solve(a_ab, jnp.concatenate([kk2, _dot(a_ak, v2, prec)], axis=2), row, col, c, prec)
    wk, tv = wt[:, :, :LANES], wt[:, :, LANES:]
    wr = jnp.concatenate([wk, r2], axis=1)
    bkd2 = jnp.concatenate([bd2, kd2], axis=1)
    e_last = jnp.exp(cl_last)
    ht = h_ref[...]
    y2 = []
    for sub in range(nsub):
        b0, b1 = sub * npair, (sub + 1) * npair
        hp = _dot_nt(wr[b0:b1], ht, prec)
        u = -hp[:, :c2, :] - tv[b0:b1]
        uv = jnp.concatenate([u, v2[b0:b1]], axis=1)
        y2.append(hp[:, c2:, :] + _dot(r_bk[b0:b1], uv, prec))
        ht = ht * e_last[b0:b1] + _dot_tn(uv, bkd2[b0:b1], prec)
    h_ref[...] = ht
    y2 = jnp.concatenate(y2, axis=0)
    y = y2[:, :c, :] + y2[:, c:, :]
    mu = seg_sum(y) * (1.0 / RWKV_HEAD)
    d = y - mu
    var = seg_sum(d * d) * (1.0 / RWKV_HEAD)
    yn = d * lax.rsqrt(var + RWKV_GN_EPS) * per_pair_vec(lnw_ref[...]) + per_pair_vec(lnb_ref[...])
    out = (yn + seg_sum(r * k * per_pair_vec(rk_ref[...])) * v) * per_pair(g_all)
    for i, (t0, l0) in enumerate(units):
        o_ref[0, t0:t0 + c, l0:l0 + LANES] = out[i].astype(o_ref.dtype)


def _rwkv(z3, zl3, lora_blk, mu_main, mu_lora, w2p, a2p, g2p, w0, a0, k_k, k_a, r_k, ln_w, ln_b, prec):
    bsz, t, nz = z3.shape
    c = RWKV_CHUNKS_PER_STEP * RWKV_CHUNK
    lw = RWKV_LORA_PAD
    vec = pl.BlockSpec((1, HALF), lambda bi, ni: (0, 0))
    full = lambda rows: pl.BlockSpec((rows, HALF), lambda bi, ni: (0, 0))
    row = lambda x: x.astype(F32).reshape(1, HALF)
    return pl.pallas_call(
        functools.partial(_rwkv_kernel, prec=prec),
        grid=(bsz, t // c),
        in_specs=[pl.BlockSpec((1, c, nz), lambda bi, ni: (bi, ni, 0)),
                  pl.BlockSpec((1, c, lw), lambda bi, ni: (bi, ni, lora_blk)),
                  pl.BlockSpec((1, nz), lambda bi, ni: (0, 0)),
                  pl.BlockSpec((1, lw), lambda bi, ni: (0, 0)),
                  full(LANES), full(LANES), full(lw - LANES)] + [vec] * 7,
        out_specs=pl.BlockSpec((1, c, HALF), lambda bi, ni: (bi, ni, 0)),
        out_shape=jax.ShapeDtypeStruct((bsz, t, HALF), BF16),
        scratch_shapes=[pltpu.VMEM((RWKV_PAIRS, LANES, LANES), F32), pltpu.VMEM((SUBLANES, nz), F32),
                        pltpu.VMEM((SUBLANES, lw), F32)],
        compiler_params=_cparams("parallel", "arbitrary"),
        name="rwkv_chunk",
    )(z3, zl3, mu_main, mu_lora, w2p, a2p, g2p, row(w0), row(a0), row(k_k), row(k_a), row(r_k), row(ln_w),
      row(ln_b))


def _rope_kernel(f_ref, cos_ref, sin_ref):
    t = cos_ref.shape[0]
    pos = lax.broadcasted_iota(jnp.int32, (t, LANES), 0).astype(F32)
    lane = lax.broadcasted_iota(jnp.int32, (t, LANES), 1)
    ang = pos * f_ref[...]
    cos_ref[...] = jnp.cos(ang)
    sin_ref[...] = jnp.where(lane < RET_DK // 2, -1.0, 1.0) * jnp.sin(ang)


def _rope_tables(t):
    inv_freq = ROPE_BASE ** (-jnp.linspace(0.0, 1.0, RET_DK // 2, dtype=F32))
    f2 = jnp.concatenate([inv_freq, inv_freq]).reshape(1, RET_DK)
    shape = jax.ShapeDtypeStruct((t, RET_DK), F32)
    return pl.pallas_call(_rope_kernel, out_shape=[shape, shape], name="rope_tables")(f2)


def _ret_kernel(q_ref, k_ref, v_ref, gate_ref, cos_ref, sin_ref, o_ref, s_ref):
    c = RET_CHUNK

    @pl.when(pl.program_id(1) == 0)
    def _():
        s_ref[...] = jnp.zeros_like(s_ref)

    row, col = _iota2((c, c))
    dist = (row - col).astype(F32)
    idx = lax.broadcasted_iota(jnp.int32, (c, 1), 0).astype(F32)
    cos, sin = cos_ref[...], sin_ref[...]
    rot = lambda x: x * cos + pltpu.roll(x, RET_DK // 2, 1) * sin

    for h in range(RET_HEADS):
        log_g = math.log(1.0 - 2.0 ** (-5.0 - h))
        q = rot(q_ref[0, :, h * RET_DK:(h + 1) * RET_DK])
        k = rot(k_ref[0, :, h * RET_DK:(h + 1) * RET_DK]) * (RET_DK ** -0.5)
        v = v_ref[0, :, h * RET_DV:(h + 1) * RET_DV]
        vb = v.astype(BF16)
        dmask = jnp.where(row >= col, jnp.exp(log_g * dist), 0.0)
        sc = _dot_nt(q.astype(BF16), k.astype(BF16)) * dmask
        s = s_ref[h]
        o = _dot(sc.astype(BF16), vb) + _dot((q * jnp.exp(log_g * (idx + 1.0))).astype(BF16), s.astype(BF16))
        kd = k * jnp.exp(log_g * (c - 1.0 - idx))
        s_ref[h] = s * math.exp(log_g * c) + _dot_tn(kd.astype(BF16), vb)
        o = o * lax.rsqrt(jnp.mean(o * o, axis=-1, keepdims=True) + EPS)
        gate = gate_ref[0, :, h * RET_DV:(h + 1) * RET_DV]
        o_ref[0, :, h * RET_DV:(h + 1) * RET_DV] = (o * gate * jax.nn.sigmoid(gate)).astype(o_ref.dtype)


def _retention(z3, q_blk, cos, sin):
    b, t, _ = z3.shape
    c = RET_CHUNK
    qk = RET_HEADS * RET_DK
    v_blk = (q_blk * qk + 2 * qk) // HALF
    return pl.pallas_call(
        _ret_kernel,
        grid=(b, t // c),
        in_specs=[
            pl.BlockSpec((1, c, qk), lambda bi, ni: (bi, ni, q_blk)),
            pl.BlockSpec((1, c, qk), lambda bi, ni: (bi, ni, q_blk + 1)),
            pl.BlockSpec((1, c, HALF), lambda bi, ni: (bi, ni, v_blk)),
            pl.BlockSpec((1, c, HALF), lambda bi, ni: (bi, ni, v_blk + 1)),
            pl.BlockSpec((c, RET_DK), lambda bi, ni: (ni, 0)),
            pl.BlockSpec((c, RET_DK), lambda bi, ni: (ni, 0)),
        ],
        out_specs=pl.BlockSpec((1, c, HALF), lambda bi, ni: (bi, ni, 0)),
        out_shape=jax.ShapeDtypeStruct((b, t, HALF), BF16),
        scratch_shapes=[pltpu.VMEM((RET_HEADS, RET_DK, RET_DV), F32)],
        compiler_params=_cparams("parallel", "arbitrary"),
        name="retention_chunk",
    )(z3, z3, z3, z3, cos, sin)


def _even_mixer(hn, bsz, seq, idx, w_in, w_out, conv_w, a_log, dt_bias, norm_w, lam_re, lam_im, b_re, b_im,
                c_re, c_im, d_skip, log_step, w_glu, res, prec):
    m = bsz * seq
    n_main = 4 * HALF
    n_small = 2 * GDN_HEADS
    w_b = jnp.concatenate([w_in[idx, n_main + n_small:, :],
                           jnp.pad(w_in[idx, n_main:n_main + n_small, :], ((0, LANES - n_small), (0, 0)))], axis=0)
    za = _matmul([(hn, w_in, idx, 0)], n=n_main, w_transposed=True).reshape(bsz, seq, n_main)
    zb = _matmul([(hn, w_b[None], 0, 0)], w_transposed=True).reshape(bsz, seq, HALF + LANES)
    ya = _gdn(za, zb, HALF // LANES, conv_w, a_log, dt_bias, norm_w, prec)
    p_re, p_im, bb_re, bb_im = _s5_params(lam_re, lam_im, log_step, b_re, b_im, seq // SUBLANES)
    bd_re, bd_im, cd_re, cd_im = _s5_block_diag(bb_re, bb_im, c_re, c_im)
    yg = _s5_scan(zb, 0, bd_re, bd_im, cd_re, cd_im, p_re, p_im, d_skip)
    yb = _glu(yg.reshape(m, HALF), w_glu.astype(BF16))
    return _matmul([(ya.reshape(m, HALF), w_out, idx, 0), (yb, w_out, idx, 1)], res=res)


def _odd_mixer(hn, bsz, seq, idx, w_in, w_out, shift_mu, w0, w2, a0, a2, g2, k_k, k_a, r_k, ln_w, ln_b, res, prec):
    m = bsz * seq
    n_main = 3 * HALF
    n_lora = 64 + 64 + 160
    w_b = jnp.concatenate([w_in[idx, n_main + n_lora:, :],
                           jnp.pad(w_in[idx, n_main:n_main + n_lora, :], ((0, RWKV_LORA_PAD - n_lora), (0, 0)))],
                          axis=0)
    za = _matmul([(hn, w_in, idx, 0)], n=n_main, w_transposed=True).reshape(bsz, seq, n_main)
    zb = _matmul([(hn, w_b[None], 0, 0)], w_transposed=True).reshape(bsz, seq, n_main + RWKV_LORA_PAD)
    mu_main = shift_mu[:n_main].astype(F32).reshape(1, n_main)
    mu_lora = jnp.pad(shift_mu[n_main:], (0, RWKV_LORA_PAD - n_lora)).astype(F32).reshape(1, RWKV_LORA_PAD)
    w2p = jnp.pad(w2, ((0, LANES - 64), (0, 0))).astype(BF16)
    a2p = jnp.pad(a2, ((64, 0), (0, 0))).astype(BF16)
    g2p = jnp.pad(g2, ((0, RWKV_LORA_PAD - LANES - 160), (0, 0))).astype(BF16)
    yc = _rwkv(za, zb, n_main // RWKV_LORA_PAD, mu_main, mu_lora, w2p, a2p, g2p, w0, a0, k_k, k_a,
               r_k, ln_w, ln_b, prec)
    cos, sin = _rope_tables(seq)
    yd = _retention(zb, 0, cos, sin)
    return _matmul([(yc.reshape(m, HALF), w_out, idx, 0), (yd.reshape(m, HALF), w_out, idx, 1)], res=res)


def kernel(x, norm_mix, norm_ffn, norm_final, ev_w_in, ev_w_out, gdn_conv_w, gdn_a_log, gdn_dt_bias, gdn_norm_w, s5_lam_re, s5_lam_im, s5_b_re, s5_b_im, s5_c_re, s5_c_im, s5_d, s5_log_step, s5_w_glu, od_w_in, od_w_out, rwkv_shift_mu, rwkv_w0, rwkv_w2, rwkv_a0, rwkv_a2, rwkv_g2, rwkv_k_k, rwkv_k_a, rwkv_r_k, rwkv_ln_w, rwkv_ln_b, ffn_w_up, ffn_conv_w, ffn_w_down):
    bsz, seq, d = x.shape
    m = bsz * seq
    depth = norm_mix.shape[0]
    prec = None
    w_down = ffn_w_down.astype(BF16)
    ev_w_in_t = jnp.swapaxes(ev_w_in, 1, 2)
    od_w_in_t = jnp.swapaxes(od_w_in, 1, 2)
    h = x.reshape(m, d).astype(F32)
    for layer in range(depth):
        hn = _rmsnorm(h, norm_mix[layer], BF16)
        i = layer // 2
        if layer % 2 == 0:
            h = _even_mixer(hn, bsz, seq, i, ev_w_in_t, ev_w_out, gdn_conv_w[i], gdn_a_log[i], gdn_dt_bias[i],
                            gdn_norm_w[i], s5_lam_re[i], s5_lam_im[i], s5_b_re[i], s5_b_im[i], s5_c_re[i],
                            s5_c_im[i], s5_d[i], s5_log_step[i], s5_w_glu[i], h, prec)
        else:
            h = _odd_mixer(hn, bsz, seq, i, od_w_in_t, od_w_out, rwkv_shift_mu[i], rwkv_w0[i], rwkv_w2[i],
                           rwkv_a0[i], rwkv_a2[i], rwkv_g2[i], rwkv_k_k[i], rwkv_k_a[i], rwkv_r_k[i],
                           rwkv_ln_w[i], rwkv_ln_b[i], h, prec)
        hn = _rmsnorm(h, norm_ffn[layer], BF16)
        act = _ffn_up(hn, ffn_w_up, ffn_conv_w, layer, seq)
        h = _matmul_rows(act, w_down, layer, h)
    return _rmsnorm(h, norm_final, x.dtype).reshape(bsz, seq, d)
```

```python
import functools
import math

import jax
import jax.numpy as jnp
from jax import lax
from jax.experimental import pallas as pl
from jax.experimental.pallas import tpu as pltpu

F32 = jnp.float32
BF16 = jnp.bfloat16

V7X_VMEM_LIMIT_BYTES = 56 * 1024 * 1024
LANES = 128
SUBLANES = 8

EPS = 1e-6
HALF = 1024
GDN_HEADS = 8
GDN_D = 128
GDN_CONV = 4
GDN_CHUNK = 64
S5_GROUP = 16
S5_GROUPS = 64
S5_STATE = 64
S5_TILE_GROUPS = LANES // S5_GROUP
S5_TILE_STATES = S5_TILE_GROUPS * S5_STATE
RWKV_HEAD = 64
RWKV_PAIRS = HALF // LANES
RWKV_CHUNK = 64
RWKV_LORA_PAD = 384
RWKV_GN_EPS = 64e-5
RET_HEADS = 4
RET_DK = 128
RET_DV = 256
RET_CHUNK = 256
ROPE_BASE = 10000.0
INV_BLOCK = 16
GDN_CHUNKS_PER_STEP = 4
RWKV_CHUNKS_PER_STEP = 4


def _cparams(*sem):
    return pltpu.CompilerParams(dimension_semantics=sem, vmem_limit_bytes=V7X_VMEM_LIMIT_BYTES)


def _dot_dims(a, b, dims, precision):
    if precision is None:
        a, b = a.astype(BF16), b.astype(BF16)
    batch = ((), ())
    if a.ndim == 3:
        dims = tuple(tuple(d + 1 for d in side) for side in dims)
        batch = ((0,), (0,))
    return lax.dot_general(a, b, (dims, batch), preferred_element_type=F32, precision=precision)


def _dot(a, b, precision=None):
    return _dot_dims(a, b, ((1,), (0,)), precision)


def _dot_nt(a, b, precision=None):
    return _dot_dims(a, b, ((1,), (1,)), precision)


def _dot_tn(a, b, precision=None):
    return _dot_dims(a, b, ((0,), (0,)), precision)


def _time_cumsum(x):
    row = lax.broadcasted_iota(jnp.int32, x.shape, 0)
    d = 1
    while d < x.shape[0]:
        x = x + jnp.where(row >= d, pltpu.roll(x, d, 0), 0.0)
        d *= 2
    return x


def _iota2(shape):
    return lax.broadcasted_iota(jnp.int32, shape, 0), lax.broadcasted_iota(jnp.int32, shape, 1)


def _unit_lower_solve(a, rhs, row, col, sub, precision):
    assert sub == 4 * INV_BLOCK
    shift = INV_BLOCK.bit_length() - 1
    op = lambda x: x.astype(BF16) if precision is None else x
    eye = op(jnp.where(row == col, 1.0, 0.0).astype(F32))
    same_block = (row >> shift) == (col >> shift)
    ad = op(jnp.where(same_block, a, 0.0))
    ao = op(jnp.where(same_block, 0.0, a))
    a2 = op(_dot(ad, ad, precision))
    a4 = op(_dot(a2, a2, precision))
    t2 = _dot(eye - ad, eye + a2, precision)
    a8 = op(_dot(a4, a4, precision))
    t4 = _dot(t2, eye + a4, precision)
    td = op(_dot(t4, eye + a8, precision))
    n = op(_dot(td, ao, precision))
    r = _dot(td, rhs, precision)
    n2 = op(_dot(n, n, precision))
    return _dot(_dot(eye - n, eye + n2, precision), r, precision)


def _rmsnorm_kernel(x_ref, w_ref, o_ref):
    x = x_ref[...]
    ms = jnp.mean(x * x, axis=-1, keepdims=True)
    o_ref[...] = (x * lax.rsqrt(ms + EPS) * w_ref[...]).astype(o_ref.dtype)


def _rmsnorm(x, w, out_dtype):
    m, d = x.shape
    tm = min(512, m)
    return pl.pallas_call(
        _rmsnorm_kernel,
        grid=(m // tm,),
        in_specs=[pl.BlockSpec((tm, d), lambda i: (i, 0)), pl.BlockSpec((1, d), lambda i: (0, 0))],
        out_specs=pl.BlockSpec((tm, d), lambda i: (i, 0)),
        out_shape=jax.ShapeDtypeStruct((m, d), out_dtype),
        compiler_params=_cparams("parallel"),
        name="rmsnorm",
    )(x, w.reshape(1, d).astype(F32))


def _mm_kernel(*refs, n_pairs, has_res, w_transposed):
    n_in = 2 * n_pairs + int(has_res)
    o_ref = refs[n_in]
    wb_refs = refs[n_in + 1:]

    @pl.when(pl.program_id(1) == 0)
    def _():
        for p in range(n_pairs):
            w = refs[2 * p + 1][...]
            wb_refs[p][...] = (w.T if w_transposed else w).astype(BF16)

    acc = None
    for p in range(n_pairs):
        d = _dot(refs[2 * p][...], wb_refs[p][...])
        acc = d if acc is None else acc + d
    if has_res:
        acc = acc + refs[2 * n_pairs][...]
    o_ref[...] = acc.astype(o_ref.dtype)


def _pick_tile(n, prefs):
    for t in prefs:
        if n % t == 0:
            return t
    return n


def _matmul(pairs, n=None, res=None, out_dtype=F32, tm=1024, tn=None, w_transposed=False):
    m = pairs[0][0].shape[0]
    n = n or pairs[0][1].shape[1 if w_transposed else 2]
    tm = min(tm, m)
    tn = tn or _pick_tile(n, (1024, 1152, 512, 384, 256, 128))
    in_specs, args, scratch = [], [], []
    for a, w, l, r in pairs:
        k = a.shape[1]
        wspec = (pl.BlockSpec((None, tn, k), lambda j, i, l=l, r=r: (l, j, r)) if w_transposed else
                 pl.BlockSpec((None, k, tn), lambda j, i, l=l, r=r: (l, r, j)))
        in_specs += [pl.BlockSpec((tm, k), lambda j, i: (i, 0)), wspec]
        args += [a, w]
        scratch.append(pltpu.VMEM((k, tn), BF16))
    if res is not None:
        in_specs.append(pl.BlockSpec((tm, tn), lambda j, i: (i, j)))
        args.append(res)
    return pl.pallas_call(
        functools.partial(_mm_kernel, n_pairs=len(pairs), has_res=res is not None, w_transposed=w_transposed),
        grid=(n // tn, m // tm),
        in_specs=in_specs,
        out_specs=pl.BlockSpec((tm, tn), lambda j, i: (i, j)),
        out_shape=jax.ShapeDtypeStruct((m, n), out_dtype),
        scratch_shapes=scratch,
        compiler_params=_cparams("parallel", "arbitrary"),
        name="matmul",
    )(*args)


def _mm_rows_kernel(a_ref, w_ref, res_ref, o_ref):
    o_ref[...] = (_dot(a_ref[...], w_ref[...]) + res_ref[...]).astype(o_ref.dtype)


def _matmul_rows(a, w, layer, res, tm=1024, tn=512):
    m, k = a.shape
    n = w.shape[2]
    tm = min(tm, m)
    return pl.pallas_call(
        _mm_rows_kernel,
        grid=(m // tm, n // tn),
        in_specs=[pl.BlockSpec((tm, k), lambda i, j: (i, 0)),
                  pl.BlockSpec((None, k, tn), lambda i, j: (layer, 0, j)),
                  pl.BlockSpec((tm, tn), lambda i, j: (i, j))],
        out_specs=pl.BlockSpec((tm, tn), lambda i, j: (i, j)),
        out_shape=jax.ShapeDtypeStruct((m, n), res.dtype),
        compiler_params=_cparams("parallel", "parallel"),
        name="matmul_rows",
    )(a, w, res)


def _ffn_up_kernel(h_ref, wg_ref, wv_ref, cg_ref, cv_ref, o_ref, carry_ref, wgb_ref, wvb_ref, *, blocks_per_seq):
    i = pl.program_id(1)

    @pl.when(i == 0)
    def _():
        wgb_ref[...] = wg_ref[...].astype(BF16)
        wvb_ref[...] = wv_ref[...].astype(BF16)

    @pl.when(i % blocks_per_seq == 0)
    def _():
        carry_ref[...] = jnp.zeros_like(carry_ref)

    h = h_ref[...]
    zg = _dot(h, wgb_ref[...])
    zv = _dot(h, wvb_ref[...])
    tm = zg.shape[0]
    row = lax.broadcasted_iota(jnp.int32, zg.shape, 0)

    def conv(z, c_ref, prev):
        m1 = jnp.where(row == 0, prev[7:8, :], pltpu.roll(z, 1, 0))
        m2 = jnp.where(row == 0, prev[6:7, :], jnp.where(row == 1, prev[7:8, :], pltpu.roll(z, 2, 0)))
        c = c_ref[...]
        return c[0:1, :] * m2 + c[1:2, :] * m1 + c[2:3, :] * z

    g = conv(zg, cg_ref, carry_ref[0])
    v = conv(zv, cv_ref, carry_ref[1])
    carry_ref[0] = zg[tm - SUBLANES:, :]
    carry_ref[1] = zv[tm - SUBLANES:, :]
    o_ref[...] = (g * jax.nn.sigmoid(g) * v).astype(o_ref.dtype)


def _ffn_up(hn, w_up, conv_w, layer, seq):
    m, d = hn.shape
    f = w_up.shape[2] // 2
    tm = min(1024, seq)
    tn = _pick_tile(f, (512, 256, 128))
    nj = f // tn
    return pl.pallas_call(
        functools.partial(_ffn_up_kernel, blocks_per_seq=seq // tm),
        grid=(nj, m // tm),
        in_specs=[
            pl.BlockSpec((tm, d), lambda j, i: (i, 0)),
            pl.BlockSpec((None, d, tn), lambda j, i: (layer, 0, j)),
            pl.BlockSpec((None, d, tn), lambda j, i: (layer, 0, j + nj)),
            pl.BlockSpec((None, 3, tn), lambda j, i: (layer, 0, j)),
            pl.BlockSpec((None, 3, tn), lambda j, i: (layer, 0, j + nj)),
        ],
        out_specs=pl.BlockSpec((tm, tn), lambda j, i: (i, j)),
        out_shape=jax.ShapeDtypeStruct((m, f), BF16),
        scratch_shapes=[pltpu.VMEM((2, SUBLANES, tn), F32), pltpu.VMEM((d, tn), BF16), pltpu.VMEM((d, tn), BF16)],
        compiler_params=_cparams("parallel", "arbitrary"),
        name="ffn_up_conv",
    )(hn, w_up, w_up, conv_w, conv_w)


def _gdn_kernel(z_ref, zs_ref, cw_ref, alog_ref, dtb_ref, nw_ref, o_ref, s_ref, prev_ref, *, prec):
    c = GDN_CHUNK
    nqkv = 3 * HALF
    rows = z_ref.shape[1]
    nsub = rows // c

    @pl.when(pl.program_id(1) == 0)
    def _():
        s_ref[...] = jnp.zeros_like(s_ref)
        prev_ref[...] = jnp.zeros_like(prev_ref)

    z = z_ref[0, :, :nqkv]
    prev = prev_ref[...]
    cw = cw_ref[...]
    row8 = lax.broadcasted_iota(jnp.int32, (SUBLANES, nqkv), 0)
    acc = cw[GDN_CONV - 1:GDN_CONV, :] * z
    for tap in range(1, GDN_CONV):
        zr = pltpu.roll(z, tap, 0)
        head = jnp.where(row8 < tap, pltpu.roll(prev, tap, 0), zr[:SUBLANES, :])
        acc = acc + cw[GDN_CONV - 1 - tap:GDN_CONV - tap, :] * jnp.concatenate([head, zr[SUBLANES:, :]], axis=0)
    prev_ref[...] = z[rows - SUBLANES:, :]
    x = acc * jax.nn.sigmoid(acc)

    zs = zs_ref[0]
    beta_all = jax.nn.sigmoid(zs)
    g_all = -jnp.exp(alog_ref[...]) * jax.nn.softplus(zs + dtb_ref[...])
    row, col = _iota2((c, c))
    nh = GDN_HEADS
    units = [(sub * c, h) for sub in range(nsub) for h in range(nh)]
    per_unit = lambda off: jnp.stack([x[t0:t0 + c, off + h * GDN_D:off + (h + 1) * GDN_D] for t0, h in units])
    l2n = lambda t: t * lax.rsqrt(jnp.sum(t * t, axis=-1, keepdims=True) + 1e-6)
    q = l2n(per_unit(0)) * (GDN_D ** -0.5)
    k = l2n(per_unit(HALF))
    v = per_unit(2 * HALF)
    beta = jnp.stack([beta_all[t0:t0 + c, h:h + 1] for t0, h in units])
    gcum = [_time_cumsum(g_all[sub * c:(sub + 1) * c, :]) for sub in range(nsub)]
    gcum_t = [jnp.concatenate([gs, jnp.zeros((LANES - c, LANES), F32)], axis=0).T for gs in gcum]
    gc_col = [gcum[t0 // c][:, nh + h:nh + h + 1] for t0, h in units]
    diff = jnp.stack([gc_col[i] - gcum_t[t0 // c][nh + h:nh + h + 1, :c] for i, (t0, h) in enumerate(units)])
    decay = jnp.where(row >= col, jnp.exp(diff), 0.0)
    gc = jnp.stack([jnp.broadcast_to(gcol, (c, GDN_D)) for gcol in gc_col])
    gl = gc[:, c - 1:c, :]
    egc = jnp.exp(gc)
    kb = k * beta
    kk = _dot_nt(jnp.concatenate([kb, q], axis=1), k, prec)
    a = jnp.where(row > col, kk[:, :c, :] * decay, 0.0)
    attn = kk[:, c:, :] * decay
    sol = _unit_lower_solve(a, jnp.concatenate([v * beta, kb * egc], axis=2), row, col, c, prec)
    u, w = sol[:, :, :GDN_D], sol[:, :, GDN_D:]
    qa = jnp.concatenate([q * egc, attn], axis=2)
    kdec = k * jnp.exp(gl - gc)
    egl = jnp.exp(gl)
    s = s_ref[...]
    for sub in range(nsub):
        b0, b1 = sub * nh, (sub + 1) * nh
        v_new = u[b0:b1] - _dot(w[b0:b1], s, prec)
        o = _dot(qa[b0:b1], jnp.concatenate([s, v_new], axis=1), prec)
        s = s * egl[b0:b1] + _dot_tn(kdec[b0:b1], v_new, prec)
        o = o * lax.rsqrt(jnp.mean(o * o, axis=-1, keepdims=True) + EPS) * nw_ref[...]
        for h in range(nh):
            gate = z_ref[0, sub * c:(sub + 1) * c, nqkv + h * GDN_D:nqkv + (h + 1) * GDN_D]
            o_ref[0, sub * c:(sub + 1) * c, h * GDN_D:(h + 1) * GDN_D] = (
                o[h] * gate * jax.nn.sigmoid(gate)).astype(o_ref.dtype)
    s_ref[...] = s


def _gdn(z3, zs3, zs_blk, conv_w, a_log, dt_bias, norm_w, prec):
    b, t, nz = z3.shape
    c = GDN_CHUNKS_PER_STEP * GDN_CHUNK
    pad = lambda p: jnp.zeros((1, LANES), F32).at[0, GDN_HEADS:2 * GDN_HEADS].set(p.astype(F32))
    vec = pl.BlockSpec((1, LANES), lambda bi, ni: (0, 0))
    return pl.pallas_call(
        functools.partial(_gdn_kernel, prec=prec),
        grid=(b, t // c),
        in_specs=[pl.BlockSpec((1, c, nz), lambda bi, ni: (bi, ni, 0)),
                  pl.BlockSpec((1, c, LANES), lambda bi, ni: (bi, ni, zs_blk)),
                  pl.BlockSpec((GDN_CONV, 3 * HALF), lambda bi, ni: (0, 0)),
                  vec, vec, vec],
        out_specs=pl.BlockSpec((1, c, HALF), lambda bi, ni: (bi, ni, 0)),
        out_shape=jax.ShapeDtypeStruct((b, t, HALF), BF16),
        scratch_shapes=[pltpu.VMEM((GDN_HEADS, GDN_D, GDN_D), F32), pltpu.VMEM((SUBLANES, 3 * HALF), F32)],
        compiler_params=_cparams("parallel", "arbitrary"),
        name="gdn_chunk",
    )(z3, zs3, conv_w.astype(F32), pad(a_log), pad(dt_bias), norm_w.reshape(1, GDN_D).astype(F32))


def _s5_param_kernel(lr_ref, li_ref, dt_ref, br_ref, bi_ref, ar_ref, ai_ref, bbr_ref, bbi_ref):
    lr, li, dt = lr_ref[...], li_ref[...], dt_ref[...]
    step = jnp.exp(dt)
    mag = jnp.exp(lr * step)
    ang = li * step
    ab_re, ab_im = mag * jnp.cos(ang), mag * jnp.sin(ang)
    den = lr * lr + li * li
    nr = ab_re - 1.0
    f_re = (nr * lr + ab_im * li) / den
    f_im = (ab_im * lr - nr * li) / den
    br, bi = br_ref[...], bi_ref[...]
    bbr_ref[...] = f_re * br - f_im * bi
    bbi_ref[...] = f_re * bi + f_im * br
    n = ab_re.shape[1]
    cmul = lambda xr, xi, yr, yi: (xr * yr - xi * yi, xr * yi + xi * yr)
    row = lax.broadcasted_iota(jnp.int32, (SUBLANES, n), 0)
    cur = (ab_re, ab_im)
    pr = jnp.broadcast_to(ab_re, (SUBLANES, n))
    pi = jnp.broadcast_to(ab_im, (SUBLANES, n))
    for r in range(1, SUBLANES):
        cur = cmul(cur[0], cur[1], ab_re, ab_im)
        pr = jnp.where(row == r, cur[0], pr)
        pi = jnp.where(row == r, cur[1], pi)
    rows = SUBLANES
    while rows < ar_ref.shape[0]:
        tr, ti = cmul(pr, pi, pr[rows - 1:rows, :], pi[rows - 1:rows, :])
        pr = jnp.concatenate([pr, tr], axis=0)
        pi = jnp.concatenate([pi, ti], axis=0)
        rows *= 2
    ar_ref[...] = pr
    ai_ref[...] = pi


def _s5_params(lam_re, lam_im, log_step, b_re, b_im, n_pow):
    assert n_pow >= SUBLANES and n_pow & (n_pow - 1) == 0
    gp = S5_GROUPS * S5_STATE
    row = lambda x: x.astype(F32).reshape(1, gp)
    bt = lambda x: jnp.transpose(x.astype(F32), (2, 0, 1)).reshape(S5_GROUP, gp)
    pshape = jax.ShapeDtypeStruct((n_pow, gp), F32)
    mshape = jax.ShapeDtypeStruct((S5_GROUP, gp), F32)
    return pl.pallas_call(
        _s5_param_kernel,
        out_shape=[pshape, pshape, mshape, mshape],
        name="s5_params",
    )(row(lam_re), row(lam_im), row(jnp.repeat(log_step[:, None], S5_STATE, axis=1)), bt(b_re), bt(b_im))


def _s5_scan_kernel(u_ref, bdr_ref, bdi_ref, cdr_ref, cdi_ref, pr_ref, pi_ref, d_ref, o_ref,
                    up_ref, xr_ref, xi_ref, y_ref):
    t = u_ref.shape[1]
    n = S5_TILE_STATES
    nk = t // SUBLANES
    rt = min(512, t)

    def permute_in(k, _):
        r0 = pl.multiple_of(k * SUBLANES, SUBLANES)
        up_ref[pl.ds(r0, SUBLANES), :] = u_ref[0, pl.ds(k, SUBLANES, stride=nk), :]
        return 0

    lax.fori_loop(0, nk, permute_in, 0, unroll=SUBLANES)

    for r0 in range(0, t, rt):
        ub = up_ref[r0:r0 + rt, :].astype(BF16)
        xr_ref[r0:r0 + rt, :] = _dot(ub, bdr_ref[0])
        xi_ref[r0:r0 + rt, :] = _dot(ub, bdi_ref[0])

    ar = jnp.broadcast_to(pr_ref[0:1, :], (SUBLANES, n))
    ai = jnp.broadcast_to(pi_ref[0:1, :], (SUBLANES, n))

    def local_scan(k, carry):
        cr, ci = carry
        r0 = pl.multiple_of(k * SUBLANES, SUBLANES)
        xr = xr_ref[pl.ds(r0, SUBLANES), :] + (ar * cr - ai * ci)
        xi = xi_ref[pl.ds(r0, SUBLANES), :] + (ar * ci + ai * cr)
        xr_ref[pl.ds(r0, SUBLANES), :] = xr
        xi_ref[pl.ds(r0, SUBLANES), :] = xi
        return xr, xi

    zero = jnp.zeros((SUBLANES, n), F32)
    fr, fi = lax.fori_loop(0, nk, local_scan, (zero, zero), unroll=SUBLANES)

    row = lax.broadcasted_iota(jnp.int32, (SUBLANES, n), 0)
    cmul = lambda xr, xi, yr, yi: (xr * yr - xi * yi, xr * yi + xi * yr)
    gr = jnp.broadcast_to(pr_ref[nk - 1:nk, :], (SUBLANES, n))
    gi = jnp.broadcast_to(pi_ref[nk - 1:nk, :], (SUBLANES, n))
    for d in (1, 2, 4):
        sr = jnp.where(row >= d, pltpu.roll(fr, d, 0), 0.0)
        si = jnp.where(row >= d, pltpu.roll(fi, d, 0), 0.0)
        tr, ti = cmul(gr, gi, sr, si)
        fr, fi = fr + tr, fi + ti
        gr, gi = cmul(gr, gi, gr, gi)
    cr = jnp.where(row >= 1, pltpu.roll(fr, 1, 0), 0.0)
    ci = jnp.where(row >= 1, pltpu.roll(fi, 1, 0), 0.0)

    def add_carry(k8, _):
        p0 = pl.multiple_of(k8 * SUBLANES, SUBLANES)
        pr8 = pr_ref[pl.ds(p0, SUBLANES), :]
        pi8 = pi_ref[pl.ds(p0, SUBLANES), :]
        for j in range(SUBLANES):
            r0 = pl.multiple_of((k8 * SUBLANES + j) * SUBLANES, SUBLANES)
            pr = jnp.broadcast_to(pr8[j:j + 1, :], (SUBLANES, n))
            pi = jnp.broadcast_to(pi8[j:j + 1, :], (SUBLANES, n))
            xr_ref[pl.ds(r0, SUBLANES), :] = xr_ref[pl.ds(r0, SUBLANES), :] + (pr * cr - pi * ci)
            xi_ref[pl.ds(r0, SUBLANES), :] = xi_ref[pl.ds(r0, SUBLANES), :] + (pr * ci + pi * cr)
        return 0

    lax.fori_loop(0, nk // SUBLANES, add_carry, 0)

    for r0 in range(0, t, rt):
        y = (_dot(xr_ref[r0:r0 + rt, :].astype(BF16), cdr_ref[0])
             - _dot(xi_ref[r0:r0 + rt, :].astype(BF16), cdi_ref[0]) + up_ref[r0:r0 + rt, :] * d_ref[...])
        y_ref[r0:r0 + rt, :] = jax.nn.gelu(y)

    def permute_out(k, _):
        r0 = pl.multiple_of(k * SUBLANES, SUBLANES)
        o_ref[0, pl.ds(k, SUBLANES, stride=nk), :] = y_ref[pl.ds(r0, SUBLANES), :]
        return 0

    lax.fori_loop(0, nk, permute_out, 0, unroll=SUBLANES)


def _s5_scan(z3, u_off, bd_re, bd_im, cd_re, cd_im, p_re, p_im, d_skip):
    b, t, _ = z3.shape
    nt = HALF // LANES
    n = S5_TILE_STATES
    nk = t // SUBLANES
    assert p_re.shape[0] == nk
    return pl.pallas_call(
        _s5_scan_kernel,
        grid=(b, nt),
        in_specs=[
            pl.BlockSpec((1, t, LANES), lambda bi, j: (bi, 0, j + u_off)),
            pl.BlockSpec((1, LANES, n), lambda bi, j: (j, 0, 0)),
            pl.BlockSpec((1, LANES, n), lambda bi, j: (j, 0, 0)),
            pl.BlockSpec((1, n, LANES), lambda bi, j: (j, 0, 0)),
            pl.BlockSpec((1, n, LANES), lambda bi, j: (j, 0, 0)),
            pl.BlockSpec((nk, n), lambda bi, j: (0, j)),
            pl.BlockSpec((nk, n), lambda bi, j: (0, j)),
            pl.BlockSpec((1, LANES), lambda bi, j: (0, j)),
        ],
        out_specs=pl.BlockSpec((1, t, LANES), lambda bi, j: (bi, 0, j)),
        out_shape=jax.ShapeDtypeStruct((b, t, HALF), F32),
        scratch_shapes=[pltpu.VMEM((t, LANES), F32), pltpu.VMEM((t, n), F32), pltpu.VMEM((t, n), F32),
                        pltpu.VMEM((t, LANES), F32)],
        compiler_params=_cparams("parallel", "parallel"),
        name="s5_scan",
    )(z3, bd_re, bd_im, cd_re, cd_im, p_re, p_im, d_skip.reshape(1, HALF).astype(F32))


def _glu_kernel(y_ref, w_ref, yt_ref, o_ref):
    gate = _dot(y_ref[...].astype(BF16), w_ref[...])
    o_ref[...] = (yt_ref[...] * jax.nn.sigmoid(gate)).astype(o_ref.dtype)


def _glu(y, w):
    m, k = y.shape
    tm = min(1024, m)
    tn = 512
    return pl.pallas_call(
        _glu_kernel,
        grid=(m // tm, k // tn),
        in_specs=[pl.BlockSpec((tm, k), lambda i, j: (i, 0)),
                  pl.BlockSpec((k, tn), lambda i, j: (0, j)),
                  pl.BlockSpec((tm, tn), lambda i, j: (i, j))],
        out_specs=pl.BlockSpec((tm, tn), lambda i, j: (i, j)),
        out_shape=jax.ShapeDtypeStruct((m, k), BF16),
        compiler_params=_cparams("parallel", "parallel"),
        name="s5_glu",
    )(y, w, y)


def _s5_block_diag(bb_re, bb_im, c_re, c_im):
    nt, tg = HALF // LANES, S5_TILE_GROUPS
    eye = jnp.eye(tg, dtype=F32)

    def bmap(bb):
        x = bb.reshape(S5_GROUP, nt, tg, S5_STATE)
        x = jnp.einsum('cjgp,gh->jgchp', x, eye)
        return x.reshape(nt, LANES, S5_TILE_STATES).astype(BF16)

    def cmap(cc):
        x = cc.astype(F32).reshape(nt, tg, S5_GROUP, S5_STATE)
        x = jnp.einsum('jgcp,gh->jgphc', x, eye)
        return x.reshape(nt, S5_TILE_STATES, LANES).astype(BF16)

    return bmap(bb_re), bmap(bb_im), cmap(c_re), cmap(c_im)


def _rwkv_kernel(z_ref, zl_ref, mum_ref, mul_ref, w2_ref, a2_ref, g2_ref, w0_ref, a0_ref, kkp_ref, ka_ref,
                 rk_ref, lnw_ref, lnb_ref, o_ref, h_ref, prevm_ref, prevl_ref, *, prec):
    c = RWKV_CHUNK
    c2 = 2 * c
    rows = z_ref.shape[1]
    nsub = rows // c

    @pl.when(pl.program_id(1) == 0)
    def _():
        h_ref[...] = jnp.zeros_like(h_ref)
        prevm_ref[...] = jnp.zeros_like(prevm_ref)
        prevl_ref[...] = jnp.zeros_like(prevl_ref)

    def shift_mix(z, prev_ref, mu_ref):
        row0 = lax.broadcasted_iota(jnp.int32, z.shape, 0) == 0
        zm1 = jnp.where(row0, prev_ref[SUBLANES - 1:SUBLANES, :], pltpu.roll(z, 1, 0))
        prev_ref[...] = z[rows - SUBLANES:, :]
        return z + (zm1 - z) * mu_ref[...]

    x = shift_mix(z_ref[0], prevm_ref, mum_ref)
    zl = shift_mix(zl_ref[0], prevl_ref, mul_ref)
    r_all, k_raw, v_all = x[:, :HALF], x[:, HALF:2 * HALF], x[:, 2 * HALF:]
    wa = zl[:, :LANES]
    w = w0_ref[...] + _dot(jnp.tanh(wa), w2_ref[...])
    w = -jax.nn.softplus(-w) - 0.5
    lw_all = -jnp.exp(w)
    a = jax.nn.sigmoid(a0_ref[...] + _dot(wa, a2_ref[...]))
    g_all = _dot(jax.nn.sigmoid(zl[:, LANES:]), g2_ref[...])
    k_all = k_raw * (1.0 + (a - 1.0) * ka_ref[...])

    row, col = _iota2((c2, c2))
    same = (row >> 6) == (col >> 6)
    strict = jnp.logical_and(same, row > col)
    incl = jnp.logical_and(same, row >= col)
    lane = lax.broadcasted_iota(jnp.int32, (c, LANES), 1)
    first_head = lane < RWKV_HEAD

    def stack2(x):
        return jnp.concatenate([jnp.where(first_head, x, 0.0), jnp.where(first_head, 0.0, x)], axis=1)

    npair = RWKV_PAIRS
    units = [(sub * c, p * LANES) for sub in range(nsub) for p in range(npair)]
    nu = len(units)
    per_pair = lambda x: jnp.stack([x[t0:t0 + c, l0:l0 + LANES] for t0, l0 in units])
    per_pair_vec = lambda x: jnp.stack([x[:, l0:l0 + LANES] for _, l0 in units])
    def seg_sum(x):
        sa = jnp.sum(jnp.where(first_head, x, 0.0), axis=-1, keepdims=True)
        sb = jnp.sum(jnp.where(first_head, 0.0, x), axis=-1, keepdims=True)
        return jnp.where(first_head, sa, sb)
    r, lw, k, v = per_pair(r_all), per_pair(lw_all), per_pair(k_all), per_pair(v_all)
    kk = per_pair(k_raw * kkp_ref[...])
    kk = kk * lax.rsqrt(seg_sum(kk * kk) + 1e-6)
    b = kk * per_pair(a)
    cl = per_pair(jnp.concatenate([_time_cumsum(lw_all[sub * c:(sub + 1) * c, :]) for sub in range(nsub)], axis=0))
    cl_last = cl[:, c - 1:c, :]
    e_neg = jnp.exp(-cl)
    e_tail = jnp.exp(cl_last - cl)
    kk2 = stack2(kk * jnp.exp(cl - lw))
    r2 = stack2(r * jnp.exp(cl))
    b2 = stack2(b * e_neg)
    k2 = stack2(k * e_neg)
    v2 = stack2(v)
    bd2 = stack2(b * e_tail)
    kd2 = stack2(k * e_tail)
    sc = _dot_nt(jnp.concatenate([kk2, r2], axis=1), jnp.concatenate([b2, k2], axis=1), prec)
    a_ab = jnp.where(strict, sc[:, :c2, :c2], 0.0)
    a_ak = jnp.where(strict, sc[:, :c2, c2:], 0.0)
    r_bk = jnp.concatenate([jnp.where(incl, sc[:, c2:, :c2], 0.0), jnp.where(incl, sc[:, c2:, c2:], 0.0)], axis=2)
    wt = _unit_lower_solve(a_ab, jnp.concatenate([kk2, _dot(a_ak, v2, prec)], axis=2), row, col, c, prec)
    wk, tv = wt[:, :, :LANES], wt[:, :, LANES:]
    wr = jnp.concatenate([wk, r2], axis=1)
    bkd2 = jnp.concatenate([bd2, kd2], axis=1)
    e_last = jnp.exp(cl_last)
    ht = h_ref[...]
    y2 = []
    for sub in range(nsub):
        b0, b1 = sub * npair, (sub + 1) * npair
        hp = _dot_nt(wr[b0:b1], ht, prec)
        u = -hp[:, :c2, :] - tv[b0:b1]
        uv = jnp.concatenate([u, v2[b0:b1]], axis=1)
        y2.append(hp[:, c2:, :] + _dot(r_bk[b0:b1], uv, prec))
        ht = ht * e_last[b0:b1] + _dot_tn(uv, bkd2[b0:b1], prec)
    h_ref[...] = ht
    y2 = jnp.concatenate(y2, axis=0)
    y = y2[:, :c, :] + y2[:, c:, :]
    mu = seg_sum(y) * (1.0 / RWKV_HEAD)
    d = y - mu
    var = seg_sum(d * d) * (1.0 / RWKV_HEAD)
    yn = d * lax.rsqrt(var + RWKV_GN_EPS) * per_pair_vec(lnw_ref[...]) + per_pair_vec(lnb_ref[...])
    out = (yn + seg_sum(r * k * per_pair_vec(rk_ref[...])) * v) * per_pair(g_all)
    for i, (t0, l0) in enumerate(units):
        o_ref[0, t0:t0 + c, l0:l0 + LANES] = out[i].astype(o_ref.dtype)


def _rwkv(z3, zl3, lora_blk, mu_main, mu_lora, w2p, a2p, g2p, w0, a0, k_k, k_a, r_k, ln_w, ln_b, prec):
    bsz, t, nz = z3.shape
    c = RWKV_CHUNKS_PER_STEP * RWKV_CHUNK
    lw = RWKV_LORA_PAD
    vec = pl.BlockSpec((1, HALF), lambda bi, ni: (0, 0))
    full = lambda rows: pl.BlockSpec((rows, HALF), lambda bi, ni: (0, 0))
    row = lambda x: x.astype(F32).reshape(1, HALF)
    return pl.pallas_call(
        functools.partial(_rwkv_kernel, prec=prec),
        grid=(bsz, t // c),
        in_specs=[pl.BlockSpec((1, c, nz), lambda bi, ni: (bi, ni, 0)),
                  pl.BlockSpec((1, c, lw), lambda bi, ni: (bi, ni, lora_blk)),
                  pl.BlockSpec((1, nz), lambda bi, ni: (0, 0)),
                  pl.BlockSpec((1, lw), lambda bi, ni: (0, 0)),
                  full(LANES), full(LANES), full(lw - LANES)] + [vec] * 7,
        out_specs=pl.BlockSpec((1, c, HALF), lambda bi, ni: (bi, ni, 0)),
        out_shape=jax.ShapeDtypeStruct((bsz, t, HALF), BF16),
        scratch_shapes=[pltpu.VMEM((RWKV_PAIRS, LANES, LANES), F32), pltpu.VMEM((SUBLANES, nz), F32),
                        pltpu.VMEM((SUBLANES, lw), F32)],
        compiler_params=_cparams("parallel", "arbitrary"),
        name="rwkv_chunk",
    )(z3, zl3, mu_main, mu_lora, w2p, a2p, g2p, row(w0), row(a0), row(k_k), row(k_a), row(r_k), row(ln_w),
      row(ln_b))


def _rope_kernel(f_ref, cos_ref, sin_ref):
    t = cos_ref.shape[0]
    pos = lax.broadcasted_iota(jnp.int32, (t, LANES), 0).astype(F32)
    lane = lax.broadcasted_iota(jnp.int32, (t, LANES), 1)
    ang = pos * f_ref[...]
    cos_ref[...] = jnp.cos(ang)
    sin_ref[...] = jnp.where(lane < RET_DK // 2, -1.0, 1.0) * jnp.sin(ang)


def _rope_tables(t):
    inv_freq = ROPE_BASE ** (-jnp.linspace(0.0, 1.0, RET_DK // 2, dtype=F32))
    f2 = jnp.concatenate([inv_freq, inv_freq]).reshape(1, RET_DK)
    shape = jax.ShapeDtypeStruct((t, RET_DK), F32)
    return pl.pallas_call(_rope_kernel, out_shape=[shape, shape], name="rope_tables")(f2)


def _ret_kernel(q_ref, k_ref, v_ref, gate_ref, cos_ref, sin_ref, o_ref, s_ref):
    c = RET_CHUNK

    @pl.when(pl.program_id(1) == 0)
    def _():
        s_ref[...] = jnp.zeros_like(s_ref)

    row, col = _iota2((c, c))
    dist = (row - col).astype(F32)
    idx = lax.broadcasted_iota(jnp.int32, (c, 1), 0).astype(F32)
    cos, sin = cos_ref[...], sin_ref[...]
    rot = lambda x: x * cos + pltpu.roll(x, RET_DK // 2, 1) * sin

    for h in range(RET_HEADS):
        log_g = math.log(1.0 - 2.0 ** (-5.0 - h))
        q = rot(q_ref[0, :, h * RET_DK:(h + 1) * RET_DK])
        k = rot(k_ref[0, :, h * RET_DK:(h + 1) * RET_DK]) * (RET_DK ** -0.5)
        v = v_ref[0, :, h * RET_DV:(h + 1) * RET_DV]
        vb = v.astype(BF16)
        dmask = jnp.where(row >= col, jnp.exp(log_g * dist), 0.0)
        sc = _dot_nt(q.astype(BF16), k.astype(BF16)) * dmask
        s = s_ref[h]
        o = _dot(sc.astype(BF16), vb) + _dot((q * jnp.exp(log_g * (idx + 1.0))).astype(BF16), s.astype(BF16))
        kd = k * jnp.exp(log_g * (c - 1.0 - idx))
        s_ref[h] = s * math.exp(log_g * c) + _dot_tn(kd.astype(BF16), vb)
        o = o * lax.rsqrt(jnp.mean(o * o, axis=-1, keepdims=True) + EPS)
        gate = gate_ref[0, :, h * RET_DV:(h + 1) * RET_DV]
        o_ref[0, :, h * RET_DV:(h + 1) * RET_DV] = (o * gate * jax.nn.sigmoid(gate)).astype(o_ref.dtype)


def _retention(z3, q_blk, cos, sin):
    b, t, _ = z3.shape
    c = RET_CHUNK
    qk = RET_HEADS * RET_DK
    v_blk = (q_blk * qk + 2 * qk) // HALF
    return pl.pallas_call(
        _ret_kernel,
        grid=(b, t // c),
        in_specs=[
            pl.BlockSpec((1, c, qk), lambda bi, ni: (bi, ni, q_blk)),
            pl.BlockSpec((1, c, qk), lambda bi, ni: (bi, ni, q_blk + 1)),
            pl.BlockSpec((1, c, HALF), lambda bi, ni: (bi, ni, v_blk)),
            pl.BlockSpec((1, c, HALF), lambda bi, ni: (bi, ni, v_blk + 1)),
            pl.BlockSpec((c, RET_DK), lambda bi, ni: (ni, 0)),
            pl.BlockSpec((c, RET_DK), lambda bi, ni: (ni, 0)),
        ],
        out_specs=pl.BlockSpec((1, c, HALF), lambda bi, ni: (bi, ni, 0)),
        out_shape=jax.ShapeDtypeStruct((b, t, HALF), BF16),
        scratch_shapes=[pltpu.VMEM((RET_HEADS, RET_DK, RET_DV), F32)],
        compiler_params=_cparams("parallel", "arbitrary"),
        name="retention_chunk",
    )(z3, z3, z3, z3, cos, sin)


def _even_mixer(hn, bsz, seq, idx, w_in, w_out, conv_w, a_log, dt_bias, norm_w, lam_re, lam_im, b_re, b_im,
                c_re, c_im, d_skip, log_step, w_glu, res, prec):
    m = bsz * seq
    n_main = 4 * HALF
    n_small = 2 * GDN_HEADS
    w_b = jnp.concatenate([w_in[idx, n_main + n_small:, :],
                           jnp.pad(w_in[idx, n_main:n_main + n_small, :], ((0, LANES - n_small), (0, 0)))], axis=0)
    za = _matmul([(hn, w_in, idx, 0)], n=n_main, w_transposed=True).reshape(bsz, seq, n_main)
    zb = _matmul([(hn, w_b[None], 0, 0)], w_transposed=True).reshape(bsz, seq, HALF + LANES)
    ya = _gdn(za, zb, HALF // LANES, conv_w, a_log, dt_bias, norm_w, prec)
    p_re, p_im, bb_re, bb_im = _s5_params(lam_re, lam_im, log_step, b_re, b_im, seq // SUBLANES)
    bd_re, bd_im, cd_re, cd_im = _s5_block_diag(bb_re, bb_im, c_re, c_im)
    yg = _s5_scan(zb, 0, bd_re, bd_im, cd_re, cd_im, p_re, p_im, d_skip)
    yb = _glu(yg.reshape(m, HALF), w_glu.astype(BF16))
    return _matmul([(ya.reshape(m, HALF), w_out, idx, 0), (yb, w_out, idx, 1)], res=res)


def _odd_mixer(hn, bsz, seq, idx, w_in, w_out, shift_mu, w0, w2, a0, a2, g2, k_k, k_a, r_k, ln_w, ln_b, res, prec):
    m = bsz * seq
    n_main = 3 * HALF
    n_lora = 64 + 64 + 160
    w_b = jnp.concatenate([w_in[idx, n_main + n_lora:, :],
                           jnp.pad(w_in[idx, n_main:n_main + n_lora, :], ((0, RWKV_LORA_PAD - n_lora), (0, 0)))],
                          axis=0)
    za = _matmul([(hn, w_in, idx, 0)], n=n_main, w_transposed=True).reshape(bsz, seq, n_main)
    zb = _matmul([(hn, w_b[None], 0, 0)], w_transposed=True).reshape(bsz, seq, n_main + RWKV_LORA_PAD)
    mu_main = shift_mu[:n_main].astype(F32).reshape(1, n_main)
    mu_lora = jnp.pad(shift_mu[n_main:], (0, RWKV_LORA_PAD - n_lora)).astype(F32).reshape(1, RWKV_LORA_PAD)
    w2p = jnp.pad(w2, ((0, LANES - 64), (0, 0))).astype(BF16)
    a2p = jnp.pad(a2, ((64, 0), (0, 0))).astype(BF16)
    g2p = jnp.pad(g2, ((0, RWKV_LORA_PAD - LANES - 160), (0, 0))).astype(BF16)
    yc = _rwkv(za, zb, n_main // RWKV_LORA_PAD, mu_main, mu_lora, w2p, a2p, g2p, w0, a0, k_k, k_a,
               r_k, ln_w, ln_b, prec)
    cos, sin = _rope_tables(seq)
    yd = _retention(zb, 0, cos, sin)
    return _matmul([(yc.reshape(m, HALF), w_out, idx, 0), (yd.reshape(m, HALF), w_out, idx, 1)], res=res)


def kernel(x, norm_mix, norm_ffn, norm_final, ev_w_in, ev_w_out, gdn_conv_w, gdn_a_log, gdn_dt_bias, gdn_norm_w, s5_lam_re, s5_lam_im, s5_b_re, s5_b_im, s5_c_re, s5_c_im, s5_d, s5_log_step, s5_w_glu, od_w_in, od_w_out, rwkv_shift_mu, rwkv_w0, rwkv_w2, rwkv_a0, rwkv_a2, rwkv_g2, rwkv_k_k, rwkv_k_a, rwkv_r_k, rwkv_ln_w, rwkv_ln_b, ffn_w_up, ffn_conv_w, ffn_w_down):
    bsz, seq, d = x.shape
    m = bsz * seq
    depth = norm_mix.shape[0]
    prec = None
    w_down = ffn_w_down.astype(BF16)
    ev_w_in_t = jnp.swapaxes(ev_w_in, 1, 2)
    od_w_in_t = jnp.swapaxes(od_w_in, 1, 2)
    h = x.reshape(m, d).astype(F32)
    for layer in range(depth):
        hn = _rmsnorm(h, norm_mix[layer], BF16)
        i = layer // 2
        if layer % 2 == 0:
            h = _even_mixer(hn, bsz, seq, i, ev_w_in_t, ev_w_out, gdn_conv_w[i], gdn_a_log[i], gdn_dt_bias[i],
                            gdn_norm_w[i], s5_lam_re[i], s5_lam_im[i], s5_b_re[i], s5_b_im[i], s5_c_re[i],
                            s5_c_im[i], s5_d[i], s5_log_step[i], s5_w_glu[i], h, prec)
        else:
            h = _odd_mixer(hn, bsz, seq, i, od_w_in_t, od_w_out, rwkv_shift_mu[i], rwkv_w0[i], rwkv_w2[i],
                           rwkv_a0[i], rwkv_a2[i], rwkv_g2[i], rwkv_k_k[i], rwkv_k_a[i], rwkv_r_k[i],
                           rwkv_ln_w[i], rwkv_ln_b[i], h, prec)
        hn = _rmsnorm(h, norm_ffn[layer], BF16)
        act = _ffn_up(hn, ffn_w_up, ffn_conv_w, layer, seq)
        h = _matmul_rows(act, w_down, layer, h)
    return _rmsnorm(h, norm_final, x.dtype).reshape(bsz, seq, d)
```

```python
import functools
import math

import jax
import jax.numpy as jnp
from jax import lax
from jax.experimental import pallas as pl
from jax.experimental.pallas import tpu as pltpu

F32 = jnp.float32
BF16 = jnp.bfloat16

V7X_VMEM_LIMIT_BYTES = 56 * 1024 * 1024
LANES = 128
SUBLANES = 8

EPS = 1e-6
HALF = 1024
GDN_HEADS = 8
GDN_D = 128
GDN_CONV = 4
GDN_CHUNK = 64
S5_GROUP = 16
S5_GROUPS = 64
S5_STATE = 64
S5_TILE_GROUPS = LANES // S5_GROUP
S5_TILE_STATES = S5_TILE_GROUPS * S5_STATE
RWKV_HEAD = 64
RWKV_PAIRS = HALF // LANES
RWKV_CHUNK = 64
RWKV_LORA_PAD = 384
RWKV_GN_EPS = 64e-5
RET_HEADS = 4
RET_DK = 128
RET_DV = 256
RET_CHUNK = 256
ROPE_BASE = 10000.0
INV_BLOCK = 16
GDN_CHUNKS_PER_STEP = 4
RWKV_CHUNKS_PER_STEP = 4


def _cparams(*sem):
    return pltpu.CompilerParams(dimension_semantics=sem, vmem_limit_bytes=V7X_VMEM_LIMIT_BYTES)


def _dot_dims(a, b, dims, precision):
    if precision is None:
        a, b = a.astype(BF16), b.astype(BF16)
    batch = ((), ())
    if a.ndim == 3:
        dims = tuple(tuple(d + 1 for d in side) for side in dims)
        batch = ((0,), (0,))
    return lax.dot_general(a, b, (dims, batch), preferred_element_type=F32, precision=precision)


def _dot(a, b, precision=None):
    return _dot_dims(a, b, ((1,), (0,)), precision)


def _dot_nt(a, b, precision=None):
    return _dot_dims(a, b, ((1,), (1,)), precision)


def _dot_tn(a, b, precision=None):
    return _dot_dims(a, b, ((0,), (0,)), precision)


def _time_cumsum(x):
    row = lax.broadcasted_iota(jnp.int32, x.shape, 0)
    d = 1
    while d < x.shape[0]:
        x = x + jnp.where(row >= d, pltpu.roll(x, d, 0), 0.0)
        d *= 2
    return x


def _iota2(shape):
    return lax.broadcasted_iota(jnp.int32, shape, 0), lax.broadcasted_iota(jnp.int32, shape, 1)


def _unit_lower_solve(a, rhs, row, col, sub, precision):
    assert sub == 4 * INV_BLOCK
    shift = INV_BLOCK.bit_length() - 1
    op = lambda x: x.astype(BF16) if precision is None else x
    eye = op(jnp.where(row == col, 1.0, 0.0).astype(F32))
    a = op(a)
    ad = a * op(jnp.where((row >> shift) == (col >> shift), 1.0, 0.0).astype(F32))
    ao = a - ad
    a2 = op(_dot(ad, ad, precision))
    a4 = op(_dot(a2, a2, precision))
    t2 = _dot(eye - ad, eye + a2, precision)
    a8 = op(_dot(a4, a4, precision))
    t4 = _dot(t2, eye + a4, precision)
    td = op(_dot(t4, eye + a8, precision))
    n = op(_dot(td, ao, precision))
    r = _dot(td, rhs, precision)
    n2 = op(_dot(n, n, precision))
    return _dot(_dot(eye - n, eye + n2, precision), r, precision)


def _rmsnorm_kernel(x_ref, w_ref, o_ref):
    x = x_ref[...]
    ms = jnp.mean(x * x, axis=-1, keepdims=True)
    o_ref[...] = (x * lax.rsqrt(ms + EPS) * w_ref[...]).astype(o_ref.dtype)


def _rmsnorm(x, w, out_dtype):
    m, d = x.shape
    tm = min(512, m)
    return pl.pallas_call(
        _rmsnorm_kernel,
        grid=(m // tm,),
        in_specs=[pl.BlockSpec((tm, d), lambda i: (i, 0)), pl.BlockSpec((1, d), lambda i: (0, 0))],
        out_specs=pl.BlockSpec((tm, d), lambda i: (i, 0)),
        out_shape=jax.ShapeDtypeStruct((m, d), out_dtype),
        compiler_params=_cparams("parallel"),
        name="rmsnorm",
    )(x, w.reshape(1, d).astype(F32))


def _mm_kernel(*refs, n_pairs, has_res, w_transposed):
    n_in = 2 * n_pairs + int(has_res)
    o_ref = refs[n_in]
    wb_refs = refs[n_in + 1:]

    @pl.when(pl.program_id(1) == 0)
    def _():
        for p in range(n_pairs):
            w = refs[2 * p + 1][...]
            wb_refs[p][...] = (w.T if w_transposed else w).astype(BF16)

    acc = None
    for p in range(n_pairs):
        d = _dot(refs[2 * p][...], wb_refs[p][...])
        acc = d if acc is None else acc + d
    if has_res:
        acc = acc + refs[2 * n_pairs][...]
    o_ref[...] = acc.astype(o_ref.dtype)


def _pick_tile(n, prefs):
    for t in prefs:
        if n % t == 0:
            return t
    return n


def _matmul(pairs, n=None, res=None, out_dtype=F32, tm=1024, tn=None, w_transposed=False):
    m = pairs[0][0].shape[0]
    n = n or pairs[0][1].shape[1 if w_transposed else 2]
    tm = min(tm, m)
    tn = tn or _pick_tile(n, (1024, 1152, 512, 384, 256, 128))
    in_specs, args, scratch = [], [], []
    for a, w, l, r in pairs:
        k = a.shape[1]
        wspec = (pl.BlockSpec((None, tn, k), lambda j, i, l=l, r=r: (l, j, r)) if w_transposed else
                 pl.BlockSpec((None, k, tn), lambda j, i, l=l, r=r: (l, r, j)))
        in_specs += [pl.BlockSpec((tm, k), lambda j, i: (i, 0)), wspec]
        args += [a, w]
        scratch.append(pltpu.VMEM((k, tn), BF16))
    if res is not None:
        in_specs.append(pl.BlockSpec((tm, tn), lambda j, i: (i, j)))
        args.append(res)
    return pl.pallas_call(
        functools.partial(_mm_kernel, n_pairs=len(pairs), has_res=res is not None, w_transposed=w_transposed),
        grid=(n // tn, m // tm),
        in_specs=in_specs,
        out_specs=pl.BlockSpec((tm, tn), lambda j, i: (i, j)),
        out_shape=jax.ShapeDtypeStruct((m, n), out_dtype),
        scratch_shapes=scratch,
        compiler_params=_cparams("parallel", "arbitrary"),
        name="matmul",
    )(*args)


def _mm_rows_kernel(a_ref, w_ref, res_ref, o_ref):
    o_ref[...] = (_dot(a_ref[...], w_ref[...]) + res_ref[...]).astype(o_ref.dtype)


def _matmul_rows(a, w, layer, res, tm=1024, tn=512):
    m, k = a.shape
    n = w.shape[2]
    tm = min(tm, m)
    return pl.pallas_call(
        _mm_rows_kernel,
        grid=(m // tm, n // tn),
        in_specs=[pl.BlockSpec((tm, k), lambda i, j: (i, 0)),
                  pl.BlockSpec((None, k, tn), lambda i, j: (layer, 0, j)),
                  pl.BlockSpec((tm, tn), lambda i, j: (i, j))],
        out_specs=pl.BlockSpec((tm, tn), lambda i, j: (i, j)),
        out_shape=jax.ShapeDtypeStruct((m, n), res.dtype),
        compiler_params=_cparams("parallel", "parallel"),
        name="matmul_rows",
    )(a, w, res)


def _ffn_up_kernel(h_ref, wg_ref, wv_ref, cg_ref, cv_ref, o_ref, carry_ref, wgb_ref, wvb_ref, *, blocks_per_seq):
    i = pl.program_id(1)

    @pl.when(i == 0)
    def _():
        wgb_ref[...] = wg_ref[...].astype(BF16)
        wvb_ref[...] = wv_ref[...].astype(BF16)

    @pl.when(i % blocks_per_seq == 0)
    def _():
        carry_ref[...] = jnp.zeros_like(carry_ref)

    h = h_ref[...]
    zg = _dot(h, wgb_ref[...])
    zv = _dot(h, wvb_ref[...])
    tm = zg.shape[0]
    row = lax.broadcasted_iota(jnp.int32, zg.shape, 0)

    def conv(z, c_ref, prev):
        m1 = jnp.where(row == 0, prev[7:8, :], pltpu.roll(z, 1, 0))
        m2 = jnp.where(row == 0, prev[6:7, :], jnp.where(row == 1, prev[7:8, :], pltpu.roll(z, 2, 0)))
        c = c_ref[...]
        return c[0:1, :] * m2 + c[1:2, :] * m1 + c[2:3, :] * z

    g = conv(zg, cg_ref, carry_ref[0])
    v = conv(zv, cv_ref, carry_ref[1])
    carry_ref[0] = zg[tm - SUBLANES:, :]
    carry_ref[1] = zv[tm - SUBLANES:, :]
    o_ref[...] = (g * jax.nn.sigmoid(g) * v).astype(o_ref.dtype)


def _ffn_up(hn, w_up, conv_w, layer, seq):
    m, d = hn.shape
    f = w_up.shape[2] // 2
    tm = min(1024, seq)
    tn = _pick_tile(f, (512, 256, 128))
    nj = f // tn
    return pl.pallas_call(
        functools.partial(_ffn_up_kernel, blocks_per_seq=seq // tm),
        grid=(nj, m // tm),
        in_specs=[
            pl.BlockSpec((tm, d), lambda j, i: (i, 0)),
            pl.BlockSpec((None, d, tn), lambda j, i: (layer, 0, j)),
            pl.BlockSpec((None, d, tn), lambda j, i: (layer, 0, j + nj)),
            pl.BlockSpec((None, 3, tn), lambda j, i: (layer, 0, j)),
            pl.BlockSpec((None, 3, tn), lambda j, i: (layer, 0, j + nj)),
        ],
        out_specs=pl.BlockSpec((tm, tn), lambda j, i: (i, j)),
        out_shape=jax.ShapeDtypeStruct((m, f), BF16),
        scratch_shapes=[pltpu.VMEM((2, SUBLANES, tn), F32), pltpu.VMEM((d, tn), BF16), pltpu.VMEM((d, tn), BF16)],
        compiler_params=_cparams("parallel", "arbitrary"),
        name="ffn_up_conv",
    )(hn, w_up, w_up, conv_w, conv_w)


def _gdn_kernel(z_ref, zs_ref, cw_ref, alog_ref, dtb_ref, nw_ref, o_ref, s_ref, prev_ref, *, prec):
    c = GDN_CHUNK
    nqkv = 3 * HALF
    rows = z_ref.shape[1]
    nsub = rows // c

    @pl.when(pl.program_id(1) == 0)
    def _():
        s_ref[...] = jnp.zeros_like(s_ref)
        prev_ref[...] = jnp.zeros_like(prev_ref)

    z = z_ref[0, :, :nqkv]
    prev = prev_ref[...]
    cw = cw_ref[...]
    row8 = lax.broadcasted_iota(jnp.int32, (SUBLANES, nqkv), 0)
    acc = cw[GDN_CONV - 1:GDN_CONV, :] * z
    for tap in range(1, GDN_CONV):
        zr = pltpu.roll(z, tap, 0)
        head = jnp.where(row8 < tap, pltpu.roll(prev, tap, 0), zr[:SUBLANES, :])
        acc = acc + cw[GDN_CONV - 1 - tap:GDN_CONV - tap, :] * jnp.concatenate([head, zr[SUBLANES:, :]], axis=0)
    prev_ref[...] = z[rows - SUBLANES:, :]
    x = acc * jax.nn.sigmoid(acc)

    zs = zs_ref[0]
    beta_all = jax.nn.sigmoid(zs)
    g_all = -jnp.exp(alog_ref[...]) * jax.nn.softplus(zs + dtb_ref[...])
    row, col = _iota2((c, c))
    nh = GDN_HEADS
    units = [(sub * c, h) for sub in range(nsub) for h in range(nh)]
    per_unit = lambda off: jnp.stack([x[t0:t0 + c, off + h * GDN_D:off + (h + 1) * GDN_D] for t0, h in units])
    l2n = lambda t: t * lax.rsqrt(jnp.sum(t * t, axis=-1, keepdims=True) + 1e-6)
    q = l2n(per_unit(0)) * (GDN_D ** -0.5)
    k = l2n(per_unit(HALF))
    v = per_unit(2 * HALF)
    beta = jnp.stack([beta_all[t0:t0 + c, h:h + 1] for t0, h in units])
    gcum = [_time_cumsum(g_all[sub * c:(sub + 1) * c, :]) for sub in range(nsub)]
    gcum_t = [jnp.concatenate([gs, jnp.zeros((LANES - c, LANES), F32)], axis=0).T for gs in gcum]
    gc_col = [gcum[t0 // c][:, nh + h:nh + h + 1] for t0, h in units]
    diff = jnp.stack([gc_col[i] - gcum_t[t0 // c][nh + h:nh + h + 1, :c] for i, (t0, h) in enumerate(units)])
    decay = jnp.where(row >= col, jnp.exp(diff), 0.0)
    gc = jnp.stack([jnp.broadcast_to(gcol, (c, GDN_D)) for gcol in gc_col])
    gl = gc[:, c - 1:c, :]
    egc = jnp.exp(gc)
    kb = k * beta
    kk = _dot_nt(jnp.concatenate([kb, q], axis=1), k, prec)
    a = jnp.where(row > col, kk[:, :c, :] * decay, 0.0)
    attn = kk[:, c:, :] * decay
    sol = _unit_lower_solve(a, jnp.concatenate([v * beta, kb * egc], axis=2), row, col, c, prec)
    u, w = sol[:, :, :GDN_D], sol[:, :, GDN_D:]
    qa = jnp.concatenate([q * egc, attn], axis=2)
    kdec = k * jnp.exp(gl - gc)
    egl = jnp.exp(gl)
    s = s_ref[...]
    for sub in range(nsub):
        b0, b1 = sub * nh, (sub + 1) * nh
        v_new = u[b0:b1] - _dot(w[b0:b1], s, prec)
        o = _dot(qa[b0:b1], jnp.concatenate([s, v_new], axis=1), prec)
        s = s * egl[b0:b1] + _dot_tn(kdec[b0:b1], v_new, prec)
        o = o * lax.rsqrt(jnp.mean(o * o, axis=-1, keepdims=True) + EPS) * nw_ref[...]
        for h in range(nh):
            gate = z_ref[0, sub * c:(sub + 1) * c, nqkv + h * GDN_D:nqkv + (h + 1) * GDN_D]
            o_ref[0, sub * c:(sub + 1) * c, h * GDN_D:(h + 1) * GDN_D] = (
                o[h] * gate * jax.nn.sigmoid(gate)).astype(o_ref.dtype)
    s_ref[...] = s


def _gdn(z3, zs3, zs_blk, conv_w, a_log, dt_bias, norm_w, prec):
    b, t, nz = z3.shape
    c = GDN_CHUNKS_PER_STEP * GDN_CHUNK
    pad = lambda p: jnp.zeros((1, LANES), F32).at[0, GDN_HEADS:2 * GDN_HEADS].set(p.astype(F32))
    vec = pl.BlockSpec((1, LANES), lambda bi, ni: (0, 0))
    return pl.pallas_call(
        functools.partial(_gdn_kernel, prec=prec),
        grid=(b, t // c),
        in_specs=[pl.BlockSpec((1, c, nz), lambda bi, ni: (bi, ni, 0)),
                  pl.BlockSpec((1, c, LANES), lambda bi, ni: (bi, ni, zs_blk)),
                  pl.BlockSpec((GDN_CONV, 3 * HALF), lambda bi, ni: (0, 0)),
                  vec, vec, vec],
        out_specs=pl.BlockSpec((1, c, HALF), lambda bi, ni: (bi, ni, 0)),
        out_shape=jax.ShapeDtypeStruct((b, t, HALF), BF16),
        scratch_shapes=[pltpu.VMEM((GDN_HEADS, GDN_D, GDN_D), F32), pltpu.VMEM((SUBLANES, 3 * HALF), F32)],
        compiler_params=_cparams("parallel", "arbitrary"),
        name="gdn_chunk",
    )(z3, zs3, conv_w.astype(F32), pad(a_log), pad(dt_bias), norm_w.reshape(1, GDN_D).astype(F32))


def _s5_param_kernel(lr_ref, li_ref, dt_ref, br_ref, bi_ref, ar_ref, ai_ref, bbr_ref, bbi_ref):
    lr, li, dt = lr_ref[...], li_ref[...], dt_ref[...]
    step = jnp.exp(dt)
    mag = jnp.exp(lr * step)
    ang = li * step
    ab_re, ab_im = mag * jnp.cos(ang), mag * jnp.sin(ang)
    den = lr * lr + li * li
    nr = ab_re - 1.0
    f_re = (nr * lr + ab_im * li) / den
    f_im = (ab_im * lr - nr * li) / den
    br, bi = br_ref[...], bi_ref[...]
    bbr_ref[...] = f_re * br - f_im * bi
    bbi_ref[...] = f_re * bi + f_im * br
    n = ab_re.shape[1]
    cmul = lambda xr, xi, yr, yi: (xr * yr - xi * yi, xr * yi + xi * yr)
    row = lax.broadcasted_iota(jnp.int32, (SUBLANES, n), 0)
    cur = (ab_re, ab_im)
    pr = jnp.broadcast_to(ab_re, (SUBLANES, n))
    pi = jnp.broadcast_to(ab_im, (SUBLANES, n))
    for r in range(1, SUBLANES):
        cur = cmul(cur[0], cur[1], ab_re, ab_im)
        pr = jnp.where(row == r, cur[0], pr)
        pi = jnp.where(row == r, cur[1], pi)
    rows = SUBLANES
    while rows < ar_ref.shape[0]:
        tr, ti = cmul(pr, pi, pr[rows - 1:rows, :], pi[rows - 1:rows, :])
        pr = jnp.concatenate([pr, tr], axis=0)
        pi = jnp.concatenate([pi, ti], axis=0)
        rows *= 2
    ar_ref[...] = pr
    ai_ref[...] = pi


def _s5_params(lam_re, lam_im, log_step, b_re, b_im, n_pow):
    assert n_pow >= SUBLANES and n_pow & (n_pow - 1) == 0
    gp = S5_GROUPS * S5_STATE
    row = lambda x: x.astype(F32).reshape(1, gp)
    bt = lambda x: jnp.transpose(x.astype(F32), (2, 0, 1)).reshape(S5_GROUP, gp)
    pshape = jax.ShapeDtypeStruct((n_pow, gp), F32)
    mshape = jax.ShapeDtypeStruct((S5_GROUP, gp), F32)
    return pl.pallas_call(
        _s5_param_kernel,
        out_shape=[pshape, pshape, mshape, mshape],
        name="s5_params",
    )(row(lam_re), row(lam_im), row(jnp.repeat(log_step[:, None], S5_STATE, axis=1)), bt(b_re), bt(b_im))


def _s5_scan_kernel(u_ref, bdr_ref, bdi_ref, cdr_ref, cdi_ref, pr_ref, pi_ref, d_ref, o_ref,
                    up_ref, xr_ref, xi_ref, y_ref):
    t = u_ref.shape[1]
    n = S5_TILE_STATES
    nk = t // SUBLANES
    rt = min(512, t)

    def permute_in(k, _):
        r0 = pl.multiple_of(k * SUBLANES, SUBLANES)
        up_ref[pl.ds(r0, SUBLANES), :] = u_ref[0, pl.ds(k, SUBLANES, stride=nk), :]
        return 0

    lax.fori_loop(0, nk, permute_in, 0, unroll=SUBLANES)

    for r0 in range(0, t, rt):
        ub = up_ref[r0:r0 + rt, :].astype(BF16)
        xr_ref[r0:r0 + rt, :] = _dot(ub, bdr_ref[0])
        xi_ref[r0:r0 + rt, :] = _dot(ub, bdi_ref[0])

    ar = jnp.broadcast_to(pr_ref[0:1, :], (SUBLANES, n))
    ai = jnp.broadcast_to(pi_ref[0:1, :], (SUBLANES, n))

    def local_scan(k, carry):
        cr, ci = carry
        r0 = pl.multiple_of(k * SUBLANES, SUBLANES)
        xr = xr_ref[pl.ds(r0, SUBLANES), :] + (ar * cr - ai * ci)
        xi = xi_ref[pl.ds(r0, SUBLANES), :] + (ar * ci + ai * cr)
        xr_ref[pl.ds(r0, SUBLANES), :] = xr
        xi_ref[pl.ds(r0, SUBLANES), :] = xi
        return xr, xi

    zero = jnp.zeros((SUBLANES, n), F32)
    fr, fi = lax.fori_loop(0, nk, local_scan, (zero, zero), unroll=SUBLANES)

    row = lax.broadcasted_iota(jnp.int32, (SUBLANES, n), 0)
    cmul = lambda xr, xi, yr, yi: (xr * yr - xi * yi, xr * yi + xi * yr)
    gr = jnp.broadcast_to(pr_ref[nk - 1:nk, :], (SUBLANES, n))
    gi = jnp.broadcast_to(pi_ref[nk - 1:nk, :], (SUBLANES, n))
    for d in (1, 2, 4):
        sr = jnp.where(row >= d, pltpu.roll(fr, d, 0), 0.0)
        si = jnp.where(row >= d, pltpu.roll(fi, d, 0), 0.0)
        tr, ti = cmul(gr, gi, sr, si)
        fr, fi = fr + tr, fi + ti
        gr, gi = cmul(gr, gi, gr, gi)
    cr = jnp.where(row >= 1, pltpu.roll(fr, 1, 0), 0.0)
    ci = jnp.where(row >= 1, pltpu.roll(fi, 1, 0), 0.0)

    def add_carry(k8, _):
        p0 = pl.multiple_of(k8 * SUBLANES, SUBLANES)
        pr8 = pr_ref[pl.ds(p0, SUBLANES), :]
        pi8 = pi_ref[pl.ds(p0, SUBLANES), :]
        for j in range(SUBLANES):
            r0 = pl.multiple_of((k8 * SUBLANES + j) * SUBLANES, SUBLANES)
            pr = jnp.broadcast_to(pr8[j:j + 1, :], (SUBLANES, n))
            pi = jnp.broadcast_to(pi8[j:j + 1, :], (SUBLANES, n))
            xr_ref[pl.ds(r0, SUBLANES), :] = xr_ref[pl.ds(r0, SUBLANES), :] + (pr * cr - pi * ci)
            xi_ref[pl.ds(r0, SUBLANES), :] = xi_ref[pl.ds(r0, SUBLANES), :] + (pr * ci + pi * cr)
        return 0

    lax.fori_loop(0, nk // SUBLANES, add_carry, 0)

    for r0 in range(0, t, rt):
        y = (_dot(xr_ref[r0:r0 + rt, :].astype(BF16), cdr_ref[0])
             - _dot(xi_ref[r0:r0 + rt, :].astype(BF16), cdi_ref[0]) + up_ref[r0:r0 + rt, :] * d_ref[...])
        y_ref[r0:r0 + rt, :] = jax.nn.gelu(y)

    def permute_out(k, _):
        r0 = pl.multiple_of(k * SUBLANES, SUBLANES)
        o_ref[0, pl.ds(k, SUBLANES, stride=nk), :] = y_ref[pl.ds(r0, SUBLANES), :]
        return 0

    lax.fori_loop(0, nk, permute_out, 0, unroll=SUBLANES)


def _s5_scan(z3, u_off, bd_re, bd_im, cd_re, cd_im, p_re, p_im, d_skip):
    b, t, _ = z3.shape
    nt = HALF // LANES
    n = S5_TILE_STATES
    nk = t // SUBLANES
    assert p_re.shape[0] == nk
    return pl.pallas_call(
        _s5_scan_kernel,
        grid=(b, nt),
        in_specs=[
            pl.BlockSpec((1, t, LANES), lambda bi, j: (bi, 0, j + u_off)),
            pl.BlockSpec((1, LANES, n), lambda bi, j: (j, 0, 0)),
            pl.BlockSpec((1, LANES, n), lambda bi, j: (j, 0, 0)),
            pl.BlockSpec((1, n, LANES), lambda bi, j: (j, 0, 0)),
            pl.BlockSpec((1, n, LANES), lambda bi, j: (j, 0, 0)),
            pl.BlockSpec((nk, n), lambda bi, j: (0, j)),
            pl.BlockSpec((nk, n), lambda bi, j: (0, j)),
            pl.BlockSpec((1, LANES), lambda bi, j: (0, j)),
        ],
        out_specs=pl.BlockSpec((1, t, LANES), lambda bi, j: (bi, 0, j)),
        out_shape=jax.ShapeDtypeStruct((b, t, HALF), F32),
        scratch_shapes=[pltpu.VMEM((t, LANES), F32), pltpu.VMEM((t, n), F32), pltpu.VMEM((t, n), F32),
                        pltpu.VMEM((t, LANES), F32)],
        compiler_params=_cparams("parallel", "parallel"),
        name="s5_scan",
    )(z3, bd_re, bd_im, cd_re, cd_im, p_re, p_im, d_skip.reshape(1, HALF).astype(F32))


def _glu_kernel(y_ref, w_ref, yt_ref, o_ref):
    gate = _dot(y_ref[...].astype(BF16), w_ref[...])
    o_ref[...] = (yt_ref[...] * jax.nn.sigmoid(gate)).astype(o_ref.dtype)


def _glu(y, w):
    m, k = y.shape
    tm = min(1024, m)
    tn = 512
    return pl.pallas_call(
        _glu_kernel,
        grid=(m // tm, k // tn),
        in_specs=[pl.BlockSpec((tm, k), lambda i, j: (i, 0)),
                  pl.BlockSpec((k, tn), lambda i, j: (0, j)),
                  pl.BlockSpec((tm, tn), lambda i, j: (i, j))],
        out_specs=pl.BlockSpec((tm, tn), lambda i, j: (i, j)),
        out_shape=jax.ShapeDtypeStruct((m, k), BF16),
        compiler_params=_cparams("parallel", "parallel"),
        name="s5_glu",
    )(y, w, y)


def _s5_block_diag(bb_re, bb_im, c_re, c_im):
    nt, tg = HALF // LANES, S5_TILE_GROUPS
    eye = jnp.eye(tg, dtype=F32)

    def bmap(bb):
        x = bb.reshape(S5_GROUP, nt, tg, S5_STATE)
        x = jnp.einsum('cjgp,gh->jgchp', x, eye)
        return x.reshape(nt, LANES, S5_TILE_STATES).astype(BF16)

    def cmap(cc):
        x = cc.astype(F32).reshape(nt, tg, S5_GROUP, S5_STATE)
        x = jnp.einsum('jgcp,gh->jgphc', x, eye)
        return x.reshape(nt, S5_TILE_STATES, LANES).astype(BF16)

    return bmap(bb_re), bmap(bb_im), cmap(c_re), cmap(c_im)


def _rwkv_kernel(z_ref, zl_ref, mum_ref, mul_ref, w2_ref, a2_ref, g2_ref, w0_ref, a0_ref, kkp_ref, ka_ref,
                 rk_ref, lnw_ref, lnb_ref, o_ref, h_ref, prevm_ref, prevl_ref, *, prec):
    c = RWKV_CHUNK
    c2 = 2 * c
    rows = z_ref.shape[1]
    nsub = rows // c

    @pl.when(pl.program_id(1) == 0)
    def _():
        h_ref[...] = jnp.zeros_like(h_ref)
        prevm_ref[...] = jnp.zeros_like(prevm_ref)
        prevl_ref[...] = jnp.zeros_like(prevl_ref)

    def shift_mix(z, prev_ref, mu_ref):
        row0 = lax.broadcasted_iota(jnp.int32, z.shape, 0) == 0
        zm1 = jnp.where(row0, prev_ref[SUBLANES - 1:SUBLANES, :], pltpu.roll(z, 1, 0))
        prev_ref[...] = z[rows - SUBLANES:, :]
        return z + (zm1 - z) * mu_ref[...]

    x = shift_mix(z_ref[0], prevm_ref, mum_ref)
    zl = shift_mix(zl_ref[0], prevl_ref, mul_ref)
    r_all, k_raw, v_all = x[:, :HALF], x[:, HALF:2 * HALF], x[:, 2 * HALF:]
    wa = zl[:, :LANES]
    w = w0_ref[...] + _dot(jnp.tanh(wa), w2_ref[...])
    w = -jax.nn.softplus(-w) - 0.5
    lw_all = -jnp.exp(w)
    a = jax.nn.sigmoid(a0_ref[...] + _dot(wa, a2_ref[...]))
    g_all = _dot(jax.nn.sigmoid(zl[:, LANES:]), g2_ref[...])
    k_all = k_raw * (1.0 + (a - 1.0) * ka_ref[...])

    row, col = _iota2((c2, c2))
    same = (row >> 6) == (col >> 6)
    strict = jnp.where(jnp.logical_and(same, row > col), 1.0, 0.0).astype(BF16)
    incl = jnp.where(jnp.logical_and(same, row >= col), 1.0, 0.0).astype(BF16)
    lane = lax.broadcasted_iota(jnp.int32, (c, LANES), 1)
    first_head = lane < RWKV_HEAD
    head0 = jnp.where(first_head, 1.0, 0.0).astype(BF16)
    head1 = jnp.where(first_head, 0.0, 1.0).astype(BF16)

    def stack2(x):
        xb = x.astype(BF16)
        return jnp.concatenate([xb * head0, xb * head1], axis=1)

    npair = RWKV_PAIRS
    units = [(sub * c, p * LANES) for sub in range(nsub) for p in range(npair)]
    nu = len(units)
    per_pair = lambda x: jnp.stack([x[t0:t0 + c, l0:l0 + LANES] for t0, l0 in units])
    per_pair_vec = lambda x: jnp.stack([x[:, l0:l0 + LANES] for _, l0 in units])
    def seg_sum(x):
        sa = jnp.sum(jnp.where(first_head, x, 0.0), axis=-1, keepdims=True)
        sb = jnp.sum(jnp.where(first_head, 0.0, x), axis=-1, keepdims=True)
        return jnp.where(first_head, sa, sb)
    r, lw, k, v = per_pair(r_all), per_pair(lw_all), per_pair(k_all), per_pair(v_all)
    kk = per_pair(k_raw * kkp_ref[...])
    kk = kk * lax.rsqrt(seg_sum(kk * kk) + 1e-6)
    b = kk * per_pair(a)
    cl = per_pair(jnp.concatenate([_time_cumsum(lw_all[sub * c:(sub + 1) * c, :]) for sub in range(nsub)], axis=0))
    cl_last = cl[:, c - 1:c, :]
    e_neg = jnp.exp(-cl)
    e_tail = jnp.exp(cl_last - cl)
    kk2 = stack2(kk * jnp.exp(cl - lw))
    r2 = stack2(r * jnp.exp(cl))
    b2 = stack2(b * e_neg)
    k2 = stack2(k * e_neg)
    v2 = stack2(v)
    bd2 = stack2(b * e_tail)
    kd2 = stack2(k * e_tail)
    sc = _dot_nt(jnp.concatenate([kk2, r2], axis=1), jnp.concatenate([b2, k2], axis=1), prec).astype(BF16)
    a_ab = sc[:, :c2, :c2] * strict
    a_ak = sc[:, :c2, c2:] * strict
    r_bk = jnp.concatenate([sc[:, c2:, :c2] * incl, sc[:, c2:, c2:] * incl], axis=2)
    rhs = jnp.concatenate([kk2, _dot(a_ak, v2, prec).astype(BF16)], axis=2)
    wt = _unit_lower_solve(a_ab, rhs, row, col, c, prec)
    wk, tv = wt[:, :, :LANES], wt[:, :, LANES:]
    wr = jnp.concatenate([wk.astype(BF16), r2], axis=1)
    bkd2 = jnp.concatenate([bd2, kd2], axis=1)
    e_last = jnp.exp(cl_last)
    ht = h_ref[...]
    y2 = []
    for sub in range(nsub):
        b0, b1 = sub * npair, (sub + 1) * npair
        hp = _dot_nt(wr[b0:b1], ht, prec)
        u = -hp[:, :c2, :] - tv[b0:b1]
        uv = jnp.concatenate([u.astype(BF16), v2[b0:b1]], axis=1)
        y2.append(hp[:, c2:, :] + _dot(r_bk[b0:b1], uv, prec))
        ht = ht * e_last[b0:b1] + _dot_tn(uv, bkd2[b0:b1], prec)
    h_ref[...] = ht
    y2 = jnp.concatenate(y2, axis=0)
    y = y2[:, :c, :] + y2[:, c:, :]
    mu = seg_sum(y) * (1.0 / RWKV_HEAD)
    d = y - mu
    var = seg_sum(d * d) * (1.0 / RWKV_HEAD)
    yn = d * lax.rsqrt(var + RWKV_GN_EPS) * per_pair_vec(lnw_ref[...]) + per_pair_vec(lnb_ref[...])
    out = (yn + seg_sum(r * k * per_pair_vec(rk_ref[...])) * v) * per_pair(g_all)
    for i, (t0, l0) in enumerate(units):
        o_ref[0, t0:t0 + c, l0:l0 + LANES] = out[i].astype(o_ref.dtype)


def _rwkv(z3, zl3, lora_blk, mu_main, mu_lora, w2p, a2p, g2p, w0, a0, k_k, k_a, r_k, ln_w, ln_b, prec):
    bsz, t, nz = z3.shape
    c = RWKV_CHUNKS_PER_STEP * RWKV_CHUNK
    lw = RWKV_LORA_PAD
    vec = pl.BlockSpec((1, HALF), lambda bi, ni: (0, 0))
    full = lambda rows: pl.BlockSpec((rows, HALF), lambda bi, ni: (0, 0))
    row = lambda x: x.astype(F32).reshape(1, HALF)
    return pl.pallas_call(
        functools.partial(_rwkv_kernel, prec=prec),
        grid=(bsz, t // c),
        in_specs=[pl.BlockSpec((1, c, nz), lambda bi, ni: (bi, ni, 0)),
                  pl.BlockSpec((1, c, lw), lambda bi, ni: (bi, ni, lora_blk)),
                  pl.BlockSpec((1, nz), lambda bi, ni: (0, 0)),
                  pl.BlockSpec((1, lw), lambda bi, ni: (0, 0)),
                  full(LANES), full(LANES), full(lw - LANES)] + [vec] * 7,
        out_specs=pl.BlockSpec((1, c, HALF), lambda bi, ni: (bi, ni, 0)),
        out_shape=jax.ShapeDtypeStruct((bsz, t, HALF), BF16),
        scratch_shapes=[pltpu.VMEM((RWKV_PAIRS, LANES, LANES), F32), pltpu.VMEM((SUBLANES, nz), F32),
                        pltpu.VMEM((SUBLANES, lw), F32)],
        compiler_params=_cparams("parallel", "arbitrary"),
        name="rwkv_chunk",
    )(z3, zl3, mu_main, mu_lora, w2p, a2p, g2p, row(w0), row(a0), row(k_k), row(k_a), row(r_k), row(ln_w),
      row(ln_b))


def _rope_kernel(f_ref, cos_ref, sin_ref):
    t = cos_ref.shape[0]
    pos = lax.broadcasted_iota(jnp.int32, (t, LANES), 0).astype(F32)
    lane = lax.broadcasted_iota(jnp.int32, (t, LANES), 1)
    ang = pos * f_ref[...]
    cos_ref[...] = jnp.cos(ang)
    sin_ref[...] = jnp.where(lane < RET_DK // 2, -1.0, 1.0) * jnp.sin(ang)


def _rope_tables(t):
    inv_freq = ROPE_BASE ** (-jnp.linspace(0.0, 1.0, RET_DK // 2, dtype=F32))
    f2 = jnp.concatenate([inv_freq, inv_freq]).reshape(1, RET_DK)
    shape = jax.ShapeDtypeStruct((t, RET_DK), F32)
    return pl.pallas_call(_rope_kernel, out_shape=[shape, shape], name="rope_tables")(f2)


def _ret_kernel(q_ref, k_ref, v_ref, gate_ref, cos_ref, sin_ref, o_ref, s_ref):
    c = RET_CHUNK

    @pl.when(pl.program_id(1) == 0)
    def _():
        s_ref[...] = jnp.zeros_like(s_ref)

    row, col = _iota2((c, c))
    dist = (row - col).astype(F32)
    idx = lax.broadcasted_iota(jnp.int32, (c, 1), 0).astype(F32)
    cos, sin = cos_ref[...], sin_ref[...]
    rot = lambda x: x * cos + pltpu.roll(x, RET_DK // 2, 1) * sin

    for h in range(RET_HEADS):
        log_g = math.log(1.0 - 2.0 ** (-5.0 - h))
        q = rot(q_ref[0, :, h * RET_DK:(h + 1) * RET_DK])
        k = rot(k_ref[0, :, h * RET_DK:(h + 1) * RET_DK]) * (RET_DK ** -0.5)
        v = v_ref[0, :, h * RET_DV:(h + 1) * RET_DV]
        vb = v.astype(BF16)
        dmask = jnp.where(row >= col, jnp.exp(log_g * dist), 0.0)
        sc = _dot_nt(q.astype(BF16), k.astype(BF16)) * dmask
        s = s_ref[h]
        o = _dot(sc.astype(BF16), vb) + _dot((q * jnp.exp(log_g * (idx + 1.0))).astype(BF16), s.astype(BF16))
        kd = k * jnp.exp(log_g * (c - 1.0 - idx))
        s_ref[h] = s * math.exp(log_g * c) + _dot_tn(kd.astype(BF16), vb)
        o = o * lax.rsqrt(jnp.mean(o * o, axis=-1, keepdims=True) + EPS)
        gate = gate_ref[0, :, h * RET_DV:(h + 1) * RET_DV]
        o_ref[0, :, h * RET_DV:(h + 1) * RET_DV] = (o * gate * jax.nn.sigmoid(gate)).astype(o_ref.dtype)


def _retention(z3, q_blk, cos, sin):
    b, t, _ = z3.shape
    c = RET_CHUNK
    qk = RET_HEADS * RET_DK
    v_blk = (q_blk * qk + 2 * qk) // HALF
    return pl.pallas_call(
        _ret_kernel,
        grid=(b, t // c),
        in_specs=[
            pl.BlockSpec((1, c, qk), lambda bi, ni: (bi, ni, q_blk)),
            pl.BlockSpec((1, c, qk), lambda bi, ni: (bi, ni, q_blk + 1)),
            pl.BlockSpec((1, c, HALF), lambda bi, ni: (bi, ni, v_blk)),
            pl.BlockSpec((1, c, HALF), lambda bi, ni: (bi, ni, v_blk + 1)),
            pl.BlockSpec((c, RET_DK), lambda bi, ni: (ni, 0)),
            pl.BlockSpec((c, RET_DK), lambda bi, ni: (ni, 0)),
        ],
        out_specs=pl.BlockSpec((1, c, HALF), lambda bi, ni: (bi, ni, 0)),
        out_shape=jax.ShapeDtypeStruct((b, t, HALF), BF16),
        scratch_shapes=[pltpu.VMEM((RET_HEADS, RET_DK, RET_DV), F32)],
        compiler_params=_cparams("parallel", "arbitrary"),
        name="retention_chunk",
    )(z3, z3, z3, z3, cos, sin)


def _even_mixer(hn, bsz, seq, idx, w_in, w_out, conv_w, a_log, dt_bias, norm_w, lam_re, lam_im, b_re, b_im,
                c_re, c_im, d_skip, log_step, w_glu, res, prec):
    m = bsz * seq
    n_main = 4 * HALF
    n_small = 2 * GDN_HEADS
    w_b = jnp.concatenate([w_in[idx, n_main + n_small:, :],
                           jnp.pad(w_in[idx, n_main:n_main + n_small, :], ((0, LANES - n_small), (0, 0)))], axis=0)
    za = _matmul([(hn, w_in, idx, 0)], n=n_main, w_transposed=True).reshape(bsz, seq, n_main)
    zb = _matmul([(hn, w_b[None], 0, 0)], w_transposed=True).reshape(bsz, seq, HALF + LANES)
    ya = _gdn(za, zb, HALF // LANES, conv_w, a_log, dt_bias, norm_w, prec)
    p_re, p_im, bb_re, bb_im = _s5_params(lam_re, lam_im, log_step, b_re, b_im, seq // SUBLANES)
    bd_re, bd_im, cd_re, cd_im = _s5_block_diag(bb_re, bb_im, c_re, c_im)
    yg = _s5_scan(zb, 0, bd_re, bd_im, cd_re, cd_im, p_re, p_im, d_skip)
    yb = _glu(yg.reshape(m, HALF), w_glu.astype(BF16))
    return _matmul([(ya.reshape(m, HALF), w_out, idx, 0), (yb, w_out, idx, 1)], res=res)


def _odd_mixer(hn, bsz, seq, idx, w_in, w_out, shift_mu, w0, w2, a0, a2, g2, k_k, k_a, r_k, ln_w, ln_b, res, prec):
    m = bsz * seq
    n_main = 3 * HALF
    n_lora = 64 + 64 + 160
    w_b = jnp.concatenate([w_in[idx, n_main + n_lora:, :],
                           jnp.pad(w_in[idx, n_main:n_main + n_lora, :], ((0, RWKV_LORA_PAD - n_lora), (0, 0)))],
                          axis=0)
    za = _matmul([(hn, w_in, idx, 0)], n=n_main, w_transposed=True).reshape(bsz, seq, n_main)
    zb = _matmul([(hn, w_b[None], 0, 0)], w_transposed=True).reshape(bsz, seq, n_main + RWKV_LORA_PAD)
    mu_main = shift_mu[:n_main].astype(F32).reshape(1, n_main)
    mu_lora = jnp.pad(shift_mu[n_main:], (0, RWKV_LORA_PAD - n_lora)).astype(F32).reshape(1, RWKV_LORA_PAD)
    w2p = jnp.pad(w2, ((0, LANES - 64), (0, 0))).astype(BF16)
    a2p = jnp.pad(a2, ((64, 0), (0, 0))).astype(BF16)
    g2p = jnp.pad(g2, ((0, RWKV_LORA_PAD - LANES - 160), (0, 0))).astype(BF16)
    yc = _rwkv(za, zb, n_main // RWKV_LORA_PAD, mu_main, mu_lora, w2p, a2p, g2p, w0, a0, k_k, k_a,
               r_k, ln_w, ln_b, prec)
    cos, sin = _rope_tables(seq)
    yd = _retention(zb, 0, cos, sin)
    return _matmul([(yc.reshape(m, HALF), w_out, idx, 0), (yd.reshape(m, HALF), w_out, idx, 1)], res=res)


def kernel(x, norm_mix, norm_ffn, norm_final, ev_w_in, ev_w_out, gdn_conv_w, gdn_a_log, gdn_dt_bias, gdn_norm_w, s5_lam_re, s5_lam_im, s5_b_re, s5_b_im, s5_c_re, s5_c_im, s5_d, s5_log_step, s5_w_glu, od_w_in, od_w_out, rwkv_shift_mu, rwkv_w0, rwkv_w2, rwkv_a0, rwkv_a2, rwkv_g2, rwkv_k_k, rwkv_k_a, rwkv_r_k, rwkv_ln_w, rwkv_ln_b, ffn_w_up, ffn_conv_w, ffn_w_down):
    bsz, seq, d = x.shape
    m = bsz * seq
    depth = norm_mix.shape[0]
    prec = None
    w_down = ffn_w_down.astype(BF16)
    ev_w_in_t = jnp.swapaxes(ev_w_in, 1, 2)
    od_w_in_t = jnp.swapaxes(od_w_in, 1, 2)
    h = x.reshape(m, d).astype(F32)
    for layer in range(depth):
        hn = _rmsnorm(h, norm_mix[layer], BF16)
        i = layer // 2
        if layer % 2 == 0:
            h = _even_mixer(hn, bsz, seq, i, ev_w_in_t, ev_w_out, gdn_conv_w[i], gdn_a_log[i], gdn_dt_bias[i],
                            gdn_norm_w[i], s5_lam_re[i], s5_lam_im[i], s5_b_re[i], s5_b_im[i], s5_c_re[i],
                            s5_c_im[i], s5_d[i], s5_log_step[i], s5_w_glu[i], h, prec)
        else:
            h = _odd_mixer(hn, bsz, seq, i, od_w_in_t, od_w_out, rwkv_shift_mu[i], rwkv_w0[i], rwkv_w2[i],
                           rwkv_a0[i], rwkv_a2[i], rwkv_g2[i], rwkv_k_k[i], rwkv_k_a[i], rwkv_r_k[i],
                           rwkv_ln_w[i], rwkv_ln_b[i], h, prec)
        hn = _rmsnorm(h, norm_ffn[layer], BF16)
        act = _ffn_up(hn, ffn_w_up, ffn_conv_w, layer, seq)
        h = _matmul_rows(act, w_down, layer, h)
    return _rmsnorm(h, norm_final, x.dtype).reshape(bsz, seq, d)
```

```python
import functools
import math

import jax
import jax.numpy as jnp
from jax import lax
from jax.experimental import pallas as pl
from jax.experimental.pallas import tpu as pltpu

F32 = jnp.float32
BF16 = jnp.bfloat16

V7X_VMEM_LIMIT_BYTES = 56 * 1024 * 1024
LANES = 128
SUBLANES = 8

EPS = 1e-6
HALF = 1024
GDN_HEADS = 8
GDN_D = 128
GDN_CONV = 4
GDN_CHUNK = 64
S5_GROUP = 16
S5_GROUPS = 64
S5_STATE = 64
S5_TILE_GROUPS = LANES // S5_GROUP
S5_TILE_STATES = S5_TILE_GROUPS * S5_STATE
RWKV_HEAD = 64
RWKV_PAIRS = HALF // LANES
RWKV_CHUNK = 64
RWKV_LORA_PAD = 384
RWKV_GN_EPS = 64e-5
RET_HEADS = 4
RET_DK = 128
RET_DV = 256
RET_CHUNK = 256
ROPE_BASE = 10000.0
INV_BLOCK = 16
GDN_CHUNKS_PER_STEP = 4
RWKV_CHUNKS_PER_STEP = 4


def _cparams(*sem):
    return pltpu.CompilerParams(dimension_semantics=sem, vmem_limit_bytes=V7X_VMEM_LIMIT_BYTES)


def _dot_dims(a, b, dims, precision):
    if precision is None:
        a, b = a.astype(BF16), b.astype(BF16)
    batch = ((), ())
    if a.ndim == 3:
        dims = tuple(tuple(d + 1 for d in side) for side in dims)
        batch = ((0,), (0,))
    return lax.dot_general(a, b, (dims, batch), preferred_element_type=F32, precision=precision)


def _dot(a, b, precision=None):
    return _dot_dims(a, b, ((1,), (0,)), precision)


def _dot_nt(a, b, precision=None):
    return _dot_dims(a, b, ((1,), (1,)), precision)


def _dot_tn(a, b, precision=None):
    return _dot_dims(a, b, ((0,), (0,)), precision)


def _time_cumsum(x):
    row = lax.broadcasted_iota(jnp.int32, x.shape, 0)
    d = 1
    while d < x.shape[0]:
        x = x + jnp.where(row >= d, pltpu.roll(x, d, 0), 0.0)
        d *= 2
    return x


def _iota2(shape):
    return lax.broadcasted_iota(jnp.int32, shape, 0), lax.broadcasted_iota(jnp.int32, shape, 1)


def _unit_lower_solve(a, rhs, row, col, sub, precision):
    assert sub == 4 * INV_BLOCK
    shift = INV_BLOCK.bit_length() - 1
    op = lambda x: x.astype(BF16) if precision is None else x
    eye = op(jnp.where(row == col, 1.0, 0.0).astype(F32))
    a = op(a)
    ad = a * op(jnp.where((row >> shift) == (col >> shift), 1.0, 0.0).astype(F32))
    ao = a - ad
    a2 = op(_dot(ad, ad, precision))
    a4 = op(_dot(a2, a2, precision))
    t2 = _dot(eye - ad, eye + a2, precision)
    a8 = op(_dot(a4, a4, precision))
    t4 = _dot(t2, eye + a4, precision)
    td = op(_dot(t4, eye + a8, precision))
    n = op(_dot(td, ao, precision))
    r = _dot(td, rhs, precision)
    n2 = op(_dot(n, n, precision))
    return _dot(_dot(eye - n, eye + n2, precision), r, precision)


def _rmsnorm_kernel(x_ref, w_ref, o_ref):
    x = x_ref[...]
    ms = jnp.mean(x * x, axis=-1, keepdims=True)
    o_ref[...] = (x * lax.rsqrt(ms + EPS) * w_ref[...]).astype(o_ref.dtype)


def _rmsnorm(x, w, out_dtype):
    m, d = x.shape
    tm = min(1024, m)
    return pl.pallas_call(
        _rmsnorm_kernel,
        grid=(m // tm,),
        in_specs=[pl.BlockSpec((tm, d), lambda i: (i, 0)), pl.BlockSpec((1, d), lambda i: (0, 0))],
        out_specs=pl.BlockSpec((tm, d), lambda i: (i, 0)),
        out_shape=jax.ShapeDtypeStruct((m, d), out_dtype),
        compiler_params=_cparams("parallel"),
        name="rmsnorm",
    )(x, w.reshape(1, d).astype(F32))


def _mm_kernel(*refs, n_pairs, has_res, w_transposed):
    n_in = 2 * n_pairs + int(has_res)
    o_ref = refs[n_in]
    wb_refs = refs[n_in + 1:]

    @pl.when(pl.program_id(1) == 0)
    def _():
        for p in range(n_pairs):
            w = refs[2 * p + 1][...]
            wb_refs[p][...] = (w.T if w_transposed else w).astype(BF16)

    acc = None
    for p in range(n_pairs):
        d = _dot(refs[2 * p][...], wb_refs[p][...])
        acc = d if acc is None else acc + d
    if has_res:
        acc = acc + refs[2 * n_pairs][...]
    o_ref[...] = acc.astype(o_ref.dtype)


def _pick_tile(n, prefs):
    for t in prefs:
        if n % t == 0:
            return t
    return n


def _matmul(pairs, n=None, res=None, out_dtype=F32, tm=1024, tn=None, w_transposed=False):
    m = pairs[0][0].shape[0]
    n = n or pairs[0][1].shape[1 if w_transposed else 2]
    tm = min(2 * tm if res is None and n % 1024 == 0 else tm, m)
    tn = tn or _pick_tile(n, (1024, 1152, 512, 384, 256, 128))
    in_specs, args, scratch = [], [], []
    for a, w, l, r in pairs:
        k = a.shape[1]
        wspec = (pl.BlockSpec((None, tn, k), lambda j, i, l=l, r=r: (l, j, r), pipeline_mode=pl.Buffered(1))
                 if w_transposed else
                 pl.BlockSpec((None, k, tn), lambda j, i, l=l, r=r: (l, r, j), pipeline_mode=pl.Buffered(1)))
        in_specs += [pl.BlockSpec((tm, k), lambda j, i: (i, 0)), wspec]
        args += [a, w]
        scratch.append(pltpu.VMEM((k, tn), BF16))
    if res is not None:
        in_specs.append(pl.BlockSpec((tm, tn), lambda j, i: (i, j)))
        args.append(res)
    return pl.pallas_call(
        functools.partial(_mm_kernel, n_pairs=len(pairs), has_res=res is not None, w_transposed=w_transposed),
        grid=(n // tn, m // tm),
        in_specs=in_specs,
        out_specs=pl.BlockSpec((tm, tn), lambda j, i: (i, j)),
        out_shape=jax.ShapeDtypeStruct((m, n), out_dtype),
        scratch_shapes=scratch,
        compiler_params=_cparams("parallel", "arbitrary"),
        name="matmul",
    )(*args)


def _mm_rows_kernel(a_ref, w_ref, res_ref, o_ref):
    o_ref[...] = (_dot(a_ref[...], w_ref[...]) + res_ref[...]).astype(o_ref.dtype)


def _matmul_rows(a, w, layer, res, tm=1024, tn=512):
    m, k = a.shape
    n = w.shape[2]
    tm = min(tm, m)
    return pl.pallas_call(
        _mm_rows_kernel,
        grid=(m // tm, n // tn),
        in_specs=[pl.BlockSpec((tm, k), lambda i, j: (i, 0)),
                  pl.BlockSpec((None, k, tn), lambda i, j: (layer, 0, j)),
                  pl.BlockSpec((tm, tn), lambda i, j: (i, j))],
        out_specs=pl.BlockSpec((tm, tn), lambda i, j: (i, j)),
        out_shape=jax.ShapeDtypeStruct((m, n), res.dtype),
        compiler_params=_cparams("parallel", "parallel"),
        name="matmul_rows",
    )(a, w, res)


def _ffn_up_kernel(h_ref, wg_ref, wv_ref, cg_ref, cv_ref, o_ref, carry_ref, wgb_ref, wvb_ref, *, blocks_per_seq):
    i = pl.program_id(1)

    @pl.when(i == 0)
    def _():
        wgb_ref[...] = wg_ref[...].astype(BF16)
        wvb_ref[...] = wv_ref[...].astype(BF16)

    @pl.when(i % blocks_per_seq == 0)
    def _():
        carry_ref[...] = jnp.zeros_like(carry_ref)

    h = h_ref[...]
    zg = _dot(h, wgb_ref[...])
    zv = _dot(h, wvb_ref[...])
    tm = zg.shape[0]
    row = lax.broadcasted_iota(jnp.int32, zg.shape, 0)

    def conv(z, c_ref, prev):
        m1 = jnp.where(row == 0, prev[7:8, :], pltpu.roll(z, 1, 0))
        m2 = jnp.where(row == 0, prev[6:7, :], jnp.where(row == 1, prev[7:8, :], pltpu.roll(z, 2, 0)))
        c = c_ref[...]
        return c[0:1, :] * m2 + c[1:2, :] * m1 + c[2:3, :] * z

    g = conv(zg, cg_ref, carry_ref[0])
    v = conv(zv, cv_ref, carry_ref[1])
    carry_ref[0] = zg[tm - SUBLANES:, :]
    carry_ref[1] = zv[tm - SUBLANES:, :]
    o_ref[...] = (g * jax.nn.sigmoid(g) * v).astype(o_ref.dtype)


def _ffn_up(hn, w_up, conv_w, layer, seq):
    m, d = hn.shape
    f = w_up.shape[2] // 2
    tm = min(1024, seq)
    tn = _pick_tile(f, (512, 256, 128))
    nj = f // tn
    return pl.pallas_call(
        functools.partial(_ffn_up_kernel, blocks_per_seq=seq // tm),
        grid=(nj, m // tm),
        in_specs=[
            pl.BlockSpec((tm, d), lambda j, i: (i, 0)),
            pl.BlockSpec((None, d, tn), lambda j, i: (layer, 0, j)),
            pl.BlockSpec((None, d, tn), lambda j, i: (layer, 0, j + nj)),
            pl.BlockSpec((None, 3, tn), lambda j, i: (layer, 0, j)),
            pl.BlockSpec((None, 3, tn), lambda j, i: (layer, 0, j + nj)),
        ],
        out_specs=pl.BlockSpec((tm, tn), lambda j, i: (i, j)),
        out_shape=jax.ShapeDtypeStruct((m, f), BF16),
        scratch_shapes=[pltpu.VMEM((2, SUBLANES, tn), F32), pltpu.VMEM((d, tn), BF16), pltpu.VMEM((d, tn), BF16)],
        compiler_params=_cparams("parallel", "arbitrary"),
        name="ffn_up_conv",
    )(hn, w_up, w_up, conv_w, conv_w)


def _gdn_kernel(z_ref, zs_ref, cw_ref, alog_ref, dtb_ref, nw_ref, o_ref, s_ref, prev_ref, *, prec):
    c = GDN_CHUNK
    nqkv = 3 * HALF
    rows = z_ref.shape[1]
    nsub = rows // c

    @pl.when(pl.program_id(1) == 0)
    def _():
        s_ref[...] = jnp.zeros_like(s_ref)
        prev_ref[...] = jnp.zeros_like(prev_ref)

    z = z_ref[0, :, :nqkv]
    prev = prev_ref[...]
    cw = cw_ref[...]
    row8 = lax.broadcasted_iota(jnp.int32, (SUBLANES, nqkv), 0)
    acc = cw[GDN_CONV - 1:GDN_CONV, :] * z
    for tap in range(1, GDN_CONV):
        zr = pltpu.roll(z, tap, 0)
        head = jnp.where(row8 < tap, pltpu.roll(prev, tap, 0), zr[:SUBLANES, :])
        acc = acc + cw[GDN_CONV - 1 - tap:GDN_CONV - tap, :] * jnp.concatenate([head, zr[SUBLANES:, :]], axis=0)
    prev_ref[...] = z[rows - SUBLANES:, :]
    x = acc * jax.nn.sigmoid(acc)

    zs = zs_ref[0]
    beta_all = jax.nn.sigmoid(zs)
    g_all = -jnp.exp(alog_ref[...]) * jax.nn.softplus(zs + dtb_ref[...])
    row, col = _iota2((c, c))
    nh = GDN_HEADS
    units = [(sub * c, h) for sub in range(nsub) for h in range(nh)]
    per_unit = lambda off: jnp.stack([x[t0:t0 + c, off + h * GDN_D:off + (h + 1) * GDN_D] for t0, h in units])
    l2n = lambda t: t * lax.rsqrt(jnp.sum(t * t, axis=-1, keepdims=True) + 1e-6)
    q = l2n(per_unit(0)) * (GDN_D ** -0.5)
    k = l2n(per_unit(HALF))
    v = per_unit(2 * HALF)
    beta = jnp.stack([beta_all[t0:t0 + c, h:h + 1] for t0, h in units])
    gcum = [_time_cumsum(g_all[sub * c:(sub + 1) * c, :]) for sub in range(nsub)]
    gcum_t = [jnp.concatenate([gs, jnp.zeros((LANES - c, LANES), F32)], axis=0).T for gs in gcum]
    gc_col = [gcum[t0 // c][:, nh + h:nh + h + 1] for t0, h in units]
    diff = jnp.stack([gc_col[i] - gcum_t[t0 // c][nh + h:nh + h + 1, :c] for i, (t0, h) in enumerate(units)])
    decay = jnp.where(row >= col, jnp.exp(diff), 0.0)
    gc = jnp.stack([jnp.broadcast_to(gcol, (c, GDN_D)) for gcol in gc_col])
    gl = gc[:, c - 1:c, :]
    egc = jnp.exp(gc)
    kb = k * beta
    kk = _dot_nt(jnp.concatenate([kb, q], axis=1), k, prec)
    a = jnp.where(row > col, kk[:, :c, :] * decay, 0.0)
    attn = kk[:, c:, :] * decay
    sol = _unit_lower_solve(a, jnp.concatenate([v * beta, kb * egc], axis=2), row, col, c, prec)
    u, w = sol[:, :, :GDN_D], sol[:, :, GDN_D:]
    qa = jnp.concatenate([q * egc, attn], axis=2)
    kdec = k * jnp.exp(gl - gc)
    egl = jnp.exp(gl)
    s = s_ref[...]
    for sub in range(nsub):
        b0, b1 = sub * nh, (sub + 1) * nh
        v_new = u[b0:b1] - _dot(w[b0:b1], s, prec)
        o = _dot(qa[b0:b1], jnp.concatenate([s, v_new], axis=1), prec)
        s = s * egl[b0:b1] + _dot_tn(kdec[b0:b1], v_new, prec)
        o = o * lax.rsqrt(jnp.mean(o * o, axis=-1, keepdims=True) + EPS) * nw_ref[...]
        for h in range(nh):
            gate = z_ref[0, sub * c:(sub + 1) * c, nqkv + h * GDN_D:nqkv + (h + 1) * GDN_D]
            o_ref[0, sub * c:(sub + 1) * c, h * GDN_D:(h + 1) * GDN_D] = (
                o[h] * gate * jax.nn.sigmoid(gate)).astype(o_ref.dtype)
    s_ref[...] = s


def _gdn(z3, zs3, zs_blk, conv_w, a_log, dt_bias, norm_w, prec):
    b, t, nz = z3.shape
    c = GDN_CHUNKS_PER_STEP * GDN_CHUNK
    pad = lambda p: jnp.zeros((1, LANES), F32).at[0, GDN_HEADS:2 * GDN_HEADS].set(p.astype(F32))
    vec = pl.BlockSpec((1, LANES), lambda bi, ni: (0, 0))
    return pl.pallas_call(
        functools.partial(_gdn_kernel, prec=prec),
        grid=(b, t // c),
        in_specs=[pl.BlockSpec((1, c, nz), lambda bi, ni: (bi, ni, 0)),
                  pl.BlockSpec((1, c, LANES), lambda bi, ni: (bi, ni, zs_blk)),
                  pl.BlockSpec((GDN_CONV, 3 * HALF), lambda bi, ni: (0, 0)),
                  vec, vec, vec],
        out_specs=pl.BlockSpec((1, c, HALF), lambda bi, ni: (bi, ni, 0)),
        out_shape=jax.ShapeDtypeStruct((b, t, HALF), BF16),
        scratch_shapes=[pltpu.VMEM((GDN_HEADS, GDN_D, GDN_D), F32), pltpu.VMEM((SUBLANES, 3 * HALF), F32)],
        compiler_params=_cparams("parallel", "arbitrary"),
        name="gdn_chunk",
    )(z3, zs3, conv_w.astype(F32), pad(a_log), pad(dt_bias), norm_w.reshape(1, GDN_D).astype(F32))


def _s5_param_kernel(lr_ref, li_ref, dt_ref, br_ref, bi_ref, ar_ref, ai_ref, bbr_ref, bbi_ref):
    lr, li, dt = lr_ref[...], li_ref[...], dt_ref[...]
    step = jnp.exp(dt)
    mag = jnp.exp(lr * step)
    ang = li * step
    ab_re, ab_im = mag * jnp.cos(ang), mag * jnp.sin(ang)
    den = lr * lr + li * li
    nr = ab_re - 1.0
    f_re = (nr * lr + ab_im * li) / den
    f_im = (ab_im * lr - nr * li) / den
    br, bi = br_ref[...], bi_ref[...]
    bbr_ref[...] = f_re * br - f_im * bi
    bbi_ref[...] = f_re * bi + f_im * br
    n = ab_re.shape[1]
    cmul = lambda xr, xi, yr, yi: (xr * yr - xi * yi, xr * yi + xi * yr)
    row = lax.broadcasted_iota(jnp.int32, (SUBLANES, n), 0)
    cur = (ab_re, ab_im)
    pr = jnp.broadcast_to(ab_re, (SUBLANES, n))
    pi = jnp.broadcast_to(ab_im, (SUBLANES, n))
    for r in range(1, SUBLANES):
        cur = cmul(cur[0], cur[1], ab_re, ab_im)
        pr = jnp.where(row == r, cur[0], pr)
        pi = jnp.where(row == r, cur[1], pi)
    rows = SUBLANES
    while rows < ar_ref.shape[0]:
        tr, ti = cmul(pr, pi, pr[rows - 1:rows, :], pi[rows - 1:rows, :])
        pr = jnp.concatenate([pr, tr], axis=0)
        pi = jnp.concatenate([pi, ti], axis=0)
        rows *= 2
    ar_ref[...] = pr
    ai_ref[...] = pi


def _s5_params(lam_re, lam_im, log_step, b_re, b_im, n_pow):
    assert n_pow >= SUBLANES and n_pow & (n_pow - 1) == 0
    gp = S5_GROUPS * S5_STATE
    row = lambda x: x.astype(F32).reshape(1, gp)
    bt = lambda x: jnp.transpose(x.astype(F32), (2, 0, 1)).reshape(S5_GROUP, gp)
    pshape = jax.ShapeDtypeStruct((n_pow, gp), F32)
    mshape = jax.ShapeDtypeStruct((S5_GROUP, gp), F32)
    return pl.pallas_call(
        _s5_param_kernel,
        out_shape=[pshape, pshape, mshape, mshape],
        name="s5_params",
    )(row(lam_re), row(lam_im), row(jnp.repeat(log_step[:, None], S5_STATE, axis=1)), bt(b_re), bt(b_im))


def _s5_scan_kernel(u_ref, bdr_ref, bdi_ref, cdr_ref, cdi_ref, pr_ref, pi_ref, d_ref, o_ref,
                    up_ref, xr_ref, xi_ref, y_ref):
    t = u_ref.shape[1]
    n = S5_TILE_STATES
    nk = t // SUBLANES
    rt = min(512, t)

    def permute_in(k, _):
        r0 = pl.multiple_of(k * SUBLANES, SUBLANES)
        up_ref[pl.ds(r0, SUBLANES), :] = u_ref[0, pl.ds(k, SUBLANES, stride=nk), :]
        return 0

    lax.fori_loop(0, nk, permute_in, 0, unroll=SUBLANES)

    for r0 in range(0, t, rt):
        ub = up_ref[r0:r0 + rt, :].astype(BF16)
        xr_ref[r0:r0 + rt, :] = _dot(ub, bdr_ref[0])
        xi_ref[r0:r0 + rt, :] = _dot(ub, bdi_ref[0])

    ar = jnp.broadcast_to(pr_ref[0:1, :], (SUBLANES, n))
    ai = jnp.broadcast_to(pi_ref[0:1, :], (SUBLANES, n))

    def local_scan(k, carry):
        cr, ci = carry
        r0 = pl.multiple_of(k * SUBLANES, SUBLANES)
        xr = xr_ref[pl.ds(r0, SUBLANES), :] + (ar * cr - ai * ci)
        xi = xi_ref[pl.ds(r0, SUBLANES), :] + (ar * ci + ai * cr)
        xr_ref[pl.ds(r0, SUBLANES), :] = xr
        xi_ref[pl.ds(r0, SUBLANES), :] = xi
        return xr, xi

    zero = jnp.zeros((SUBLANES, n), F32)
    fr, fi = lax.fori_loop(0, nk, local_scan, (zero, zero), unroll=SUBLANES)

    row = lax.broadcasted_iota(jnp.int32, (SUBLANES, n), 0)
    cmul = lambda xr, xi, yr, yi: (xr * yr - xi * yi, xr * yi + xi * yr)
    gr = jnp.broadcast_to(pr_ref[nk - 1:nk, :], (SUBLANES, n))
    gi = jnp.broadcast_to(pi_ref[nk - 1:nk, :], (SUBLANES, n))
    for d in (1, 2, 4):
        sr = jnp.where(row >= d, pltpu.roll(fr, d, 0), 0.0)
        si = jnp.where(row >= d, pltpu.roll(fi, d, 0), 0.0)
        tr, ti = cmul(gr, gi, sr, si)
        fr, fi = fr + tr, fi + ti
        gr, gi = cmul(gr, gi, gr, gi)
    cr = jnp.where(row >= 1, pltpu.roll(fr, 1, 0), 0.0)
    ci = jnp.where(row >= 1, pltpu.roll(fi, 1, 0), 0.0)

    def add_carry(k8, _):
        p0 = pl.multiple_of(k8 * SUBLANES, SUBLANES)
        pr8 = pr_ref[pl.ds(p0, SUBLANES), :]
        pi8 = pi_ref[pl.ds(p0, SUBLANES), :]
        for j in range(SUBLANES):
            r0 = pl.multiple_of((k8 * SUBLANES + j) * SUBLANES, SUBLANES)
            pr = jnp.broadcast_to(pr8[j:j + 1, :], (SUBLANES, n))
            pi = jnp.broadcast_to(pi8[j:j + 1, :], (SUBLANES, n))
            xr_ref[pl.ds(r0, SUBLANES), :] = xr_ref[pl.ds(r0, SUBLANES), :] + (pr * cr - pi * ci)
            xi_ref[pl.ds(r0, SUBLANES), :] = xi_ref[pl.ds(r0, SUBLANES), :] + (pr * ci + pi * cr)
        return 0

    lax.fori_loop(0, nk // SUBLANES, add_carry, 0)

    for r0 in range(0, t, rt):
        y = (_dot(xr_ref[r0:r0 + rt, :].astype(BF16), cdr_ref[0])
             - _dot(xi_ref[r0:r0 + rt, :].astype(BF16), cdi_ref[0]) + up_ref[r0:r0 + rt, :] * d_ref[...])
        y_ref[r0:r0 + rt, :] = jax.nn.gelu(y)

    def permute_out(k, _):
        r0 = pl.multiple_of(k * SUBLANES, SUBLANES)
        o_ref[0, pl.ds(k, SUBLANES, stride=nk), :] = y_ref[pl.ds(r0, SUBLANES), :]
        return 0

    lax.fori_loop(0, nk, permute_out, 0, unroll=SUBLANES)


def _s5_scan(z3, u_off, bd_re, bd_im, cd_re, cd_im, p_re, p_im, d_skip):
    b, t, _ = z3.shape
    nt = HALF // LANES
    n = S5_TILE_STATES
    nk = t // SUBLANES
    assert p_re.shape[0] == nk
    return pl.pallas_call(
        _s5_scan_kernel,
        grid=(b, nt),
        in_specs=[
            pl.BlockSpec((1, t, LANES), lambda bi, j: (bi, 0, j + u_off)),
            pl.BlockSpec((1, LANES, n), lambda bi, j: (j, 0, 0)),
            pl.BlockSpec((1, LANES, n), lambda bi, j: (j, 0, 0)),
            pl.BlockSpec((1, n, LANES), lambda bi, j: (j, 0, 0)),
            pl.BlockSpec((1, n, LANES), lambda bi, j: (j, 0, 0)),
            pl.BlockSpec((nk, n), lambda bi, j: (0, j)),
            pl.BlockSpec((nk, n), lambda bi, j: (0, j)),
            pl.BlockSpec((1, LANES), lambda bi, j: (0, j)),
        ],
        out_specs=pl.BlockSpec((1, t, LANES), lambda bi, j: (bi, 0, j)),
        out_shape=jax.ShapeDtypeStruct((b, t, HALF), F32),
        scratch_shapes=[pltpu.VMEM((t, LANES), F32), pltpu.VMEM((t, n), F32), pltpu.VMEM((t, n), F32),
                        pltpu.VMEM((t, LANES), F32)],
        compiler_params=_cparams("parallel", "parallel"),
        name="s5_scan",
    )(z3, bd_re, bd_im, cd_re, cd_im, p_re, p_im, d_skip.reshape(1, HALF).astype(F32))


def _glu_kernel(y_ref, w_ref, yt_ref, o_ref):
    gate = _dot(y_ref[...].astype(BF16), w_ref[...])
    o_ref[...] = (yt_ref[...] * jax.nn.sigmoid(gate)).astype(o_ref.dtype)


def _glu(y, w):
    m, k = y.shape
    tm = min(1024, m)
    tn = 512
    return pl.pallas_call(
        _glu_kernel,
        grid=(m // tm, k // tn),
        in_specs=[pl.BlockSpec((tm, k), lambda i, j: (i, 0)),
                  pl.BlockSpec((k, tn), lambda i, j: (0, j)),
                  pl.BlockSpec((tm, tn), lambda i, j: (i, j))],
        out_specs=pl.BlockSpec((tm, tn), lambda i, j: (i, j)),
        out_shape=jax.ShapeDtypeStruct((m, k), BF16),
        compiler_params=_cparams("parallel", "parallel"),
        name="s5_glu",
    )(y, w, y)


def _s5_block_diag(bb_re, bb_im, c_re, c_im):
    nt, tg = HALF // LANES, S5_TILE_GROUPS
    eye = jnp.eye(tg, dtype=F32)

    def bmap(bb):
        x = bb.reshape(S5_GROUP, nt, tg, S5_STATE)
        x = jnp.einsum('cjgp,gh->jgchp', x, eye)
        return x.reshape(nt, LANES, S5_TILE_STATES).astype(BF16)

    def cmap(cc):
        x = cc.astype(F32).reshape(nt, tg, S5_GROUP, S5_STATE)
        x = jnp.einsum('jgcp,gh->jgphc', x, eye)
        return x.reshape(nt, S5_TILE_STATES, LANES).astype(BF16)

    return bmap(bb_re), bmap(bb_im), cmap(c_re), cmap(c_im)


def _rwkv_kernel(z_ref, zl_ref, mum_ref, mul_ref, w2_ref, a2_ref, g2_ref, w0_ref, a0_ref, kkp_ref, ka_ref,
                 rk_ref, lnw_ref, lnb_ref, o_ref, h_ref, prevm_ref, prevl_ref, *, prec):
    c = RWKV_CHUNK
    c2 = 2 * c
    rows = z_ref.shape[1]
    nsub = rows // c

    @pl.when(pl.program_id(1) == 0)
    def _():
        h_ref[...] = jnp.zeros_like(h_ref)
        prevm_ref[...] = jnp.zeros_like(prevm_ref)
        prevl_ref[...] = jnp.zeros_like(prevl_ref)

    def shift_mix(z, prev_ref, mu_ref):
        row0 = lax.broadcasted_iota(jnp.int32, z.shape, 0) == 0
        zm1 = jnp.where(row0, prev_ref[SUBLANES - 1:SUBLANES, :], pltpu.roll(z, 1, 0))
        prev_ref[...] = z[rows - SUBLANES:, :]
        return z + (zm1 - z) * mu_ref[...]

    x = shift_mix(z_ref[0], prevm_ref, mum_ref)
    zl = shift_mix(zl_ref[0], prevl_ref, mul_ref)
    r_all, k_raw, v_all = x[:, :HALF], x[:, HALF:2 * HALF], x[:, 2 * HALF:]
    wa = zl[:, :LANES]
    w = w0_ref[...] + _dot(jnp.tanh(wa), w2_ref[...])
    w = -jax.nn.softplus(-w) - 0.5
    lw_all = -jnp.exp(w)
    a = jax.nn.sigmoid(a0_ref[...] + _dot(wa, a2_ref[...]))
    g_all = _dot(jax.nn.sigmoid(zl[:, LANES:]), g2_ref[...])
    k_all = k_raw * (1.0 + (a - 1.0) * ka_ref[...])

    row, col = _iota2((c2, c2))
    same = (row >> 6) == (col >> 6)
    strict = jnp.where(jnp.logical_and(same, row > col), 1.0, 0.0).astype(BF16)
    incl = jnp.where(jnp.logical_and(same, row >= col), 1.0, 0.0).astype(BF16)
    lane = lax.broadcasted_iota(jnp.int32, (c, LANES), 1)
    first_head = lane < RWKV_HEAD
    head0 = jnp.where(first_head, 1.0, 0.0).astype(BF16)
    head1 = jnp.where(first_head, 0.0, 1.0).astype(BF16)

    def stack2(x):
        xb = x.astype(BF16)
        return jnp.concatenate([xb * head0, xb * head1], axis=1)

    npair = RWKV_PAIRS
    units = [(sub * c, p * LANES) for sub in range(nsub) for p in range(npair)]
    nu = len(units)
    per_pair = lambda x: jnp.stack([x[t0:t0 + c, l0:l0 + LANES] for t0, l0 in units])
    per_pair_vec = lambda x: jnp.stack([x[:, l0:l0 + LANES] for _, l0 in units])
    def seg_sum(x):
        sa = jnp.sum(jnp.where(first_head, x, 0.0), axis=-1, keepdims=True)
        sb = jnp.sum(jnp.where(first_head, 0.0, x), axis=-1, keepdims=True)
        return jnp.where(first_head, sa, sb)
    r, lw, k, v = per_pair(r_all), per_pair(lw_all), per_pair(k_all), per_pair(v_all)
    kk = per_pair(k_raw * kkp_ref[...])
    kk = kk * lax.rsqrt(seg_sum(kk * kk) + 1e-6)
    b = kk * per_pair(a)
    cl = per_pair(jnp.concatenate([_time_cumsum(lw_all[sub * c:(sub + 1) * c, :]) for sub in range(nsub)], axis=0))
    cl_last = cl[:, c - 1:c, :]
    e_neg = jnp.exp(-cl)
    e_tail = jnp.exp(cl_last - cl)
    kk2 = stack2(kk * jnp.exp(cl - lw))
    r2 = stack2(r * jnp.exp(cl))
    b2 = stack2(b * e_neg)
    k2 = stack2(k * e_neg)
    v2 = stack2(v)
    bd2 = stack2(b * e_tail)
    kd2 = stack2(k * e_tail)
    sc = _dot_nt(jnp.concatenate([kk2, r2], axis=1), jnp.concatenate([b2, k2], axis=1), prec).astype(BF16)
    a_ab = sc[:, :c2, :c2] * strict
    a_ak = sc[:, :c2, c2:] * strict
    r_bk = jnp.concatenate([sc[:, c2:, :c2] * incl, sc[:, c2:, c2:] * incl], axis=2)
    rhs = jnp.concatenate([kk2, _dot(a_ak, v2, prec).astype(BF16)], axis=2)
    wt = _unit_lower_solve(a_ab, rhs, row, col, c, prec)
    wk, tv = wt[:, :, :LANES], wt[:, :, LANES:]
    wr = jnp.concatenate([wk.astype(BF16), r2], axis=1)
    bkd2 = jnp.concatenate([bd2, kd2], axis=1)
    e_last = jnp.exp(cl_last)
    ht = h_ref[...]
    y2 = []
    for sub in range(nsub):
        b0, b1 = sub * npair, (sub + 1) * npair
        hp = _dot_nt(wr[b0:b1], ht, prec)
        u = -hp[:, :c2, :] - tv[b0:b1]
        uv = jnp.concatenate([u.astype(BF16), v2[b0:b1]], axis=1)
        y2.append(hp[:, c2:, :] + _dot(r_bk[b0:b1], uv, prec))
        ht = ht * e_last[b0:b1] + _dot_tn(uv, bkd2[b0:b1], prec)
    h_ref[...] = ht
    y2 = jnp.concatenate(y2, axis=0)
    y = y2[:, :c, :] + y2[:, c:, :]
    mu = seg_sum(y) * (1.0 / RWKV_HEAD)
    d = y - mu
    var = seg_sum(d * d) * (1.0 / RWKV_HEAD)
    yn = d * lax.rsqrt(var + RWKV_GN_EPS) * per_pair_vec(lnw_ref[...]) + per_pair_vec(lnb_ref[...])
    out = (yn + seg_sum(r * k * per_pair_vec(rk_ref[...])) * v) * per_pair(g_all)
    for i, (t0, l0) in enumerate(units):
        o_ref[0, t0:t0 + c, l0:l0 + LANES] = out[i].astype(o_ref.dtype)


def _rwkv(z3, zl3, lora_blk, mu_main, mu_lora, w2p, a2p, g2p, w0, a0, k_k, k_a, r_k, ln_w, ln_b, prec):
    bsz, t, nz = z3.shape
    c = RWKV_CHUNKS_PER_STEP * RWKV_CHUNK
    lw = RWKV_LORA_PAD
    vec = pl.BlockSpec((1, HALF), lambda bi, ni: (0, 0))
    full = lambda rows: pl.BlockSpec((rows, HALF), lambda bi, ni: (0, 0))
    row = lambda x: x.astype(F32).reshape(1, HALF)
    return pl.pallas_call(
        functools.partial(_rwkv_kernel, prec=prec),
        grid=(bsz, t // c),
        in_specs=[pl.BlockSpec((1, c, nz), lambda bi, ni: (bi, ni, 0)),
                  pl.BlockSpec((1, c, lw), lambda bi, ni: (bi, ni, lora_blk)),
                  pl.BlockSpec((1, nz), lambda bi, ni: (0, 0)),
                  pl.BlockSpec((1, lw), lambda bi, ni: (0, 0)),
                  full(LANES), full(LANES), full(lw - LANES)] + [vec] * 7,
        out_specs=pl.BlockSpec((1, c, HALF), lambda bi, ni: (bi, ni, 0)),
        out_shape=jax.ShapeDtypeStruct((bsz, t, HALF), BF16),
        scratch_shapes=[pltpu.VMEM((RWKV_PAIRS, LANES, LANES), F32), pltpu.VMEM((SUBLANES, nz), F32),
                        pltpu.VMEM((SUBLANES, lw), F32)],
        compiler_params=_cparams("parallel", "arbitrary"),
        name="rwkv_chunk",
    )(z3, zl3, mu_main, mu_lora, w2p, a2p, g2p, row(w0), row(a0), row(k_k), row(k_a), row(r_k), row(ln_w),
      row(ln_b))


def _rope_kernel(f_ref, cos_ref, sin_ref):
    t = cos_ref.shape[0]
    pos = lax.broadcasted_iota(jnp.int32, (t, LANES), 0).astype(F32)
    lane = lax.broadcasted_iota(jnp.int32, (t, LANES), 1)
    ang = pos * f_ref[...]
    cos_ref[...] = jnp.cos(ang)
    sin_ref[...] = jnp.where(lane < RET_DK // 2, -1.0, 1.0) * jnp.sin(ang)


def _rope_tables(t):
    inv_freq = ROPE_BASE ** (-jnp.linspace(0.0, 1.0, RET_DK // 2, dtype=F32))
    f2 = jnp.concatenate([inv_freq, inv_freq]).reshape(1, RET_DK)
    shape = jax.ShapeDtypeStruct((t, RET_DK), F32)
    return pl.pallas_call(_rope_kernel, out_shape=[shape, shape], name="rope_tables")(f2)


def _ret_kernel(q_ref, k_ref, v_ref, gate_ref, cos_ref, sin_ref, o_ref, s_ref):
    c = RET_CHUNK

    @pl.when(pl.program_id(1) == 0)
    def _():
        s_ref[...] = jnp.zeros_like(s_ref)

    row, col = _iota2((c, c))
    dist = (row - col).astype(F32)
    idx = lax.broadcasted_iota(jnp.int32, (c, 1), 0).astype(F32)
    cos, sin = cos_ref[...], sin_ref[...]
    rot = lambda x: x * cos + pltpu.roll(x, RET_DK // 2, 1) * sin

    for h in range(RET_HEADS):
        log_g = math.log(1.0 - 2.0 ** (-5.0 - h))
        q = rot(q_ref[0, :, h * RET_DK:(h + 1) * RET_DK])
        k = rot(k_ref[0, :, h * RET_DK:(h + 1) * RET_DK]) * (RET_DK ** -0.5)
        v = v_ref[0, :, h * RET_DV:(h + 1) * RET_DV]
        vb = v.astype(BF16)
        dmask = jnp.where(row >= col, jnp.exp(log_g * dist), 0.0)
        sc = _dot_nt(q.astype(BF16), k.astype(BF16)) * dmask
        s = s_ref[h]
        o = _dot(sc.astype(BF16), vb) + _dot((q * jnp.exp(log_g * (idx + 1.0))).astype(BF16), s.astype(BF16))
        kd = k * jnp.exp(log_g * (c - 1.0 - idx))
        s_ref[h] = s * math.exp(log_g * c) + _dot_tn(kd.astype(BF16), vb)
        o = o * lax.rsqrt(jnp.mean(o * o, axis=-1, keepdims=True) + EPS)
        gate = gate_ref[0, :, h * RET_DV:(h + 1) * RET_DV]
        o_ref[0, :, h * RET_DV:(h + 1) * RET_DV] = (o * gate * jax.nn.sigmoid(gate)).astype(o_ref.dtype)


def _retention(z3, q_blk, cos, sin):
    b, t, _ = z3.shape
    c = RET_CHUNK
    qk = RET_HEADS * RET_DK
    v_blk = (q_blk * qk + 2 * qk) // HALF
    return pl.pallas_call(
        _ret_kernel,
        grid=(b, t // c),
        in_specs=[
            pl.BlockSpec((1, c, qk), lambda bi, ni: (bi, ni, q_blk)),
            pl.BlockSpec((1, c, qk), lambda bi, ni: (bi, ni, q_blk + 1)),
            pl.BlockSpec((1, c, HALF), lambda bi, ni: (bi, ni, v_blk)),
            pl.BlockSpec((1, c, HALF), lambda bi, ni: (bi, ni, v_blk + 1)),
            pl.BlockSpec((c, RET_DK), lambda bi, ni: (ni, 0)),
            pl.BlockSpec((c, RET_DK), lambda bi, ni: (ni, 0)),
        ],
        out_specs=pl.BlockSpec((1, c, HALF), lambda bi, ni: (bi, ni, 0)),
        out_shape=jax.ShapeDtypeStruct((b, t, HALF), BF16),
        scratch_shapes=[pltpu.VMEM((RET_HEADS, RET_DK, RET_DV), F32)],
        compiler_params=_cparams("parallel", "arbitrary"),
        name="retention_chunk",
    )(z3, z3, z3, z3, cos, sin)


def _even_mixer(hn, bsz, seq, idx, w_in, w_out, conv_w, a_log, dt_bias, norm_w, lam_re, lam_im, b_re, b_im,
                c_re, c_im, d_skip, log_step, w_glu, res, prec):
    m = bsz * seq
    n_main = 4 * HALF
    n_small = 2 * GDN_HEADS
    w_b = jnp.concatenate([w_in[idx, n_main + n_small:, :],
                           jnp.pad(w_in[idx, n_main:n_main + n_small, :], ((0, LANES - n_small), (0, 0)))], axis=0)
    za = _matmul([(hn, w_in, idx, 0)], n=n_main, w_transposed=True).reshape(bsz, seq, n_main)
    zb = _matmul([(hn, w_b[None], 0, 0)], w_transposed=True).reshape(bsz, seq, HALF + LANES)
    ya = _gdn(za, zb, HALF // LANES, conv_w, a_log, dt_bias, norm_w, prec)
    p_re, p_im, bb_re, bb_im = _s5_params(lam_re, lam_im, log_step, b_re, b_im, seq // SUBLANES)
    bd_re, bd_im, cd_re, cd_im = _s5_block_diag(bb_re, bb_im, c_re, c_im)
    yg = _s5_scan(zb, 0, bd_re, bd_im, cd_re, cd_im, p_re, p_im, d_skip)
    yb = _glu(yg.reshape(m, HALF), w_glu.astype(BF16))
    return _matmul([(ya.reshape(m, HALF), w_out, idx, 0), (yb, w_out, idx, 1)], res=res)


def _odd_mixer(hn, bsz, seq, idx, w_in, w_out, shift_mu, w0, w2, a0, a2, g2, k_k, k_a, r_k, ln_w, ln_b, res, prec):
    m = bsz * seq
    n_main = 3 * HALF
    n_lora = 64 + 64 + 160
    w_b = jnp.concatenate([w_in[idx, n_main + n_lora:, :],
                           jnp.pad(w_in[idx, n_main:n_main + n_lora, :], ((0, RWKV_LORA_PAD - n_lora), (0, 0)))],
                          axis=0)
    za = _matmul([(hn, w_in, idx, 0)], n=n_main, w_transposed=True).reshape(bsz, seq, n_main)
    zb = _matmul([(hn, w_b[None], 0, 0)], w_transposed=True).reshape(bsz, seq, n_main + RWKV_LORA_PAD)
    mu_main = shift_mu[:n_main].astype(F32).reshape(1, n_main)
    mu_lora = jnp.pad(shift_mu[n_main:], (0, RWKV_LORA_PAD - n_lora)).astype(F32).reshape(1, RWKV_LORA_PAD)
    w2p = jnp.pad(w2, ((0, LANES - 64), (0, 0))).astype(BF16)
    a2p = jnp.pad(a2, ((64, 0), (0, 0))).astype(BF16)
    g2p = jnp.pad(g2, ((0, RWKV_LORA_PAD - LANES - 160), (0, 0))).astype(BF16)
    yc = _rwkv(za, zb, n_main // RWKV_LORA_PAD, mu_main, mu_lora, w2p, a2p, g2p, w0, a0, k_k, k_a,
               r_k, ln_w, ln_b, prec)
    cos, sin = _rope_tables(seq)
    yd = _retention(zb, 0, cos, sin)
    return _matmul([(yc.reshape(m, HALF), w_out, idx, 0), (yd.reshape(m, HALF), w_out, idx, 1)], res=res)


def kernel(x, norm_mix, norm_ffn, norm_final, ev_w_in, ev_w_out, gdn_conv_w, gdn_a_log, gdn_dt_bias, gdn_norm_w, s5_lam_re, s5_lam_im, s5_b_re, s5_b_im, s5_c_re, s5_c_im, s5_d, s5_log_step, s5_w_glu, od_w_in, od_w_out, rwkv_shift_mu, rwkv_w0, rwkv_w2, rwkv_a0, rwkv_a2, rwkv_g2, rwkv_k_k, rwkv_k_a, rwkv_r_k, rwkv_ln_w, rwkv_ln_b, ffn_w_up, ffn_conv_w, ffn_w_down):
    bsz, seq, d = x.shape
    m = bsz * seq
    depth = norm_mix.shape[0]
    prec = None
    w_down = ffn_w_down.astype(BF16)
    ev_w_in_t = jnp.swapaxes(ev_w_in, 1, 2)
    od_w_in_t = jnp.swapaxes(od_w_in, 1, 2)
    h = x.reshape(m, d).astype(F32)
    for layer in range(depth):
        hn = _rmsnorm(h, norm_mix[layer], BF16)
        i = layer // 2
        if layer % 2 == 0:
            h = _even_mixer(hn, bsz, seq, i, ev_w_in_t, ev_w_out, gdn_conv_w[i], gdn_a_log[i], gdn_dt_bias[i],
                            gdn_norm_w[i], s5_lam_re[i], s5_lam_im[i], s5_b_re[i], s5_b_im[i], s5_c_re[i],
                            s5_c_im[i], s5_d[i], s5_log_step[i], s5_w_glu[i], h, prec)
        else:
            h = _odd_mixer(hn, bsz, seq, i, od_w_in_t, od_w_out, rwkv_shift_mu[i], rwkv_w0[i], rwkv_w2[i],
                           rwkv_a0[i], rwkv_a2[i], rwkv_g2[i], rwkv_k_k[i], rwkv_k_a[i], rwkv_r_k[i],
                           rwkv_ln_w[i], rwkv_ln_b[i], h, prec)
        hn = _rmsnorm(h, norm_ffn[layer], BF16)
        act = _ffn_up(hn, ffn_w_up, ffn_conv_w, layer, seq)
        h = _matmul_rows(act, w_down, layer, h)
    return _rmsnorm(h, norm_final, x.dtype).reshape(bsz, seq, d)
```
